```python
import jax, jax.numpy as jnp
from jax import lax
import numpy as np

D_MODEL = 1024
BATCH = 8
SEQ = 8192
DEPTH = 2

HEAD_DIM = 64
ATTN_WIDTH = D_MODEL // 2
N_ATTN_HEADS = ATTN_WIDTH // HEAD_DIM
N_KV_HEADS = 2
GQA_GROUP = N_ATTN_HEADS // N_KV_HEADS
KV_WIDTH = N_KV_HEADS * HEAD_DIM
N_BRANCH = 3
MLP_WIDTH = D_MODEL - ATTN_WIDTH
MLP_GROUP_DIM = 64
N_MLP_GROUPS = MLP_WIDTH // MLP_GROUP_DIM
MIX_WIDTH = ATTN_WIDTH + MLP_WIDTH
IN_SIZES = (ATTN_WIDTH, KV_WIDTH, KV_WIDTH, KV_WIDTH, KV_WIDTH, KV_WIDTH, KV_WIDTH,
            N_BRANCH * N_ATTN_HEADS, MLP_WIDTH, MLP_WIDTH)
IN_WIDTH = sum(IN_SIZES)
CMP_LEN = 32
CMP_STRIDE = 16
CMP_HIDDEN = 256
SLC_LEN = 64
SLC_TOPK = 16
WINDOW = 512
QBLK = 64
CHUNK = 128
D_FF = -(-8 * D_MODEL // (3 * 256)) * 256
ALPHA = (2.0 * DEPTH) ** 0.25
BETA = (8.0 * DEPTH) ** -0.25
LN_EPS = 1e-5
NEG = -1e30
FORCED_SCORE = 1e6

kernel_name = "hymba_nsa_gmlp_deepnorm_adaln"


def layer_norm(x, g, b):
    xf = x.astype(jnp.float32)
    mu = xf.mean(-1, keepdims=True)
    var = jnp.square(xf - mu).mean(-1, keepdims=True)
    return ((xf - mu) * lax.rsqrt(var + LN_EPS) * g + b).astype(x.dtype)


def rms_norm(x, g):
    xf = x.astype(jnp.float32)
    return (xf * lax.rsqrt(jnp.square(xf).mean(-1, keepdims=True) + LN_EPS) * g).astype(x.dtype)


def masked_softmax(s, mask):
    return jax.nn.softmax(jnp.where(mask, s.astype(jnp.float32), NEG), axis=-1)


def alibi_slopes():
    h = jnp.arange(1, N_ATTN_HEADS + 1, dtype=jnp.float32)
    return jnp.exp2(-8.0 * h / N_ATTN_HEADS)


def compress(tok, pos, w1, w2):
    B, S = tok.shape[:2]
    ch = tok.reshape(B, S // CMP_STRIDE, CMP_STRIDE, N_KV_HEADS, HEAD_DIM)
    blk = jnp.concatenate([ch[:, :-1], ch[:, 1:]], axis=2)
    blk = blk + pos[None, None, :, None, :]
    nc = blk.shape[1]
    blk = jnp.moveaxis(blk, 3, 2).reshape(B, nc, N_KV_HEADS, CMP_LEN * HEAD_DIM)
    return jax.nn.gelu(blk @ w1) @ w2


def nsa_mixer(q, k_cmp, v_cmp, k_slc, v_slc, k_win, v_win, gates):
    B, S = q.shape[:2]
    nc = k_cmp.shape[1]
    nslc = S // SLC_LEN
    topk = min(SLC_TOPK, nslc)
    nqb = S // QBLK
    q = (q * HEAD_DIM ** -0.5).reshape(B, S, N_KV_HEADS, GQA_GROUP, HEAD_DIM)
    gates = gates.reshape(B, S, N_KV_HEADS, GQA_GROUP, N_BRANCH)
    slopes = alibi_slopes().reshape(N_KV_HEADS, GQA_GROUP)
    cstart = CMP_STRIDE * jnp.arange(nc)
    cmp_end = cstart + CMP_LEN - 1
    sstart = SLC_LEN * jnp.arange(nslc)
    overlap = ((cstart[:, None] <= sstart[None, :] + SLC_LEN - 1)
               & (cmp_end[:, None] >= sstart[None, :])).astype(jnp.float32)
    ks_t = jnp.transpose(k_slc, (0, 2, 1, 3))
    vs_t = jnp.transpose(v_slc, (0, 2, 1, 3))
    pad = ((0, 0), (WINDOW, 0), (0, 0), (0, 0))
    kw_pad = jnp.pad(k_win, pad)
    vw_pad = jnp.pad(v_win, pad)
    jj = jnp.arange(nslc)
    n_sel = topk * SLC_LEN

    def block(i):
        t0 = i * QBLK
        t = t0 + jnp.arange(QBLK)
        qb = lax.dynamic_slice_in_dim(q, t0, QBLK, axis=1)
        gb = lax.dynamic_slice_in_dim(gates, t0, QBLK, axis=1)
        d_c = (t[:, None] - cmp_end[None, :]).astype(jnp.float32)
        ok_c = d_c >= 0
        s_c = jnp.einsum('bqkgd,bnkd->bkgqn', qb, k_cmp).astype(jnp.float32) \
            - slopes[:, :, None, None] * d_c
        p_c = masked_softmax(s_c, ok_c) * ok_c.any(-1)[:, None]
        o_c = jnp.einsum('bkgqn,bnkd->bqkgd', p_c.astype(v_cmp.dtype), v_cmp)
        imp = jnp.einsum('bkgqn,nj->bkqj', p_c, overlap)
        cur = t // SLC_LEN
        ok_j = sstart[None, :] <= t[:, None]
        forced = (jj[None, :] == 0) | (jj[None, :] == cur[:, None]) | (jj[None, :] == cur[:, None] - 1)
        score = jnp.where(ok_j & forced, FORCED_SCORE, jnp.where(ok_j, imp, NEG))
        _, sel = lax.top_k(score, topk)
        tok = (sel[..., None] * SLC_LEN + jnp.arange(SLC_LEN)).reshape(B, N_KV_HEADS, QBLK * n_sel)
        kg = jnp.take_along_axis(ks_t, tok[..., None], axis=2).reshape(B, N_KV_HEADS, QBLK, n_sel, HEAD_DIM)
        vg = jnp.take_along_axis(vs_t, tok[..., None], axis=2).reshape(B, N_KV_HEADS, QBLK, n_sel, HEAD_DIM)
        tok = tok.reshape(B, N_KV_HEADS, QBLK, n_sel)
        d_s = (t[None, None, :, None] - tok).astype(jnp.float32)[:, :, None]
        s_s = jnp.einsum('bqkgd,bkqld->bkgql', qb, kg).astype(jnp.float32) \
            - slopes[None, :, :, None, None] * d_s
        p_s = masked_softmax(s_s, d_s >= 0)
        o_s = jnp.einsum('bkgql,bkqld->bqkgd', p_s.astype(vg.dtype), vg)
        kwb = lax.dynamic_slice_in_dim(kw_pad, t0, QBLK + WINDOW, axis=1)
        vwb = lax.dynamic_slice_in_dim(vw_pad, t0, QBLK + WINDOW, axis=1)
        spos = t0 - WINDOW + jnp.arange(QBLK + WINDOW)
        d_w = t[:, None] - spos[None, :]
        ok_w = (d_w >= 0) & (d_w < WINDOW) & (spos[None, :] >= 0)
        s_w = jnp.einsum('bqkgd,bskd->bkgqs', qb, kwb).astype(jnp.float32) \
            - slopes[:, :, None, None] * d_w.astype(jnp.float32)
        p_w = masked_softmax(s_w, ok_w)
        o_w = jnp.einsum('bkgqs,bskd->bqkgd', p_w.astype(vwb.dtype), vwb)
        g = jax.nn.sigmoid(gb.astype(jnp.float32))
        o = g[..., 0:1] * o_c + g[..., 1:2] * o_s + g[..., 2:3] * o_w
        return o.astype(q.dtype)

    out = lax.map(block, jnp.arange(nqb))
    return jnp.moveaxis(out, 0, 1).reshape(B, S, ATTN_WIDTH)


def gmlp_mixer(u, v, vn_g, vn_b, w_s, b_s):
    B, S, _ = u.shape
    v = v.reshape(B, S, N_MLP_GROUPS, MLP_GROUP_DIM)
    v = layer_norm(v, vn_g.reshape(N_MLP_GROUPS, MLP_GROUP_DIM), vn_b.reshape(N_MLP_GROUPS, MLP_GROUP_DIM))
    v = v.reshape(B, S // CHUNK, CHUNK, N_MLP_GROUPS, MLP_GROUP_DIM)
    causal = jnp.tril(jnp.ones((CHUNK, CHUNK), dtype=bool))
    w = jnp.where(causal[None], w_s, 0)
    sv = jnp.einsum('gts,bnsgd->bntgd', w, v) + b_s.T[:, :, None]
    return u * sv.reshape(B, S, MLP_WIDTH)


def hybrid_layer(x, c_act, w_ada, b_ada, w_in, cmp_pos, cmp_w1, cmp_w2, vn_g, vn_b,
                 w_s, b_s, out_g, w_o, ln1_g, ln1_b, w1, w3, w2, ln2_g, ln2_b):
    B, S, _ = x.shape
    mod = (c_act @ w_ada + b_ada)[:, None, :]
    sh1, sc1, g1, sh2, sc2, g2 = jnp.split(mod, 6, axis=-1)
    h = x * (1 + sc1) + sh1
    proj = h @ w_in
    offs = np.cumsum(np.array(IN_SIZES))[:-1].tolist()
    q, kc, vc, ksl, vsl, kwn, vwn, gt, u, v = jnp.split(proj, offs, axis=-1)
    kvs = lambda a: a.reshape(B, S, N_KV_HEADS, HEAD_DIM)
    k_cmp = compress(kvs(kc), cmp_pos[0], cmp_w1[0], cmp_w2[0])
    v_cmp = compress(kvs(vc), cmp_pos[1], cmp_w1[1], cmp_w2[1])
    o_attn = nsa_mixer(q.reshape(B, S, N_ATTN_HEADS, HEAD_DIM), k_cmp, v_cmp,
                       kvs(ksl), kvs(vsl), kvs(kwn), kvs(vwn),
                       gt.reshape(B, S, N_ATTN_HEADS, N_BRANCH))
    o_mlp = gmlp_mixer(jax.nn.gelu(u), jax.nn.gelu(v), vn_g, vn_b, w_s, b_s)
    y = jnp.concatenate([o_attn, o_mlp], axis=-1).reshape(B, S, MIX_WIDTH // HEAD_DIM, HEAD_DIM)
    y = rms_norm(y, out_g.reshape(MIX_WIDTH // HEAD_DIM, HEAD_DIM)).reshape(B, S, MIX_WIDTH) @ w_o
    x = layer_norm(ALPHA * x + (1 + g1) * y, ln1_g, ln1_b)
    h = x * (1 + sc2) + sh2
    f = (jax.nn.silu(h @ w1) * (h @ w3)) @ w2
    return layer_norm(ALPHA * x + (1 + g2) * f, ln2_g, ln2_b)


def setup_inputs(seed: int = 0) -> dict:
    key = jax.random.key(seed)
    ks = jax.random.split(key, 21)
    L = DEPTH
    nrm = lambda k, shape, s: jax.random.normal(k, shape, jnp.float32) * s
    return {
        "x": nrm(ks[0], (BATCH, SEQ, D_MODEL), 1.0),
        "c": nrm(ks[1], (BATCH, D_MODEL), 1.0),
        "w_ada": nrm(ks[2], (L, D_MODEL, 6 * D_MODEL), 0.1 * D_MODEL ** -0.5),
        "b_ada": nrm(ks[3], (L, 6 * D_MODEL), 0.02),
        "w_in": nrm(ks[4], (L, D_MODEL, IN_WIDTH), D_MODEL ** -0.5),
        "cmp_pos": nrm(ks[5], (L, 2, CMP_LEN, HEAD_DIM), 0.02),
        "cmp_w1": nrm(ks[6], (L, 2, CMP_LEN * HEAD_DIM, CMP_HIDDEN), (CMP_LEN * HEAD_DIM) ** -0.5),
        "cmp_w2": nrm(ks[7], (L, 2, CMP_HIDDEN, HEAD_DIM), CMP_HIDDEN ** -0.5),
        "vn_g": 1.0 + nrm(ks[8], (L, MLP_WIDTH), 0.02),
        "vn_b": nrm(ks[9], (L, MLP_WIDTH), 0.02),
        "w_s": nrm(ks[10], (L, N_MLP_GROUPS, CHUNK, CHUNK), CHUNK ** -0.5),
        "b_s": 1.0 + nrm(ks[11], (L, N_MLP_GROUPS, CHUNK), 0.02),
        "out_g": 1.0 + nrm(ks[12], (L, MIX_WIDTH), 0.02),
        "w_o": nrm(ks[13], (L, MIX_WIDTH, D_MODEL), BETA * MIX_WIDTH ** -0.5),
        "ln1_g": 1.0 + nrm(ks[14], (L, D_MODEL), 0.02),
        "ln1_b": nrm(ks[15], (L, D_MODEL), 0.02),
        "w1": nrm(ks[16], (L, D_MODEL, D_FF), D_MODEL ** -0.5),
        "w3": nrm(ks[17], (L, D_MODEL, D_FF), D_MODEL ** -0.5),
        "w2": nrm(ks[18], (L, D_FF, D_MODEL), BETA * D_FF ** -0.5),
        "ln2_g": 1.0 + nrm(ks[19], (L, D_MODEL), 0.02),
        "ln2_b": nrm(ks[20], (L, D_MODEL), 0.02),
    }


def reference(x, c, w_ada, b_ada, w_in, cmp_pos, cmp_w1, cmp_w2, vn_g, vn_b, w_s, b_s,
              out_g, w_o, ln1_g, ln1_b, w1, w3, w2, ln2_g, ln2_b):
    c_act = jax.nn.silu(c)
    for l in range(DEPTH):
        x = hybrid_layer(x, c_act, w_ada[l], b_ada[l], w_in[l], cmp_pos[l], cmp_w1[l], cmp_w2[l],
                         vn_g[l], vn_b[l], w_s[l], b_s[l], out_g[l], w_o[l],
                         ln1_g[l], ln1_b[l], w1[l], w3[l], w2[l], ln2_g[l], ln2_b[l])
    return x
```

```python
import functools
import math

import jax
import jax.numpy as jnp
from jax import lax
from jax.experimental import pallas as pl
from jax.experimental.pallas import tpu as pltpu

F32 = jnp.float32
BF16 = jnp.bfloat16

HEAD_DIM = 64
N_KV_HEADS = 2
GQA_GROUP = 4
N_ATTN_HEADS = N_KV_HEADS * GQA_GROUP
ATTN_WIDTH = N_ATTN_HEADS * HEAD_DIM
KV_WIDTH = N_KV_HEADS * HEAD_DIM
N_BRANCH = 3
MLP_GROUP_DIM = 64
N_MLP_GROUPS = 8
MLP_WIDTH = N_MLP_GROUPS * MLP_GROUP_DIM
CMP_LEN = 32
CMP_STRIDE = 16
CMP_HIDDEN = 256
SLC_LEN = 64
SLC_TOPK = 16
WINDOW = 512
CHUNK = 128
LN_EPS = 1e-5
NEG = -1e30
FORCED_SCORE = 1e6

LANES = 128
MAX_SLC_BLOCKS = LANES
TOK_TILE = 512
Q_TILE = 256
VMEM_LIMIT = 56 * 1024 * 1024

QKV_WIDTH = ATTN_WIDTH + 4 * KV_WIDTH
COLBLK_KSLC = ATTN_WIDTH // LANES
COLBLK_VSLC = COLBLK_KSLC + 1
COLBLK_KWIN = COLBLK_KSLC + 2
COLBLK_VWIN = COLBLK_KSLC + 3
GATE_ROWS = 16


def _nt(a, b):
    return lax.dot_general(a, b, (((1,), (1,)), ((), ())), preferred_element_type=F32)


def _tn(a, b):
    return lax.dot_general(a, b, (((0,), (0,)), ((), ())), preferred_element_type=F32)


def _nn(a, b):
    return jnp.dot(a, b, preferred_element_type=F32)


def _gelu(x):
    return x * (0.5 * (1.0 + jnp.tanh(math.sqrt(2.0 / math.pi) * (x + 0.044715 * (x * x * x)))))


def _sigmoid(x):
    return 1.0 / (1.0 + jnp.exp(-x))


def _layer_norm_rows(z, g, b):
    mu = jnp.mean(z, axis=-1, keepdims=True)
    zc = z - mu
    var = jnp.mean(zc * zc, axis=-1, keepdims=True)
    return zc * lax.rsqrt(var + LN_EPS) * g + b


def _slope(kv_head, g):
    val = jnp.float32(2.0 ** (-8.0 * (g + 1) / N_ATTN_HEADS))
    for k in range(1, N_KV_HEADS):
        val = jnp.where(kv_head == k, jnp.float32(2.0 ** (-8.0 * (k * GQA_GROUP + g + 1) / N_ATTN_HEADS)), val)
    return val


def _pick_head(x, kv_head):
    out = x[:, :HEAD_DIM]
    for k in range(1, N_KV_HEADS):
        out = jnp.where(kv_head == k, x[:, k * HEAD_DIM:(k + 1) * HEAD_DIM], out)
    return out


def _params(*sem):
    return pltpu.CompilerParams(dimension_semantics=sem, vmem_limit_bytes=VMEM_LIMIT)


def _ada_kernel(c_ref, w_ref, b_ref, o_ref):
    c = c_ref[...]
    c_act = (c * _sigmoid(c)).astype(BF16)
    o_ref[0] = _nn(c_act, w_ref[0].astype(BF16)) + b_ref[0]


def _ada_call(c, w_ada, b_ada):
    L, D, D6 = w_ada.shape
    B = c.shape[0]
    return pl.pallas_call(
        _ada_kernel,
        grid=(L, D6 // D),
        in_specs=[pl.BlockSpec((B, D), lambda l, n: (0, 0)),
                  pl.BlockSpec((1, D, D), lambda l, n: (l, 0, n)),
                  pl.BlockSpec((1, 1, D), lambda l, n: (l, 0, n))],
        out_specs=pl.BlockSpec((1, B, D), lambda l, n: (l, 0, n)),
        out_shape=jax.ShapeDtypeStruct((L, B, D6), F32),
        compiler_params=_params("arbitrary", "arbitrary"),
        name="ada_mod",
    )(c, w_ada, b_ada.reshape(L, 1, D6))


def _in_proj_kernel(x_ref, sc_ref, sh_ref, wa_ref, wg_ref, wuv_ref, vng_ref, vnb_ref, wst_ref, bs_ref,
                    qkv_ref, kvc_ref, gates_ref, mlp_ref):
    tm = x_ref.shape[0]
    h = (x_ref[...] * (1.0 + sc_ref[...]) + sh_ref[...]).astype(BF16)
    a = _nn(h, wa_ref[...])
    qkv_ref[:, :ATTN_WIDTH] = (a[:, :ATTN_WIDTH] * (HEAD_DIM ** -0.5)).astype(BF16)
    qkv_ref[:, ATTN_WIDTH:] = a[:, ATTN_WIDTH:QKV_WIDTH].astype(BF16)
    kvc_ref[0] = a[:, QKV_WIDTH:QKV_WIDTH + KV_WIDTH].astype(BF16)
    kvc_ref[1] = a[:, QKV_WIDTH + KV_WIDTH:].astype(BF16)
    gates_ref[...] = _nt(wg_ref[...], h)

    uv = _nt(wuv_ref[...], h)
    u = _gelu(uv[:MLP_WIDTH])
    v = _gelu(uv[MLP_WIDTH:]).reshape(N_MLP_GROUPS, MLP_GROUP_DIM, tm)
    mu = jnp.mean(v, axis=1, keepdims=True)
    vc = v - mu
    var = jnp.mean(vc * vc, axis=1, keepdims=True)
    vn = vc * lax.rsqrt(var + LN_EPS) * vng_ref[...] + vnb_ref[...]
    n_chunks = tm // CHUNK
    s_idx = lax.broadcasted_iota(jnp.int32, (CHUNK, CHUNK), 0)
    t_idx = lax.broadcasted_iota(jnp.int32, (CHUNK, CHUNK), 1)
    for g in range(N_MLP_GROUPS):
        vg = vn[g].astype(BF16)
        stack = jnp.concatenate([vg[:, c * CHUNK:(c + 1) * CHUNK] for c in range(n_chunks)], axis=0)
        w_t = jnp.where(s_idx <= t_idx, wst_ref[g], 0.0).astype(BF16)
        sv = _nn(stack, w_t)
        bias = bs_ref[g:g + 1, :]
        for c in range(n_chunks):
            mlp_ref[g * MLP_GROUP_DIM:(g + 1) * MLP_GROUP_DIM, c * CHUNK:(c + 1) * CHUNK] = (
                u[g * MLP_GROUP_DIM:(g + 1) * MLP_GROUP_DIM, c * CHUNK:(c + 1) * CHUNK]
                * (sv[c * MLP_GROUP_DIM:(c + 1) * MLP_GROUP_DIM] + bias))


def _in_proj_call(x, sc, sh, w_a, w_gt, w_uvt, vn_g, vn_b, w_st, b_s):
    B, S, D = x.shape
    tm = min(TOK_TILE, S)
    const2 = lambda b, i: (0, 0)
    const3 = lambda b, i: (0, 0, 0)
    return pl.pallas_call(
        _in_proj_kernel,
        grid=(B, S // tm),
        in_specs=[pl.BlockSpec((None, tm, D), lambda b, i: (b, i, 0)),
                  pl.BlockSpec((None, 1, D), lambda b, i: (b, 0, 0)),
                  pl.BlockSpec((None, 1, D), lambda b, i: (b, 0, 0)),
                  pl.BlockSpec(w_a.shape, const2),
                  pl.BlockSpec(w_gt.shape, const2),
                  pl.BlockSpec(w_uvt.shape, const2),
                  pl.BlockSpec(vn_g.shape, const3),
                  pl.BlockSpec(vn_b.shape, const3),
                  pl.BlockSpec(w_st.shape, const3),
                  pl.BlockSpec(b_s.shape, const2)],
        out_specs=[pl.BlockSpec((None, tm, QKV_WIDTH), lambda b, i: (b, i, 0)),
                   pl.BlockSpec((2, None, tm, KV_WIDTH), lambda b, i: (0, b, i, 0)),
                   pl.BlockSpec((None, N_KV_HEADS * GATE_ROWS, tm), lambda b, i: (b, 0, i)),
                   pl.BlockSpec((None, MLP_WIDTH, tm), lambda b, i: (b, 0, i))],
        out_shape=[jax.ShapeDtypeStruct((B, S, QKV_WIDTH), BF16),
                   jax.ShapeDtypeStruct((2, B, S, KV_WIDTH), BF16),
                   jax.ShapeDtypeStruct((B, N_KV_HEADS * GATE_ROWS, S), F32),
                   jax.ShapeDtypeStruct((B, MLP_WIDTH, S), F32)],
        compiler_params=_params("arbitrary", "arbitrary"),
        name="in_proj",
    )(x, sc, sh, w_a, w_gt, w_uvt, vn_g, vn_b, w_st, b_s)


def _compress_kernel(tok_ref, wbig_ref, w1_ref, pos_ref, w2_ref, out_ref):
    ab = _nn(tok_ref[...], wbig_ref[...])
    bias = _nn(pos_ref[...].astype(BF16), w1_ref[...])[0:1]
    ngrp = ab.shape[0]
    for h in range(N_KV_HEADS):
        base = h * 2 * CMP_HIDDEN
        first = ab[:, base:base + CMP_HIDDEN]
        second = ab[:, base + CMP_HIDDEN:base + 2 * CMP_HIDDEN]
        hid = _gelu(first + pltpu.roll(second, ngrp - 1, 0) + bias)
        out_ref[h] = _nn(hid.astype(BF16), w2_ref[...]).astype(BF16)


def _compress_call(tok, wbig, w1, pos, w2):
    two, B, ngrp, flat = tok.shape
    return pl.pallas_call(
        _compress_kernel,
        grid=(two, B),
        in_specs=[pl.BlockSpec((None, None, ngrp, flat), lambda t, b: (t, b, 0, 0)),
                  pl.BlockSpec((None,) + wbig.shape[1:], lambda t, b: (t, 0, 0)),
                  pl.BlockSpec((None,) + w1.shape[1:], lambda t, b: (t, 0, 0)),
                  pl.BlockSpec((None,) + pos.shape[1:], lambda t, b: (t, 0, 0)),
                  pl.BlockSpec((None,) + w2.shape[1:], lambda t, b: (t, 0, 0))],
        out_specs=pl.BlockSpec((None, None, N_KV_HEADS, ngrp, HEAD_DIM), lambda t, b: (t, b, 0, 0, 0)),
        out_shape=jax.ShapeDtypeStruct((two, B, N_KV_HEADS, ngrp, HEAD_DIM), BF16),
        compiler_params=_params("arbitrary", "arbitrary"),
        name="compress",
    )(tok, wbig, w1, pos, w2)


def _cmp_attn_kernel(q_ref, kc_ref, vc_ref, gates_ref, oc_ref, bias_ref, *, n_slc, topk):
    k_head = pl.program_id(1)
    i = pl.program_id(2)
    tq = q_ref.shape[0]
    ncmp = kc_ref.shape[0]
    t0 = i * tq
    q = q_ref[...]
    kc = kc_ref[...]
    vc = vc_ref[...]
    t_row = t0 + lax.broadcasted_iota(jnp.int32, (tq, ncmp), 0)
    cmp_end = CMP_STRIDE * lax.broadcasted_iota(jnp.int32, (tq, ncmp), 1) + (CMP_LEN - 1)
    dist = (t_row - cmp_end).astype(F32)
    ok = dist >= 0.0
    any_ok = (t0 + lax.broadcasted_iota(jnp.int32, (tq, 1), 0)) >= (CMP_LEN - 1)
    p_sum = jnp.zeros((tq, ncmp), F32)
    for g in range(GQA_GROUP):
        s = _nt(q[:, g * HEAD_DIM:(g + 1) * HEAD_DIM], kc) - _slope(k_head, g) * dist
        s = jnp.where(ok, s, NEG)
        e = jnp.exp(s - jnp.max(s, axis=-1, keepdims=True))
        p = e / jnp.sum(e, axis=-1, keepdims=True)
        p = jnp.where(any_ok, p, 0.0)
        p_sum = p_sum + p
        o = _nn(p.astype(BF16), vc)
        gate = _sigmoid(gates_ref[g:g + 1, :])
        oc_ref[g * HEAD_DIM:(g + 1) * HEAD_DIM, :] = o.T * gate

    n_i = lax.broadcasted_iota(jnp.int32, (ncmp, MAX_SLC_BLOCKS), 0)
    j_i = lax.broadcasted_iota(jnp.int32, (ncmp, MAX_SLC_BLOCKS), 1)
    overlap = ((CMP_STRIDE * n_i <= SLC_LEN * j_i + (SLC_LEN - 1))
               & (CMP_STRIDE * n_i + (CMP_LEN - 1) >= SLC_LEN * j_i)).astype(BF16)
    p_hi = p_sum.astype(BF16)
    r1 = p_sum - p_hi.astype(F32)
    p_mid = r1.astype(BF16)
    p_lo = (r1 - p_mid.astype(F32)).astype(BF16)
    imp = _nn(p_hi, overlap) + _nn(p_mid, overlap) + _nn(p_lo, overlap)

    score = imp.T
    j_row = lax.broadcasted_iota(jnp.int32, (MAX_SLC_BLOCKS, tq), 0)
    t_col = t0 + lax.broadcasted_iota(jnp.int32, (MAX_SLC_BLOCKS, tq), 1)
    cur = t_col // SLC_LEN
    valid = (SLC_LEN * j_row <= t_col) & (j_row < n_slc)
    forced = (j_row == 0) | (j_row == cur) | (j_row == cur - 1)
    score = jnp.where(valid & forced, FORCED_SCORE, jnp.where(valid, score, NEG))
    score = jnp.where(j_row < n_slc, score, -jnp.inf)
    j_f = j_row.astype(F32)
    sel = jnp.zeros((MAX_SLC_BLOCKS, tq), jnp.bool_)
    for _ in range(topk):
        m = jnp.max(score, axis=0, keepdims=True)
        first = jnp.min(jnp.where(score == m, j_f, float(MAX_SLC_BLOCKS)), axis=0, keepdims=True)
        hit = j_f == first
        sel = sel | hit
        score = jnp.where(hit, -jnp.inf, score)
    bias_t = jnp.where(sel & valid, 0.0, NEG)
    bias_ref[...] = bias_t.T.astype(BF16)


def _cmp_attn_call(qkv, cmp_kv, gates, *, n_slc, topk):
    B, S, _ = qkv.shape
    ncmp = cmp_kv.shape[3]
    tq = min(Q_TILE, S)
    kern = functools.partial(_cmp_attn_kernel, n_slc=n_slc, topk=topk)
    return pl.pallas_call(
        kern,
        grid=(B, N_KV_HEADS, S // tq),
        in_specs=[pl.BlockSpec((None, tq, GQA_GROUP * HEAD_DIM), lambda b, k, i: (b, i, k)),
                  pl.BlockSpec((None, None, None, ncmp, HEAD_DIM), lambda b, k, i: (0, b, k, 0, 0)),
                  pl.BlockSpec((None, None, None, ncmp, HEAD_DIM), lambda b, k, i: (1, b, k, 0, 0)),
                  pl.BlockSpec((None, None, GATE_ROWS, tq), lambda b, k, i: (b, k, 0, i))],
        out_specs=[pl.BlockSpec((None, GQA_GROUP * HEAD_DIM, tq), lambda b, k, i: (b, k, i)),
                   pl.BlockSpec((None, None, tq, MAX_SLC_BLOCKS), lambda b, k, i: (b, k, i, 0))],
        out_shape=[jax.ShapeDtypeStruct((B, ATTN_WIDTH, S), F32),
                   jax.ShapeDtypeStruct((B, N_KV_HEADS, S, MAX_SLC_BLOCKS), BF16)],
        compiler_params=_params("arbitrary", "arbitrary", "arbitrary"),
        name="cmp_attn",
    )(qkv, cmp_kv, cmp_kv, gates)


_AUX = HEAD_DIM


def _slc_attn_kernel(q_ref, bias_ref, ks_ref, vs_ref, gates_ref, os_ref, kaug_ref, qaug_ref, m_ref, l_ref, acc_ref):
    k_head = pl.program_id(1)
    i = pl.program_id(2)
    tq = q_ref.shape[0]
    S = ks_ref.shape[0]
    t0 = i * tq

    @pl.when(i == 0)
    def _build_keys():
        rows = min(S, 512)

        def fill(c, carry):
            r0 = pl.multiple_of(c * rows, rows)
            pos = r0 + lax.broadcasted_iota(jnp.int32, (rows, HEAD_DIM), 0)
            lane = lax.broadcasted_iota(jnp.int32, (rows, HEAD_DIM), 1)
            aux = jnp.where(lane == 0, pos // LANES, jnp.where(lane == 1, pos % LANES, jnp.where(lane == 2, 1, 0)))
            kaug_ref[pl.ds(r0, rows), 0:HEAD_DIM] = _pick_head(ks_ref[pl.ds(r0, rows), :], k_head)
            kaug_ref[pl.ds(r0, rows), _AUX:_AUX + HEAD_DIM] = aux.astype(F32).astype(BF16)
            blk = (r0 + lax.broadcasted_iota(jnp.int32, (rows, MAX_SLC_BLOCKS), 0)) // SLC_LEN
            hot = blk == lax.broadcasted_iota(jnp.int32, (rows, MAX_SLC_BLOCKS), 1)
            kaug_ref[pl.ds(r0, rows), 2 * HEAD_DIM:] = jnp.where(hot, 1.0, 0.0).astype(BF16)
            return carry

        lax.fori_loop(0, S // rows, fill, 0)

    q = q_ref[...]
    lane = lax.broadcasted_iota(jnp.int32, (tq, HEAD_DIM), 1)
    t0f = t0.astype(F32)
    for g in range(GQA_GROUP):
        slope = _slope(k_head, g)
        aux = jnp.where(lane == 0, slope * LANES, jnp.where(lane == 1, slope, jnp.where(lane == 2, -slope * t0f, 0.0)))
        qaug_ref[g * tq:(g + 1) * tq, 0:HEAD_DIM] = q[:, g * HEAD_DIM:(g + 1) * HEAD_DIM]
        qaug_ref[g * tq:(g + 1) * tq, _AUX:_AUX + HEAD_DIM] = aux.astype(BF16)
        qaug_ref[g * tq:(g + 1) * tq, 2 * HEAD_DIM:] = bias_ref[...]
    m_ref[...] = jnp.full(m_ref.shape, NEG, F32)
    l_ref[...] = jnp.zeros(l_ref.shape, F32)
    acc_ref[...] = jnp.zeros(acc_ref.shape, F32)

    def step(j, causal):
        r0 = pl.multiple_of(j * tq, tq)
        s = _nt(qaug_ref[...], kaug_ref[pl.ds(r0, tq), :])
        if causal:
            row = lax.broadcasted_iota(jnp.int32, (tq, tq), 0)
            col = lax.broadcasted_iota(jnp.int32, (tq, tq), 1)
            keep = jnp.concatenate([col <= row] * GQA_GROUP, axis=0)
            s = jnp.where(keep, s, NEG)
        m_old = m_ref[...]
        m_new = jnp.maximum(m_old, jnp.max(s, axis=-1, keepdims=True))
        alpha = jnp.exp(m_old - m_new)
        p = jnp.exp(s - m_new)
        l_ref[...] = alpha * l_ref[...] + jnp.sum(p, axis=-1, keepdims=True)
        acc_ref[...] = alpha * acc_ref[...] + _nn(p.astype(BF16), vs_ref[pl.ds(r0, tq), :])
        m_ref[...] = m_new

    def body(j, carry):
        step(j, False)
        return carry

    lax.fori_loop(0, i, body, 0)
    step(i, True)

    o = _pick_head(acc_ref[...] / l_ref[...], k_head)
    for g in range(GQA_GROUP):
        gate = _sigmoid(gates_ref[GQA_GROUP + g:GQA_GROUP + g + 1, :])
        os_ref[g * HEAD_DIM:(g + 1) * HEAD_DIM, :] = o[g * tq:(g + 1) * tq].T * gate


def _slc_attn_call(qkv, bias, gates):
    B, S, _ = qkv.shape
    tq = min(Q_TILE, S)
    return pl.pallas_call(
        _slc_attn_kernel,
        grid=(B, N_KV_HEADS, S // tq),
        in_specs=[pl.BlockSpec((None, tq, GQA_GROUP * HEAD_DIM), lambda b, k, i: (b, i, k)),
                  pl.BlockSpec((None, None, tq, MAX_SLC_BLOCKS), lambda b, k, i: (b, k, i, 0)),
                  pl.BlockSpec((None, S, KV_WIDTH), lambda b, k, i: (b, 0, COLBLK_KSLC)),
                  pl.BlockSpec((None, S, KV_WIDTH), lambda b, k, i: (b, 0, COLBLK_VSLC)),
                  pl.BlockSpec((None, None, GATE_ROWS, tq), lambda b, k, i: (b, k, 0, i))],
        out_specs=pl.BlockSpec((None, GQA_GROUP * HEAD_DIM, tq), lambda b, k, i: (b, k, i)),
        out_shape=jax.ShapeDtypeStruct((B, ATTN_WIDTH, S), F32),
        scratch_shapes=[pltpu.VMEM((S, 2 * LANES), BF16),
                        pltpu.VMEM((GQA_GROUP * tq, 2 * LANES), BF16),
                        pltpu.VMEM((GQA_GROUP * tq, 1), F32),
                        pltpu.VMEM((GQA_GROUP * tq, 1), F32),
                        pltpu.VMEM((GQA_GROUP * tq, KV_WIDTH), F32)],
        compiler_params=_params("arbitrary", "arbitrary", "arbitrary"),
        name="slc_attn",
    )(qkv, bias, qkv, qkv, gates)


def _win_attn_kernel(q_ref, k2_ref, k1_ref, k0_ref, v2_ref, v1_ref, v0_ref, gates_ref, ow_ref):
    k_head = pl.program_id(1)
    i = pl.program_id(2)
    tq = q_ref.shape[0]
    q = q_ref[...]
    row = lax.broadcasted_iota(jnp.int32, (tq, tq), 0)
    col = lax.broadcasted_iota(jnp.int32, (tq, tq), 1)
    rel = (row - col).astype(F32)
    keep2 = (row < col) & (i >= 2)
    keep1 = i >= 1
    keep0 = col <= row
    ks = [_pick_head(r[...], k_head) for r in (k2_ref, k1_ref, k0_ref)]
    vs = [_pick_head(r[...], k_head) for r in (v2_ref, v1_ref, v0_ref)]
    for g in range(GQA_GROUP):
        slope = _slope(k_head, g)
        qg = q[:, g * HEAD_DIM:(g + 1) * HEAD_DIM]
        s2 = jnp.where(keep2, _nt(qg, ks[0]) - slope * (rel + 2.0 * tq), NEG)
        s1 = jnp.where(keep1, _nt(qg, ks[1]) - slope * (rel + 1.0 * tq), NEG)
        s0 = jnp.where(keep0, _nt(qg, ks[2]) - slope * rel, NEG)
        m = jnp.maximum(jnp.maximum(jnp.max(s2, axis=-1, keepdims=True), jnp.max(s1, axis=-1, keepdims=True)),
                        jnp.max(s0, axis=-1, keepdims=True))
        p2 = jnp.exp(s2 - m)
        p1 = jnp.exp(s1 - m)
        p0 = jnp.exp(s0 - m)
        l = (jnp.sum(p2, axis=-1, keepdims=True) + jnp.sum(p1, axis=-1, keepdims=True)
             + jnp.sum(p0, axis=-1, keepdims=True))
        o = (_nn(p2.astype(BF16), vs[0]) + _nn(p1.astype(BF16), vs[1]) + _nn(p0.astype(BF16), vs[2])) / l
        gate = _sigmoid(gates_ref[2 * GQA_GROUP + g:2 * GQA_GROUP + g + 1, :])
        ow_ref[g * HEAD_DIM:(g + 1) * HEAD_DIM, :] = o.T * gate


def _win_attn_call(qkv, gates):
    B, S, _ = qkv.shape
    tq = min(Q_TILE, S)
    assert WINDOW == 2 * tq
    kv_spec = lambda colblk, back: pl.BlockSpec(
        (None, tq, KV_WIDTH), lambda b, k, i: (b, jnp.maximum(i - back, 0), colblk))
    return pl.pallas_call(
        _win_attn_kernel,
        grid=(B, N_KV_HEADS, S // tq),
        in_specs=[pl.BlockSpec((None, tq, GQA_GROUP * HEAD_DIM), lambda b, k, i: (b, i, k)),
                  kv_spec(COLBLK_KWIN, 2), kv_spec(COLBLK_KWIN, 1), kv_spec(COLBLK_KWIN, 0),
                  kv_spec(COLBLK_VWIN, 2), kv_spec(COLBLK_VWIN, 1), kv_spec(COLBLK_VWIN, 0),
                  pl.BlockSpec((None, None, GATE_ROWS, tq), lambda b, k, i: (b, k, 0, i))],
        out_specs=pl.BlockSpec((None, GQA_GROUP * HEAD_DIM, tq), lambda b, k, i: (b, k, i)),
        out_shape=jax.ShapeDtypeStruct((B, ATTN_WIDTH, S), F32),
        compiler_params=_params("arbitrary", "arbitrary", "arbitrary"),
        name="win_attn",
    )(qkv, qkv, qkv, qkv, qkv, qkv, qkv, gates)


def _group_rms(a, gain):
    c, tm = a.shape
    a3 = a.reshape(c // HEAD_DIM, HEAD_DIM, tm)
    ms = jnp.mean(a3 * a3, axis=1, keepdims=True)
    return (a3 * lax.rsqrt(ms + LN_EPS) * gain).reshape(c, tm)


def _out_proj_kernel(oc_ref, os_ref, ow_ref, mlp_ref, x_ref, g1_ref, og_ref, wo_ref, lng_ref, lnb_ref, o_ref, *, alpha):
    n_attn = ATTN_WIDTH // HEAD_DIM
    attn = _group_rms(oc_ref[...] + os_ref[...] + ow_ref[...], og_ref[:n_attn])
    mlp = _group_rms(mlp_ref[...], og_ref[n_attn:])
    y_t = jnp.concatenate([attn, mlp], axis=0).astype(BF16)
    y = _tn(y_t, wo_ref[...])
    z = alpha * x_ref[...] + (1.0 + g1_ref[...]) * y
    o_ref[...] = _layer_norm_rows(z, lng_ref[...], lnb_ref[...])


def _out_proj_call(oc, os_, ow, mlp, x, g1, out_g, w_o, ln_g, ln_b, *, alpha):
    B, S, D = x.shape
    tm = min(TOK_TILE, S)
    cm = lambda width: pl.BlockSpec((None, width, tm), lambda b, i: (b, 0, i))
    row = pl.BlockSpec((1, D), lambda b, i: (0, 0))
    return pl.pallas_call(
        functools.partial(_out_proj_kernel, alpha=alpha),
        grid=(B, S // tm),
        in_specs=[cm(ATTN_WIDTH), cm(ATTN_WIDTH), cm(ATTN_WIDTH), cm(MLP_WIDTH),
                  pl.BlockSpec((None, tm, D), lambda b, i: (b, i, 0)),
                  pl.BlockSpec((None, 1, D), lambda b, i: (b, 0, 0)),
                  pl.BlockSpec(out_g.shape, lambda b, i: (0, 0, 0)),
                  pl.BlockSpec(w_o.shape, lambda b, i: (0, 0)),
                  row, row],
        out_specs=pl.BlockSpec((None, tm, D), lambda b, i: (b, i, 0)),
        out_shape=jax.ShapeDtypeStruct((B, S, D), F32),
        compiler_params=_params("arbitrary", "arbitrary"),
        name="out_proj",
    )(oc, os_, ow, mlp, x, g1, out_g, w_o, ln_g, ln_b)


def _ffn_kernel(x_ref, sc_ref, sh_ref, g2_ref, w1_ref, w3_ref, w2_ref, lng_ref, lnb_ref, o_ref, *, alpha, splits):
    x = x_ref[...]
    h = (x * (1.0 + sc_ref[...]) + sh_ref[...]).astype(BF16)
    f = None
    for lo, hi in splits:
        a = _nn(h, w1_ref[:, lo:hi])
        b = _nn(h, w3_ref[:, lo:hi])
        part = _nn((a * _sigmoid(a) * b).astype(BF16), w2_ref[lo:hi, :])
        f = part if f is None else f + part
    z = alpha * x + (1.0 + g2_ref[...]) * f
    o_ref[...] = _layer_norm_rows(z, lng_ref[...], lnb_ref[...])


def _ffn_call(x, sc, sh, g2, w1, w3, w2, ln_g, ln_b, *, alpha):
    B, S, D = x.shape
    d_ff = w1.shape[1]
    tm = min(TOK_TILE, S)
    half = (d_ff // 2 + 255) // 256 * 256
    splits = ((0, half), (half, d_ff))
    mod = pl.BlockSpec((None, 1, D), lambda b, i: (b, 0, 0))
    row = pl.BlockSpec((1, D), lambda b, i: (0, 0))
    resident = lambda shape: pl.BlockSpec(shape, lambda b, i: (0, 0), pipeline_mode=pl.Buffered(1))
    return pl.pallas_call(
        functools.partial(_ffn_kernel, alpha=alpha, splits=splits),
        grid=(B, S // tm),
        in_specs=[pl.BlockSpec((None, tm, D), lambda b, i: (b, i, 0)), mod, mod, mod,
                  resident(w1.shape), resident(w3.shape), resident(w2.shape), row, row],
        out_specs=pl.BlockSpec((None, tm, D), lambda b, i: (b, i, 0)),
        out_shape=jax.ShapeDtypeStruct((B, S, D), F32),
        compiler_params=_params("arbitrary", "arbitrary"),
        name="ffn",
    )(x, sc, sh, g2, w1, w3, w2, ln_g, ln_b)


def _in_proj_weights(w_in):
    sizes = (ATTN_WIDTH,) + (KV_WIDTH,) * 6 + (N_BRANCH * N_ATTN_HEADS, MLP_WIDTH, MLP_WIDTH)
    offs = [0]
    for s in sizes:
        offs.append(offs[-1] + s)
    col = lambda n: w_in[:, offs[n]:offs[n + 1]]
    q, kc, vc, ksl, vsl, kwn, vwn, gt, u, v = (col(n) for n in range(10))
    w_a = jnp.concatenate([q, ksl, vsl, kwn, vwn, kc, vc], axis=1).astype(BF16)
    D = w_in.shape[0]
    gt = gt.reshape(D, N_KV_HEADS, GQA_GROUP, N_BRANCH).transpose(1, 3, 2, 0)
    gt = gt.reshape(N_KV_HEADS, N_BRANCH * GQA_GROUP, D)
    gt = jnp.pad(gt, ((0, 0), (0, GATE_ROWS - N_BRANCH * GQA_GROUP), (0, 0)))
    w_gt = gt.reshape(N_KV_HEADS * GATE_ROWS, D).astype(BF16)
    w_uvt = jnp.concatenate([u, v], axis=1).T.astype(BF16)
    return w_a, w_gt, w_uvt


def _compress_weights(cmp_w1):
    two, _, hid = cmp_w1.shape
    w = cmp_w1.reshape(two, 2, CMP_STRIDE, HEAD_DIM, hid)
    eye = jnp.eye(N_KV_HEADS, dtype=cmp_w1.dtype)
    big = jnp.einsum('thpdc,kj->tpkdjhc', w, eye)
    return big.reshape(two, CMP_STRIDE * N_KV_HEADS * HEAD_DIM, N_KV_HEADS * 2 * hid).astype(BF16)


def _hybrid_layer(x, mod, w_in, cmp_pos, cmp_w1, cmp_w2, vn_g, vn_b, w_s, b_s, out_g, w_o, ln1_g, ln1_b,
                  w1, w3, w2, ln2_g, ln2_b, *, alpha):
    B, S, D = x.shape
    assert S % Q_TILE == 0 and S % TOK_TILE == 0 and S // SLC_LEN <= MAX_SLC_BLOCKS
    sh1, sc1, g1, sh2, sc2, g2 = (mod[:, None, n * D:(n + 1) * D] for n in range(6))
    w_a, w_gt, w_uvt = _in_proj_weights(w_in)
    qkv, kvc, gates, mlp = _in_proj_call(
        x, sc1, sh1, w_a, w_gt, w_uvt,
        vn_g.reshape(N_MLP_GROUPS, MLP_GROUP_DIM, 1), vn_b.reshape(N_MLP_GROUPS, MLP_GROUP_DIM, 1),
        jnp.swapaxes(w_s, 1, 2), b_s)
    ngrp = S // CMP_STRIDE
    tok = kvc.reshape(2, B, ngrp, CMP_STRIDE * KV_WIDTH)
    pos = jnp.broadcast_to(cmp_pos.reshape(2, 1, CMP_LEN * HEAD_DIM), (2, 8, CMP_LEN * HEAD_DIM))
    cmp_kv = _compress_call(tok, _compress_weights(cmp_w1), cmp_w1.astype(BF16), pos, cmp_w2.astype(BF16))
    gates = gates.reshape(B, N_KV_HEADS, GATE_ROWS, S)
    n_slc = S // SLC_LEN
    oc, bias = _cmp_attn_call(qkv, cmp_kv, gates, n_slc=n_slc, topk=min(SLC_TOPK, n_slc))
    os_ = _slc_attn_call(qkv, bias, gates)
    ow = _win_attn_call(qkv, gates)
    row = lambda a: a.reshape(1, D)
    x = _out_proj_call(oc, os_, ow, mlp, x, g1, out_g.reshape(-1, HEAD_DIM, 1), w_o.astype(BF16),
                       row(ln1_g), row(ln1_b), alpha=alpha)
    return _ffn_call(x, sc2, sh2, g2, w1.astype(BF16), w3.astype(BF16), w2.astype(BF16),
                     row(ln2_g), row(ln2_b), alpha=alpha)


def kernel(x, c, w_ada, b_ada, w_in, cmp_pos, cmp_w1, cmp_w2, vn_g, vn_b, w_s, b_s, out_g, w_o, ln1_g, ln1_b,
           w1, w3, w2, ln2_g, ln2_b):
    depth = w_ada.shape[0]
    alpha = (2.0 * depth) ** 0.25
    mod = _ada_call(c, w_ada, b_ada)
    for l in range(depth):
        x = _hybrid_layer(x, mod[l], w_in[l], cmp_pos[l], cmp_w1[l], cmp_w2[l], vn_g[l], vn_b[l], w_s[l], b_s[l],
                          out_g[l], w_o[l], ln1_g[l], ln1_b[l], w1[l], w3[l], w2[l], ln2_g[l], ln2_b[l], alpha=alpha)
    return x
```

```python
import functools
import math

import jax
import jax.numpy as jnp
from jax import lax
from jax.experimental import pallas as pl
from jax.experimental.pallas import tpu as pltpu

F32 = jnp.float32
BF16 = jnp.bfloat16

HEAD_DIM = 64
N_KV_HEADS = 2
GQA_GROUP = 4
N_ATTN_HEADS = N_KV_HEADS * GQA_GROUP
ATTN_WIDTH = N_ATTN_HEADS * HEAD_DIM
KV_WIDTH = N_KV_HEADS * HEAD_DIM
GROUP_WIDTH = GQA_GROUP * HEAD_DIM
N_BRANCH = 3
MLP_GROUP_DIM = 64
N_MLP_GROUPS = 8
MLP_WIDTH = N_MLP_GROUPS * MLP_GROUP_DIM
CMP_LEN = 32
CMP_STRIDE = 16
CMP_HIDDEN = 256
SLC_LEN = 64
SLC_TOPK = 16
WINDOW = 512
CHUNK = 128
LN_EPS = 1e-5
NEG = -1e30
FORCED_SCORE = 1e6

LOG2E = math.log2(math.e)
Q_SCALE = HEAD_DIM ** -0.5 * LOG2E

LANES = 128
MAX_SLC_BLOCKS = LANES
TOK_TILE = 512
Q_TILE = 256
CMP_Q_TILE = 512
PICK_LANES = 256
VMEM_LIMIT = 56 * 1024 * 1024
GATE_ROWS = 16

ROW_VSLC = ATTN_WIDTH
ROW_VWIN = ROW_VSLC + KV_WIDTH
ROW_GATE = ROW_VWIN + KV_WIDTH
ROW_U = ROW_GATE + N_KV_HEADS * GATE_ROWS
ROW_V = ROW_U + MLP_WIDTH
ROWS_T = ROW_V + MLP_WIDTH
COLS_A = 4 * KV_WIDTH


def _nt(a, b):
    return lax.dot_general(a, b, (((1,), (1,)), ((), ())), preferred_element_type=F32)


def _tn(a, b):
    return lax.dot_general(a, b, (((0,), (0,)), ((), ())), preferred_element_type=F32)


def _nn(a, b):
    return jnp.dot(a, b, preferred_element_type=F32)


def _gelu(x):
    return x * (0.5 * (1.0 + jnp.tanh(math.sqrt(2.0 / math.pi) * (x + 0.044715 * (x * x * x)))))


def _sigmoid(x):
    return 1.0 / (1.0 + jnp.exp(-x))


def _layer_norm_rows(z, g, b):
    mu = jnp.mean(z, axis=-1, keepdims=True)
    zc = z - mu
    var = jnp.mean(zc * zc, axis=-1, keepdims=True)
    return zc * lax.rsqrt(var + LN_EPS) * g + b


def _slope(kv_head, g):
    slope = lambda h: LOG2E * 2.0 ** (-8.0 * (h + 1) / N_ATTN_HEADS)
    val = jnp.asarray(slope(g), F32)
    for k in range(1, N_KV_HEADS):
        val = jnp.where(kv_head == k, jnp.asarray(slope(k * GQA_GROUP + g), F32), val)
    return val


def _pick_head_lanes(x, kv_head):
    out = x[:, :HEAD_DIM]
    for k in range(1, N_KV_HEADS):
        out = jnp.where(kv_head == k, x[:, k * HEAD_DIM:(k + 1) * HEAD_DIM], out)
    return out


def _pick_head_rows(x, kv_head):
    out = x[:HEAD_DIM]
    for k in range(1, N_KV_HEADS):
        out = jnp.where(kv_head == k, x[k * HEAD_DIM:(k + 1) * HEAD_DIM], out)
    return out


def _bf16_part(x):
    return x.astype(BF16).astype(F32)


def _alibi_key_cols(pos):
    lane = lax.broadcasted_iota(jnp.int32, pos.shape, 1)
    cols = jnp.where(lane % 2 == 0, pos // LANES, pos % LANES)
    return jnp.where(lane < 4, cols, jnp.where(lane < 8, 1, 0)).astype(F32)


def _alibi_query_rows(c, t0f, rows, tq):
    shape = (8, tq)
    row = lax.broadcasted_iota(jnp.int32, shape, 0)
    c = jnp.full(shape, c, F32)
    out = jnp.zeros(shape, F32)
    for n, part in enumerate((_bf16_part(c), _bf16_part(c - _bf16_part(c)))):
        shift = part * t0f
        vals = {2 * n: LANES * part, 2 * n + 1: part, 4 + 2 * n: -_bf16_part(shift), 5 + 2 * n: -(shift - _bf16_part(shift))}
        for r, val in vals.items():
            out = jnp.where(row == r, val, out)
    return jnp.concatenate([out, jnp.zeros((rows - 8, tq), F32)], axis=0)


def _params(*sem):
    return pltpu.CompilerParams(dimension_semantics=sem, vmem_limit_bytes=VMEM_LIMIT)


def _ada_kernel(c_ref, w_ref, b_ref, o_ref):
    c = c_ref[...]
    c_act = (c * _sigmoid(c)).astype(BF16)
    o_ref[0] = _nn(c_act, w_ref[0].astype(BF16)) + b_ref[0]


def _ada_call(c, w_ada, b_ada):
    L, D, D6 = w_ada.shape
    B = c.shape[0]
    return pl.pallas_call(
        _ada_kernel,
        grid=(L, D6 // D),
        in_specs=[pl.BlockSpec((B, D), lambda l, n: (0, 0)),
                  pl.BlockSpec((1, D, D), lambda l, n: (l, 0, n)),
                  pl.BlockSpec((1, 1, D), lambda l, n: (l, 0, n))],
        out_specs=pl.BlockSpec((1, B, D), lambda l, n: (l, 0, n)),
        out_shape=jax.ShapeDtypeStruct((L, B, D6), F32),
        compiler_params=_params("arbitrary", "arbitrary"),
        name="ada_mod",
    )(c, w_ada, b_ada.reshape(L, 1, D6))


def _in_proj_kernel(x_ref, sc_ref, sh_ref, wa_ref, wt_ref, vng_ref, vnb_ref, wst_ref, bs_ref,
                    kk_ref, kvc_ref, qt_ref, vt_ref, gates_ref, mlp_ref):
    tm = x_ref.shape[0]
    h = (x_ref[...] * (1.0 + sc_ref[...]) + sh_ref[...]).astype(BF16)
    a = _nn(h, wa_ref[...])
    kk_ref[...] = a[:, :2 * KV_WIDTH].astype(BF16)
    kvc_ref[0] = a[:, 2 * KV_WIDTH:3 * KV_WIDTH].astype(BF16)
    kvc_ref[1] = a[:, 3 * KV_WIDTH:].astype(BF16)
    t = _nt(wt_ref[...], h)
    qt_ref[...] = (t[:ATTN_WIDTH] * Q_SCALE).astype(BF16)
    vt_ref[...] = t[ROW_VSLC:ROW_GATE].astype(BF16)
    gates_ref[...] = t[ROW_GATE:ROW_U]

    u = _gelu(t[ROW_U:ROW_V])
    v = _gelu(t[ROW_V:]).reshape(N_MLP_GROUPS, MLP_GROUP_DIM, tm)
    mu = jnp.mean(v, axis=1, keepdims=True)
    vc = v - mu
    var = jnp.mean(vc * vc, axis=1, keepdims=True)
    vn = vc * lax.rsqrt(var + LN_EPS) * vng_ref[...] + vnb_ref[...]
    n_chunks = tm // CHUNK
    s_idx = lax.broadcasted_iota(jnp.int32, (CHUNK, CHUNK), 0)
    t_idx = lax.broadcasted_iota(jnp.int32, (CHUNK, CHUNK), 1)
    for g in range(N_MLP_GROUPS):
        vg = vn[g].astype(BF16)
        stack = jnp.concatenate([vg[:, c * CHUNK:(c + 1) * CHUNK] for c in range(n_chunks)], axis=0)
        w_t = jnp.where(s_idx <= t_idx, wst_ref[g], 0.0).astype(BF16)
        sv = _nn(stack, w_t)
        bias = bs_ref[g:g + 1, :]
        for c in range(n_chunks):
            mlp_ref[g * MLP_GROUP_DIM:(g + 1) * MLP_GROUP_DIM, c * CHUNK:(c + 1) * CHUNK] = (
                u[g * MLP_GROUP_DIM:(g + 1) * MLP_GROUP_DIM, c * CHUNK:(c + 1) * CHUNK]
                * (sv[c * MLP_GROUP_DIM:(c + 1) * MLP_GROUP_DIM] + bias))


def _in_proj_call(x, sc, sh, w_a, w_t, vn_g, vn_b, w_st, b_s):
    B, S, D = x.shape
    tm = min(TOK_TILE, S)
    const2 = lambda b, i: (0, 0)
    const3 = lambda b, i: (0, 0, 0)
    cm = lambda rows: pl.BlockSpec((None, rows, tm), lambda b, i: (b, 0, i))
    return pl.pallas_call(
        _in_proj_kernel,
        grid=(B, S // tm),
        in_specs=[pl.BlockSpec((None, tm, D), lambda b, i: (b, i, 0)),
                  pl.BlockSpec((None, 1, D), lambda b, i: (b, 0, 0)),
                  pl.BlockSpec((None, 1, D), lambda b, i: (b, 0, 0)),
                  pl.BlockSpec(w_a.shape, const2),
                  pl.BlockSpec(w_t.shape, const2),
                  pl.BlockSpec(vn_g.shape, const3),
                  pl.BlockSpec(vn_b.shape, const3),
                  pl.BlockSpec(w_st.shape, const3),
                  pl.BlockSpec(b_s.shape, const2)],
        out_specs=[pl.BlockSpec((None, tm, 2 * KV_WIDTH), lambda b, i: (b, i, 0)),
                   pl.BlockSpec((2, None, tm, KV_WIDTH), lambda b, i: (0, b, i, 0)),
                   cm(ATTN_WIDTH), cm(2 * KV_WIDTH), cm(N_KV_HEADS * GATE_ROWS), cm(MLP_WIDTH)],
        out_shape=[jax.ShapeDtypeStruct((B, S, 2 * KV_WIDTH), BF16),
                   jax.ShapeDtypeStruct((2, B, S, KV_WIDTH), BF16),
                   jax.ShapeDtypeStruct((B, ATTN_WIDTH, S), BF16),
                   jax.ShapeDtypeStruct((B, 2 * KV_WIDTH, S), BF16),
                   jax.ShapeDtypeStruct((B, N_KV_HEADS * GATE_ROWS, S), F32),
                   jax.ShapeDtypeStruct((B, MLP_WIDTH, S), F32)],
        compiler_params=_params("arbitrary", "arbitrary"),
        name="in_proj",
    )(x, sc, sh, w_a, w_t, vn_g, vn_b, w_st, b_s)


def _compress_kernel(tok_ref, wbig_ref, w1_ref, pos_ref, w2_ref, out_ref, out_t_ref):
    ab = _nn(tok_ref[...], wbig_ref[...])
    bias = _nn(pos_ref[...].astype(BF16), w1_ref[...])[0:1]
    ngrp = ab.shape[0]
    for h in range(N_KV_HEADS):
        base = h * 2 * CMP_HIDDEN
        first = ab[:, base:base + CMP_HIDDEN]
        second = ab[:, base + CMP_HIDDEN:base + 2 * CMP_HIDDEN]
        hid = _gelu(first + pltpu.roll(second, ngrp - 1, 0) + bias)
        c = _nn(hid.astype(BF16), w2_ref[...])
        block_end = CMP_STRIDE * lax.broadcasted_iota(jnp.int32, (ngrp, HEAD_DIM), 0) + (CMP_LEN - 1)
        out_ref[h, :, 0:HEAD_DIM] = c.astype(BF16)
        out_ref[h, :, HEAD_DIM:] = _alibi_key_cols(block_end).astype(BF16)
        out_t_ref[h] = c.T.astype(BF16)


def _compress_call(tok, wbig, w1, pos, w2):
    two, B, ngrp, flat = tok.shape
    return pl.pallas_call(
        _compress_kernel,
        grid=(two, B),
        in_specs=[pl.BlockSpec((None, None, ngrp, flat), lambda t, b: (t, b, 0, 0)),
                  pl.BlockSpec((None,) + wbig.shape[1:], lambda t, b: (t, 0, 0)),
                  pl.BlockSpec((None,) + w1.shape[1:], lambda t, b: (t, 0, 0)),
                  pl.BlockSpec((None,) + pos.shape[1:], lambda t, b: (t, 0, 0)),
                  pl.BlockSpec((None,) + w2.shape[1:], lambda t, b: (t, 0, 0))],
        out_specs=[pl.BlockSpec((None, None, N_KV_HEADS, ngrp, 2 * HEAD_DIM), lambda t, b: (t, b, 0, 0, 0)),
                   pl.BlockSpec((None, None, N_KV_HEADS, HEAD_DIM, ngrp), lambda t, b: (t, b, 0, 0, 0))],
        out_shape=[jax.ShapeDtypeStruct((two, B, N_KV_HEADS, ngrp, 2 * HEAD_DIM), BF16),
                   jax.ShapeDtypeStruct((two, B, N_KV_HEADS, HEAD_DIM, ngrp), BF16)],
        compiler_params=_params("arbitrary", "arbitrary"),
        name="compress",
    )(tok, wbig, w1, pos, w2)


def _cmp_attn_kernel(qt_ref, kc_ref, vct_ref, gates_ref, oc_ref, bias_ref, score_ref, left_ref, *, n_slc, topk):
    k_head = pl.program_id(1)
    i = pl.program_id(2)
    tq = qt_ref.shape[1]
    ncmp = kc_ref.shape[0]
    t0 = i * tq
    tile4 = lambda a: jnp.concatenate([a] * GQA_GROUP, axis=1)
    t_col = t0 + lax.broadcasted_iota(jnp.int32, (ncmp, tq), 1)
    cmp_end = CMP_STRIDE * lax.broadcasted_iota(jnp.int32, (ncmp, tq), 0) + (CMP_LEN - 1)
    ok = tile4(t_col >= cmp_end)
    any_ok = tile4((t0 + lax.broadcasted_iota(jnp.int32, (1, tq), 1)) >= (CMP_LEN - 1))
    t0f = t0.astype(F32)
    q_aug = jnp.concatenate(
        [jnp.concatenate([qt_ref[g * HEAD_DIM:(g + 1) * HEAD_DIM, :] for g in range(GQA_GROUP)], axis=1),
         jnp.concatenate([_alibi_query_rows(_slope(k_head, g), t0f, HEAD_DIM, tq) for g in range(GQA_GROUP)],
                         axis=1).astype(BF16)], axis=0)
    s = jnp.where(ok, _nn(kc_ref[...], q_aug), NEG)
    e = jnp.exp2(s - jnp.max(s, axis=0, keepdims=True))
    ones = jnp.where(lax.broadcasted_iota(jnp.int32, (16, ncmp), 0) == 0, 1.0, 0.0).astype(BF16)
    o = _nn(jnp.concatenate([vct_ref[...], ones], axis=0), e.astype(BF16))
    inv = jnp.where(any_ok, 1.0 / o[HEAD_DIM:HEAD_DIM + 1], 0.0)
    p = e * inv
    p_sum = sum(p[:, g * tq:(g + 1) * tq] for g in range(GQA_GROUP))
    out = o[:HEAD_DIM] * inv
    for g in range(GQA_GROUP):
        gate = _sigmoid(gates_ref[g:g + 1, :])
        oc_ref[g * HEAD_DIM:(g + 1) * HEAD_DIM, :] = out[:, g * tq:(g + 1) * tq] * gate

    j_i = lax.broadcasted_iota(jnp.int32, (MAX_SLC_BLOCKS, ncmp), 0)
    n_i = lax.broadcasted_iota(jnp.int32, (MAX_SLC_BLOCKS, ncmp), 1)
    overlap = ((CMP_STRIDE * n_i <= SLC_LEN * j_i + (SLC_LEN - 1))
               & (CMP_STRIDE * n_i + (CMP_LEN - 1) >= SLC_LEN * j_i)).astype(BF16)
    p_hi = p_sum.astype(BF16)
    r1 = p_sum - p_hi.astype(F32)
    p_mid = r1.astype(BF16)
    p_lo = (r1 - p_mid.astype(F32)).astype(BF16)
    score = _nn(overlap, p_hi) + _nn(overlap, p_mid) + _nn(overlap, p_lo)

    j_row = lax.broadcasted_iota(jnp.int32, (MAX_SLC_BLOCKS, tq), 0)
    t_blk = t0 + lax.broadcasted_iota(jnp.int32, (MAX_SLC_BLOCKS, tq), 1)
    cur = t_blk // SLC_LEN
    valid = (SLC_LEN * j_row <= t_blk) & (j_row < n_slc)
    forced = (j_row == 0) | (j_row == cur) | (j_row == cur - 1)
    cand = valid & jnp.logical_not(forced)
    n_forced = 1 + (cur[0:1] >= 1).astype(jnp.int32) + (cur[0:1] >= 2).astype(jnp.int32)
    remaining = jnp.broadcast_to(topk - n_forced, (8, tq))
    score = jnp.where(cand, score, -jnp.inf)
    n_groups = tq // PICK_LANES
    for lb in range(n_groups):
        score_ref[lb] = score[:, lb * PICK_LANES:(lb + 1) * PICK_LANES]
        left_ref[lb] = remaining[:, lb * PICK_LANES:(lb + 1) * PICK_LANES]
    j_f = lax.broadcasted_iota(jnp.int32, (MAX_SLC_BLOCKS, LANES), 0).astype(F32)

    def pick_rounds(lb, rounds):
        halves = [score_ref[lb, :, h * LANES:(h + 1) * LANES] for h in range(PICK_LANES // LANES)]
        left = [left_ref[lb, 0:1, h * LANES:(h + 1) * LANES] for h in range(PICK_LANES // LANES)]
        for r in rounds:
            for h, sc in enumerate(halves):
                m = jnp.max(sc, axis=0, keepdims=True)
                first = jnp.min(jnp.where(sc == m, j_f, float(MAX_SLC_BLOCKS)), axis=0, keepdims=True)
                first = jnp.where(left[h] > r, first, -1.0)
                halves[h] = jnp.where(j_f == first, -jnp.inf, sc)
        score_ref[lb] = jnp.concatenate(halves, axis=1)

    base_rounds = topk - 3

    def group(lb, carry):
        pick_rounds(lb, range(base_rounds))
        return carry

    lax.fori_loop(0, n_groups, group, 0)

    @pl.when(t0 < 2 * SLC_LEN)
    def _early_queries():
        pick_rounds(0, range(base_rounds, topk - 1))

    picked = jnp.concatenate([score_ref[lb] for lb in range(n_groups)], axis=1) == -jnp.inf
    bias_ref[...] = jnp.where((valid & forced) | (cand & picked), 0.0, NEG).astype(BF16)


def _cmp_attn_call(qt, cmp_kv, cmp_kv_t, gates, *, n_slc, topk):
    B, _, S = qt.shape
    ncmp = cmp_kv.shape[3]
    tq = min(CMP_Q_TILE, S)
    assert topk >= 3 and tq >= 2 * SLC_LEN
    kern = functools.partial(_cmp_attn_kernel, n_slc=n_slc, topk=topk)
    return pl.pallas_call(
        kern,
        grid=(B, N_KV_HEADS, S // tq),
        in_specs=[pl.BlockSpec((None, GROUP_WIDTH, tq), lambda b, k, i: (b, k, i)),
                  pl.BlockSpec((None, None, None, ncmp, 2 * HEAD_DIM), lambda b, k, i: (0, b, k, 0, 0)),
                  pl.BlockSpec((None, None, None, HEAD_DIM, ncmp), lambda b, k, i: (1, b, k, 0, 0)),
                  pl.BlockSpec((None, None, GATE_ROWS, tq), lambda b, k, i: (b, k, 0, i))],
        out_specs=[pl.BlockSpec((None, GROUP_WIDTH, tq), lambda b, k, i: (b, k, i)),
                   pl.BlockSpec((None, None, MAX_SLC_BLOCKS, tq), lambda b, k, i: (b, k, 0, i))],
        out_shape=[jax.ShapeDtypeStruct((B, ATTN_WIDTH, S), F32),
                   jax.ShapeDtypeStruct((B, N_KV_HEADS, MAX_SLC_BLOCKS, S), BF16)],
        scratch_shapes=[pltpu.VMEM((tq // PICK_LANES, MAX_SLC_BLOCKS, PICK_LANES), F32),
                        pltpu.VMEM((tq // PICK_LANES, 8, PICK_LANES), jnp.int32)],
        compiler_params=_params("arbitrary", "arbitrary", "arbitrary"),
        name="cmp_attn",
    )(qt, cmp_kv, cmp_kv_t, gates)


V_ROWS = HEAD_DIM + 16


def _slc_attn_kernel(qt_ref, bias_ref, ks_ref, vt_ref, gates_ref, os_ref,
                     kaug_ref, vh_ref, qaug_ref, m_ref, acc_ref, sa_ref, sb_ref):
    k_head = pl.program_id(1)
    i = pl.program_id(2)
    tq = qt_ref.shape[1]
    n_tiles = kaug_ref.shape[0]
    t0 = i * tq

    @pl.when(i == 0)
    def _build_keys():
        def fill(c, carry):
            r0 = pl.multiple_of(c * tq, tq)
            pos = r0 + lax.broadcasted_iota(jnp.int32, (tq, HEAD_DIM), 0)
            kaug_ref[c, :, 0:HEAD_DIM] = _pick_head_lanes(ks_ref[pl.ds(r0, tq), :], k_head)
            kaug_ref[c, :, HEAD_DIM:2 * HEAD_DIM] = _alibi_key_cols(pos).astype(BF16)
            blk = (r0 + lax.broadcasted_iota(jnp.int32, (tq, MAX_SLC_BLOCKS), 0)) // SLC_LEN
            hot = blk == lax.broadcasted_iota(jnp.int32, (tq, MAX_SLC_BLOCKS), 1)
            kaug_ref[c, :, 2 * HEAD_DIM:] = jnp.where(hot, 1.0, 0.0).astype(BF16)
            return carry

        lax.fori_loop(0, n_tiles, fill, 0)
        ones_row = lax.broadcasted_iota(jnp.int32, (V_ROWS - HEAD_DIM, tq), 0) == 0
        for c in range(n_tiles):
            vh_ref[c, 0:HEAD_DIM, :] = _pick_head_rows(vt_ref[:, c * tq:(c + 1) * tq], k_head)
            vh_ref[c, HEAD_DIM:, :] = jnp.where(ones_row, 1.0, 0.0).astype(BF16)

    t0f = t0.astype(F32)
    bias = bias_ref[...]
    for g in range(GQA_GROUP):
        aux = _alibi_query_rows(_slope(k_head, g), t0f, HEAD_DIM, tq)
        qaug_ref[0:HEAD_DIM, g * tq:(g + 1) * tq] = qt_ref[g * HEAD_DIM:(g + 1) * HEAD_DIM, :]
        qaug_ref[HEAD_DIM:2 * HEAD_DIM, g * tq:(g + 1) * tq] = aux.astype(BF16)
        qaug_ref[2 * HEAD_DIM:, g * tq:(g + 1) * tq] = bias
    m_ref[...] = jnp.full(m_ref.shape, NEG, F32)
    acc_ref[...] = jnp.zeros(acc_ref.shape, F32)

    s_bufs = (sa_ref, sb_ref)

    def scores(j, slot):
        s_bufs[slot][...] = _nn(kaug_ref[j], qaug_ref[...])

    def step(j, slot, causal, prefetch):
        if prefetch:
            scores(j + 1, 1 - slot)
        s = s_bufs[slot][...]
        if causal:
            key = lax.broadcasted_iota(jnp.int32, (tq, tq), 0)
            qry = lax.broadcasted_iota(jnp.int32, (tq, tq), 1)
            s = jnp.where(jnp.concatenate([key <= qry] * GQA_GROUP, axis=1), s, NEG)
        m_old = m_ref[...]
        m_new = jnp.maximum(m_old, jnp.max(s, axis=0, keepdims=True))
        alpha = jnp.exp2(m_old - m_new)
        p = jnp.exp2(s - m_new).astype(BF16)
        acc_ref[...] = alpha * acc_ref[...] + _nn(vh_ref[j], p)
        m_ref[...] = m_new

    scores(0, 0)

    def pair(p, carry):
        step(2 * p, 0, False, True)
        step(2 * p + 1, 1, False, True)
        return carry

    lax.fori_loop(0, i // 2, pair, 0)

    @pl.when(i % 2 == 1)
    def _odd_tail():
        step(i - 1, 0, False, True)
        step(i, 1, True, False)

    @pl.when(i % 2 == 0)
    def _even_tail():
        step(i, 0, True, False)

    o = acc_ref[0:HEAD_DIM, :] / acc_ref[HEAD_DIM:HEAD_DIM + 1, :]
    for g in range(GQA_GROUP):
        gate = _sigmoid(gates_ref[GQA_GROUP + g:GQA_GROUP + g + 1, :])
        os_ref[g * HEAD_DIM:(g + 1) * HEAD_DIM, :] = o[:, g * tq:(g + 1) * tq] * gate


def _slc_attn_call(qt, bias, kk, vt, gates):
    B, _, S = qt.shape
    tq = min(Q_TILE, S)
    return pl.pallas_call(
        _slc_attn_kernel,
        grid=(B, N_KV_HEADS, S // tq),
        in_specs=[pl.BlockSpec((None, GROUP_WIDTH, tq), lambda b, k, i: (b, k, i)),
                  pl.BlockSpec((None, None, MAX_SLC_BLOCKS, tq), lambda b, k, i: (b, k, 0, i)),
                  pl.BlockSpec((None, S, KV_WIDTH), lambda b, k, i: (b, 0, 0)),
                  pl.BlockSpec((None, KV_WIDTH, S), lambda b, k, i: (b, 0, 0)),
                  pl.BlockSpec((None, None, GATE_ROWS, tq), lambda b, k, i: (b, k, 0, i))],
        out_specs=pl.BlockSpec((None, GROUP_WIDTH, tq), lambda b, k, i: (b, k, i)),
        out_shape=jax.ShapeDtypeStruct((B, ATTN_WIDTH, S), F32),
        scratch_shapes=[pltpu.VMEM((S // tq, tq, 2 * LANES), BF16),
                        pltpu.VMEM((S // tq, V_ROWS, tq), BF16),
                        pltpu.VMEM((2 * LANES, GQA_GROUP * tq), BF16),
                        pltpu.VMEM((1, GQA_GROUP * tq), F32),
                        pltpu.VMEM((V_ROWS, GQA_GROUP * tq), F32),
                        pltpu.VMEM((tq, GQA_GROUP * tq), F32),
                        pltpu.VMEM((tq, GQA_GROUP * tq), F32)],
        compiler_params=_params("arbitrary", "arbitrary", "arbitrary"),
        name="slc_attn",
    )(qt, bias, kk, vt, gates)


def _win_attn_kernel(qt_ref, k2_ref, k1_ref, k0_ref, v2_ref, v1_ref, v0_ref, gates_ref, ow_ref):
    k_head = pl.program_id(1)
    i = pl.program_id(2)
    tq = qt_ref.shape[1]
    t0f = (i * tq).astype(F32)
    qs = jnp.concatenate([qt_ref[g * HEAD_DIM:(g + 1) * HEAD_DIM, :] for g in range(GQA_GROUP)], axis=1)
    alibi = jnp.concatenate([_alibi_query_rows(_slope(k_head, g), t0f, KV_WIDTH, tq) for g in range(GQA_GROUP)],
                            axis=1).astype(BF16)
    q2 = jnp.concatenate([jnp.where(k_head == k, qs, jnp.zeros_like(qs)) for k in range(N_KV_HEADS)] + [alibi],
                         axis=0)
    key = lax.broadcasted_iota(jnp.int32, (tq, tq), 0)
    qry = lax.broadcasted_iota(jnp.int32, (tq, tq), 1)
    tile4 = lambda a: jnp.concatenate([a] * GQA_GROUP, axis=1)
    keeps = (tile4(qry < key) & (i >= 2), i >= 1, tile4(key <= qry))
    ones = jnp.where(lax.broadcasted_iota(jnp.int32, (16, tq), 0) == 0, 1.0, 0.0).astype(BF16)
    s_tiles, v_tiles = [], []
    for back, k_ref, v_ref, keep in zip((2, 1, 0), (k2_ref, k1_ref, k0_ref), (v2_ref, v1_ref, v0_ref), keeps):
        pos = (i - back) * tq + lax.broadcasted_iota(jnp.int32, (tq, KV_WIDTH), 0)
        k_aug = jnp.concatenate([k_ref[...], _alibi_key_cols(pos).astype(BF16)], axis=1)
        s_tiles.append(jnp.where(keep, _nn(k_aug, q2), NEG))
        v_tiles.append(jnp.concatenate([v_ref[...], ones], axis=0))
    m = functools.reduce(jnp.maximum, [jnp.max(s, axis=0, keepdims=True) for s in s_tiles])
    o2 = sum(_nn(v, jnp.exp2(s - m).astype(BF16)) for v, s in zip(v_tiles, s_tiles))
    o = _pick_head_rows(o2[:KV_WIDTH], k_head) / o2[KV_WIDTH:KV_WIDTH + 1]
    for g in range(GQA_GROUP):
        gate = _sigmoid(gates_ref[2 * GQA_GROUP + g:2 * GQA_GROUP + g + 1, :])
        ow_ref[g * HEAD_DIM:(g + 1) * HEAD_DIM, :] = o[:, g * tq:(g + 1) * tq] * gate


def _win_attn_call(qt, kk, vt, gates):
    B, _, S = qt.shape
    tq = min(Q_TILE, S)
    assert WINDOW == 2 * tq
    k_spec = lambda back: pl.BlockSpec((None, tq, KV_WIDTH), lambda b, k, i: (b, jnp.maximum(i - back, 0), 1))
    v_spec = lambda back: pl.BlockSpec((None, KV_WIDTH, tq), lambda b, k, i: (b, 1, jnp.maximum(i - back, 0)))
    return pl.pallas_call(
        _win_attn_kernel,
        grid=(B, N_KV_HEADS, S // tq),
        in_specs=[pl.BlockSpec((None, GROUP_WIDTH, tq), lambda b, k, i: (b, k, i)),
                  k_spec(2), k_spec(1), k_spec(0), v_spec(2), v_spec(1), v_spec(0),
                  pl.BlockSpec((None, None, GATE_ROWS, tq), lambda b, k, i: (b, k, 0, i))],
        out_specs=pl.BlockSpec((None, GROUP_WIDTH, tq), lambda b, k, i: (b, k, i)),
        out_shape=jax.ShapeDtypeStruct((B, ATTN_WIDTH, S), F32),
        compiler_params=_params("arbitrary", "arbitrary", "arbitrary"),
        name="win_attn",
    )(qt, kk, kk, kk, vt, vt, vt, gates)


def _group_rms(a, gain):
    c, tm = a.shape
    a3 = a.reshape(c // HEAD_DIM, HEAD_DIM, tm)
    ms = jnp.mean(a3 * a3, axis=1, keepdims=True)
    return (a3 * lax.rsqrt(ms + LN_EPS) * gain).reshape(c, tm)


def _out_proj_kernel(oc_ref, os_ref, ow_ref, mlp_ref, x_ref, g1_ref, og_ref, wo_ref, lng_ref, lnb_ref, o_ref, *, alpha):
    n_attn = ATTN_WIDTH // HEAD_DIM
    attn = _group_rms(oc_ref[...] + os_ref[...] + ow_ref[...], og_ref[:n_attn])
    mlp = _group_rms(mlp_ref[...], og_ref[n_attn:])
    y_t = jnp.concatenate([attn, mlp], axis=0).astype(BF16)
    y = _tn(y_t, wo_ref[...])
    z = alpha * x_ref[...] + (1.0 + g1_ref[...]) * y
    o_ref[...] = _layer_norm_rows(z, lng_ref[...], lnb_ref[...])


def _out_proj_call(oc, os_, ow, mlp, x, g1, out_g, w_o, ln_g, ln_b, *, alpha):
    B, S, D = x.shape
    tm = min(TOK_TILE, S)
    cm = lambda width: pl.BlockSpec((None, width, tm), lambda b, i: (b, 0, i))
    row = pl.BlockSpec((1, D), lambda b, i: (0, 0))
    return pl.pallas_call(
        functools.partial(_out_proj_kernel, alpha=alpha),
        grid=(B, S // tm),
        in_specs=[cm(ATTN_WIDTH), cm(ATTN_WIDTH), cm(ATTN_WIDTH), cm(MLP_WIDTH),
                  pl.BlockSpec((None, tm, D), lambda b, i: (b, i, 0)),
                  pl.BlockSpec((None, 1, D), lambda b, i: (b, 0, 0)),
                  pl.BlockSpec(out_g.shape, lambda b, i: (0, 0, 0)),
                  pl.BlockSpec(w_o.shape, lambda b, i: (0, 0)),
                  row, row],
        out_specs=pl.BlockSpec((None, tm, D), lambda b, i: (b, i, 0)),
        out_shape=jax.ShapeDtypeStruct((B, S, D), F32),
        compiler_params=_params("arbitrary", "arbitrary"),
        name="out_proj",
    )(oc, os_, ow, mlp, x, g1, out_g, w_o, ln_g, ln_b)


def _ffn_kernel(x_ref, sc_ref, sh_ref, g2_ref, w1_ref, w3_ref, w2_ref, lng_ref, lnb_ref, o_ref, *, alpha, splits):
    x = x_ref[...]
    h = (x * (1.0 + sc_ref[...]) + sh_ref[...]).astype(BF16)
    f = None
    for lo, hi in splits:
        a = _nn(h, w1_ref[:, lo:hi])
        b = _nn(h, w3_ref[:, lo:hi])
        part = _nn((a * _sigmoid(a) * b).astype(BF16), w2_ref[lo:hi, :])
        f = part if f is None else f + part
    z = alpha * x + (1.0 + g2_ref[...]) * f
    o_ref[...] = _layer_norm_rows(z, lng_ref[...], lnb_ref[...])


def _ffn_call(x, sc, sh, g2, w1, w3, w2, ln_g, ln_b, *, alpha):
    B, S, D = x.shape
    d_ff = w1.shape[1]
    tm = min(TOK_TILE, S)
    half = (d_ff // 2 + 255) // 256 * 256
    splits = ((0, half), (half, d_ff))
    mod = pl.BlockSpec((None, 1, D), lambda b, i: (b, 0, 0))
    row = pl.BlockSpec((1, D), lambda b, i: (0, 0))
    resident = lambda shape: pl.BlockSpec(shape, lambda b, i: (0, 0), pipeline_mode=pl.Buffered(1))
    return pl.pallas_call(
        functools.partial(_ffn_kernel, alpha=alpha, splits=splits),
        grid=(B, S // tm),
        in_specs=[pl.BlockSpec((None, tm, D), lambda b, i: (b, i, 0)), mod, mod, mod,
                  resident(w1.shape), resident(w3.shape), resident(w2.shape), row, row],
        out_specs=pl.BlockSpec((None, tm, D), lambda b, i: (b, i, 0)),
        out_shape=jax.ShapeDtypeStruct((B, S, D), F32),
        compiler_params=_params("arbitrary", "arbitrary"),
        name="ffn",
    )(x, sc, sh, g2, w1, w3, w2, ln_g, ln_b)


def _in_proj_weights(w_in):
    sizes = (ATTN_WIDTH,) + (KV_WIDTH,) * 6 + (N_BRANCH * N_ATTN_HEADS, MLP_WIDTH, MLP_WIDTH)
    offs = [0]
    for s in sizes:
        offs.append(offs[-1] + s)
    col = lambda n: w_in[:, offs[n]:offs[n + 1]]
    q, kc, vc, ksl, vsl, kwn, vwn, gt, u, v = (col(n) for n in range(10))
    w_a = jnp.concatenate([ksl, kwn, kc, vc], axis=1).astype(BF16)
    D = w_in.shape[0]
    gt = gt.reshape(D, N_KV_HEADS, GQA_GROUP, N_BRANCH).transpose(1, 3, 2, 0)
    gt = gt.reshape(N_KV_HEADS, N_BRANCH * GQA_GROUP, D)
    gt = jnp.pad(gt, ((0, 0), (0, GATE_ROWS - N_BRANCH * GQA_GROUP), (0, 0)))
    gt = gt.reshape(N_KV_HEADS * GATE_ROWS, D)
    w_t = jnp.concatenate([q.T, vsl.T, vwn.T, gt, u.T, v.T], axis=0).astype(BF16)
    return w_a, w_t


def _compress_weights(cmp_w1):
    two, _, hid = cmp_w1.shape
    w = cmp_w1.reshape(two, 2, CMP_STRIDE, HEAD_DIM, hid)
    eye = jnp.eye(N_KV_HEADS, dtype=cmp_w1.dtype)
    big = jnp.einsum('thpdc,kj->tpkdjhc', w, eye)
    return big.reshape(two, CMP_STRIDE * N_KV_HEADS * HEAD_DIM, N_KV_HEADS * 2 * hid).astype(BF16)


def _hybrid_layer(x, mod, w_in, cmp_pos, cmp_w1, cmp_w2, vn_g, vn_b, w_s, b_s, out_g, w_o, ln1_g, ln1_b,
                  w1, w3, w2, ln2_g, ln2_b, *, alpha):
    B, S, D = x.shape
    assert S % Q_TILE == 0 and S % TOK_TILE == 0 and S // SLC_LEN <= MAX_SLC_BLOCKS
    sh1, sc1, g1, sh2, sc2, g2 = (mod[:, None, n * D:(n + 1) * D] for n in range(6))
    w_a, w_t = _in_proj_weights(w_in)
    kk, kvc, qt, vt, gates, mlp = _in_proj_call(
        x, sc1, sh1, w_a, w_t,
        vn_g.reshape(N_MLP_GROUPS, MLP_GROUP_DIM, 1), vn_b.reshape(N_MLP_GROUPS, MLP_GROUP_DIM, 1),
        jnp.swapaxes(w_s, 1, 2), b_s)
    ngrp = S // CMP_STRIDE
    tok = kvc.reshape(2, B, ngrp, CMP_STRIDE * KV_WIDTH)
    pos = jnp.broadcast_to(cmp_pos.reshape(2, 1, CMP_LEN * HEAD_DIM), (2, 8, CMP_LEN * HEAD_DIM))
    cmp_kv, cmp_kv_t = _compress_call(tok, _compress_weights(cmp_w1), cmp_w1.astype(BF16), pos, cmp_w2.astype(BF16))
    gates = gates.reshape(B, N_KV_HEADS, GATE_ROWS, S)
    n_slc = S // SLC_LEN
    oc, bias = _cmp_attn_call(qt, cmp_kv, cmp_kv_t, gates, n_slc=n_slc, topk=min(SLC_TOPK, n_slc))
    os_ = _slc_attn_call(qt, bias, kk, vt, gates)
    ow = _win_attn_call(qt, kk, vt, gates)
    row = lambda a: a.reshape(1, D)
    x = _out_proj_call(oc, os_, ow, mlp, x, g1, out_g.reshape(-1, HEAD_DIM, 1), w_o.astype(BF16),
                       row(ln1_g), row(ln1_b), alpha=alpha)
    return _ffn_call(x, sc2, sh2, g2, w1.astype(BF16), w3.astype(BF16), w2.astype(BF16),
                     row(ln2_g), row(ln2_b), alpha=alpha)


def kernel(x, c, w_ada, b_ada, w_in, cmp_pos, cmp_w1, cmp_w2, vn_g, vn_b, w_s, b_s, out_g, w_o, ln1_g, ln1_b,
           w1, w3, w2, ln2_g, ln2_b):
    depth = w_ada.shape[0]
    alpha = (2.0 * depth) ** 0.25
    mod = _ada_call(c, w_ada, b_ada)
    for l in range(depth):
        x = _hybrid_layer(x, mod[l], w_in[l], cmp_pos[l], cmp_w1[l], cmp_w2[l], vn_g[l], vn_b[l], w_s[l], b_s[l],
                          out_g[l], w_o[l], ln1_g[l], ln1_b[l], w1[l], w3[l], w2[l], ln2_g[l], ln2_b[l], alpha=alpha)
    return x
```

```python
import functools
import math

import jax
import jax.numpy as jnp
from jax import lax
from jax.experimental import pallas as pl
from jax.experimental.pallas import tpu as pltpu

F32 = jnp.float32
BF16 = jnp.bfloat16

HEAD_DIM = 64
N_KV_HEADS = 2
GQA_GROUP = 4
N_ATTN_HEADS = N_KV_HEADS * GQA_GROUP
ATTN_WIDTH = N_ATTN_HEADS * HEAD_DIM
KV_WIDTH = N_KV_HEADS * HEAD_DIM
GROUP_WIDTH = GQA_GROUP * HEAD_DIM
N_BRANCH = 3
MLP_GROUP_DIM = 64
N_MLP_GROUPS = 8
MLP_WIDTH = N_MLP_GROUPS * MLP_GROUP_DIM
CMP_LEN = 32
CMP_STRIDE = 16
CMP_HIDDEN = 256
SLC_LEN = 64
SLC_TOPK = 16
WINDOW = 512
CHUNK = 128
LN_EPS = 1e-5
NEG = -1e30
FORCED_SCORE = 1e6

LOG2E = math.log2(math.e)
Q_SCALE = HEAD_DIM ** -0.5 * LOG2E

LANES = 128
MAX_SLC_BLOCKS = LANES
TOK_TILE = 512
Q_TILE = 256
SLC_TILE = 512
CMP_Q_TILE = 512
PICK_LANES = 256
VMEM_LIMIT = 56 * 1024 * 1024
GATE_ROWS = 16

ROW_VSLC = ATTN_WIDTH
ROW_VWIN = ROW_VSLC + KV_WIDTH
ROW_GATE = ROW_VWIN + KV_WIDTH
ROW_U = ROW_GATE + N_KV_HEADS * GATE_ROWS
ROW_V = ROW_U + MLP_WIDTH
ROWS_T = ROW_V + MLP_WIDTH
COLS_A = 4 * KV_WIDTH


def _nt(a, b):
    return lax.dot_general(a, b, (((1,), (1,)), ((), ())), preferred_element_type=F32)


def _tn(a, b):
    return lax.dot_general(a, b, (((0,), (0,)), ((), ())), preferred_element_type=F32)


def _nn(a, b):
    return jnp.dot(a, b, preferred_element_type=F32)


def _gelu(x):
    return x * (0.5 * (1.0 + jnp.tanh(math.sqrt(2.0 / math.pi) * (x + 0.044715 * (x * x * x)))))


def _sigmoid(x):
    return 1.0 / (1.0 + jnp.exp(-x))


def _layer_norm_rows(z, g, b):
    mu = jnp.mean(z, axis=-1, keepdims=True)
    zc = z - mu
    var = jnp.mean(zc * zc, axis=-1, keepdims=True)
    return zc * lax.rsqrt(var + LN_EPS) * g + b


def _slope(kv_head, g):
    slope = lambda h: LOG2E * 2.0 ** (-8.0 * (h + 1) / N_ATTN_HEADS)
    val = jnp.asarray(slope(g), F32)
    for k in range(1, N_KV_HEADS):
        val = jnp.where(kv_head == k, jnp.asarray(slope(k * GQA_GROUP + g), F32), val)
    return val


def _pick_head_lanes(x, kv_head):
    out = x[:, :HEAD_DIM]
    for k in range(1, N_KV_HEADS):
        out = jnp.where(kv_head == k, x[:, k * HEAD_DIM:(k + 1) * HEAD_DIM], out)
    return out


def _pick_head_rows(x, kv_head):
    out = x[:HEAD_DIM]
    for k in range(1, N_KV_HEADS):
        out = jnp.where(kv_head == k, x[k * HEAD_DIM:(k + 1) * HEAD_DIM], out)
    return out


def _bf16_part(x):
    return x.astype(BF16).astype(F32)


def _alibi_key_cols(pos):
    lane = lax.broadcasted_iota(jnp.int32, pos.shape, 1)
    cols = jnp.where(lane % 2 == 0, pos // LANES, pos % LANES)
    return jnp.where(lane < 4, cols, jnp.where(lane < 8, 1, 0)).astype(F32)


def _alibi_query_rows(c, t0f, rows, tq, row8=0.0):
    shape = (8, tq)
    row = lax.broadcasted_iota(jnp.int32, shape, 0)
    c = jnp.full(shape, c, F32)
    out = jnp.zeros(shape, F32)
    for n, part in enumerate((_bf16_part(c), _bf16_part(c - _bf16_part(c)))):
        shift = part * t0f
        vals = {2 * n: LANES * part, 2 * n + 1: part, 4 + 2 * n: -_bf16_part(shift), 5 + 2 * n: -(shift - _bf16_part(shift))}
        for r, val in vals.items():
            out = jnp.where(row == r, val, out)
    tail = jnp.where(lax.broadcasted_iota(jnp.int32, (rows - 8, tq), 0) == 0, row8, 0.0)
    return jnp.concatenate([out, tail], axis=0)


def _params(*sem):
    return pltpu.CompilerParams(dimension_semantics=sem, vmem_limit_bytes=VMEM_LIMIT)


def _ada_kernel(c_ref, w_ref, b_ref, o_ref):
    c = c_ref[...]
    c_act = (c * _sigmoid(c)).astype(BF16)
    o_ref[0] = _nn(c_act, w_ref[0].astype(BF16)) + b_ref[0]


def _ada_call(c, w_ada, b_ada):
    L, D, D6 = w_ada.shape
    B = c.shape[0]
    return pl.pallas_call(
        _ada_kernel,
        grid=(L, D6 // D),
        in_specs=[pl.BlockSpec((B, D), lambda l, n: (0, 0)),
                  pl.BlockSpec((1, D, D), lambda l, n: (l, 0, n)),
                  pl.BlockSpec((1, 1, D), lambda l, n: (l, 0, n))],
        out_specs=pl.BlockSpec((1, B, D), lambda l, n: (l, 0, n)),
        out_shape=jax.ShapeDtypeStruct((L, B, D6), F32),
        compiler_params=_params("arbitrary", "arbitrary"),
        name="ada_mod",
    )(c, w_ada, b_ada.reshape(L, 1, D6))


def _in_proj_kernel(x_ref, sc_ref, sh_ref, wa_ref, wt_ref, vng_ref, vnb_ref, wst_ref, bs_ref,
                    kk_ref, kvc_ref, qt_ref, vt_ref, gates_ref, mlp_ref):
    tm = x_ref.shape[0]
    h = (x_ref[...] * (1.0 + sc_ref[...]) + sh_ref[...]).astype(BF16)
    a = _nn(h, wa_ref[...])
    kk_ref[...] = a[:, :2 * KV_WIDTH].astype(BF16)
    kvc_ref[0] = a[:, 2 * KV_WIDTH:3 * KV_WIDTH].astype(BF16)
    kvc_ref[1] = a[:, 3 * KV_WIDTH:].astype(BF16)
    t = _nt(wt_ref[...], h)
    qt_ref[...] = (t[:ATTN_WIDTH] * Q_SCALE).astype(BF16)
    vt_ref[...] = t[ROW_VSLC:ROW_GATE].astype(BF16)
    gates_ref[...] = t[ROW_GATE:ROW_U]

    u = _gelu(t[ROW_U:ROW_V])
    v = _gelu(t[ROW_V:]).reshape(N_MLP_GROUPS, MLP_GROUP_DIM, tm)
    mu = jnp.mean(v, axis=1, keepdims=True)
    vc = v - mu
    var = jnp.mean(vc * vc, axis=1, keepdims=True)
    vn = vc * lax.rsqrt(var + LN_EPS) * vng_ref[...] + vnb_ref[...]
    n_chunks = tm // CHUNK
    s_idx = lax.broadcasted_iota(jnp.int32, (CHUNK, CHUNK), 0)
    t_idx = lax.broadcasted_iota(jnp.int32, (CHUNK, CHUNK), 1)
    for g in range(N_MLP_GROUPS):
        vg = vn[g].astype(BF16)
        stack = jnp.concatenate([vg[:, c * CHUNK:(c + 1) * CHUNK] for c in range(n_chunks)], axis=0)
        w_t = jnp.where(s_idx <= t_idx, wst_ref[g], 0.0).astype(BF16)
        sv = _nn(stack, w_t)
        bias = bs_ref[g:g + 1, :]
        for c in range(n_chunks):
            mlp_ref[g * MLP_GROUP_DIM:(g + 1) * MLP_GROUP_DIM, c * CHUNK:(c + 1) * CHUNK] = (
                u[g * MLP_GROUP_DIM:(g + 1) * MLP_GROUP_DIM, c * CHUNK:(c + 1) * CHUNK]
                * (sv[c * MLP_GROUP_DIM:(c + 1) * MLP_GROUP_DIM] + bias))


def _in_proj_call(x, sc, sh, w_a, w_t, vn_g, vn_b, w_st, b_s):
    B, S, D = x.shape
    tm = min(TOK_TILE, S)
    const2 = lambda b, i: (0, 0)
    const3 = lambda b, i: (0, 0, 0)
    cm = lambda rows: pl.BlockSpec((None, rows, tm), lambda b, i: (b, 0, i))
    return pl.pallas_call(
        _in_proj_kernel,
        grid=(B, S // tm),
        in_specs=[pl.BlockSpec((None, tm, D), lambda b, i: (b, i, 0)),
                  pl.BlockSpec((None, 1, D), lambda b, i: (b, 0, 0)),
                  pl.BlockSpec((None, 1, D), lambda b, i: (b, 0, 0)),
                  pl.BlockSpec(w_a.shape, const2),
                  pl.BlockSpec(w_t.shape, const2),
                  pl.BlockSpec(vn_g.shape, const3),
                  pl.BlockSpec(vn_b.shape, const3),
                  pl.BlockSpec(w_st.shape, const3),
                  pl.BlockSpec(b_s.shape, const2)],
        out_specs=[pl.BlockSpec((None, tm, 2 * KV_WIDTH), lambda b, i: (b, i, 0)),
                   pl.BlockSpec((2, None, tm, KV_WIDTH), lambda b, i: (0, b, i, 0)),
                   cm(ATTN_WIDTH), cm(2 * KV_WIDTH), cm(N_KV_HEADS * GATE_ROWS), cm(MLP_WIDTH)],
        out_shape=[jax.ShapeDtypeStruct((B, S, 2 * KV_WIDTH), BF16),
                   jax.ShapeDtypeStruct((2, B, S, KV_WIDTH), BF16),
                   jax.ShapeDtypeStruct((B, ATTN_WIDTH, S), BF16),
                   jax.ShapeDtypeStruct((B, 2 * KV_WIDTH, S), BF16),
                   jax.ShapeDtypeStruct((B, N_KV_HEADS * GATE_ROWS, S), F32),
                   jax.ShapeDtypeStruct((B, MLP_WIDTH, S), F32)],
        compiler_params=_params("arbitrary", "arbitrary"),
        name="in_proj",
    )(x, sc, sh, w_a, w_t, vn_g, vn_b, w_st, b_s)


def _compress_kernel(tok_ref, wbig_ref, w1_ref, pos_ref, w2_ref, out_ref, out_t_ref):
    ab = _nn(tok_ref[...], wbig_ref[...])
    bias = _nn(pos_ref[...].astype(BF16), w1_ref[...])[0:1]
    ngrp = ab.shape[0]
    for h in range(N_KV_HEADS):
        base = h * 2 * CMP_HIDDEN
        first = ab[:, base:base + CMP_HIDDEN]
        second = ab[:, base + CMP_HIDDEN:base + 2 * CMP_HIDDEN]
        hid = _gelu(first + pltpu.roll(second, ngrp - 1, 0) + bias)
        c = _nn(hid.astype(BF16), w2_ref[...])
        block_end = CMP_STRIDE * lax.broadcasted_iota(jnp.int32, (ngrp, HEAD_DIM), 0) + (CMP_LEN - 1)
        out_ref[h, :, 0:HEAD_DIM] = c.astype(BF16)
        out_ref[h, :, HEAD_DIM:] = _alibi_key_cols(block_end).astype(BF16)
        out_t_ref[h] = c.T.astype(BF16)


def _compress_call(tok, wbig, w1, pos, w2):
    two, B, ngrp, flat = tok.shape
    return pl.pallas_call(
        _compress_kernel,
        grid=(two, B),
        in_specs=[pl.BlockSpec((None, None, ngrp, flat), lambda t, b: (t, b, 0, 0)),
                  pl.BlockSpec((None,) + wbig.shape[1:], lambda t, b: (t, 0, 0)),
                  pl.BlockSpec((None,) + w1.shape[1:], lambda t, b: (t, 0, 0)),
                  pl.BlockSpec((None,) + pos.shape[1:], lambda t, b: (t, 0, 0)),
                  pl.BlockSpec((None,) + w2.shape[1:], lambda t, b: (t, 0, 0))],
        out_specs=[pl.BlockSpec((None, None, N_KV_HEADS, ngrp, 2 * HEAD_DIM), lambda t, b: (t, b, 0, 0, 0)),
                   pl.BlockSpec((None, None, N_KV_HEADS, HEAD_DIM, ngrp), lambda t, b: (t, b, 0, 0, 0))],
        out_shape=[jax.ShapeDtypeStruct((two, B, N_KV_HEADS, ngrp, 2 * HEAD_DIM), BF16),
                   jax.ShapeDtypeStruct((two, B, N_KV_HEADS, HEAD_DIM, ngrp), BF16)],
        compiler_params=_params("arbitrary", "arbitrary"),
        name="compress",
    )(tok, wbig, w1, pos, w2)


def _cmp_attn_kernel(qt_ref, kc_ref, vct_ref, gates_ref, oc_ref, bias_ref, score_ref, left_ref, *, n_slc, topk):
    k_head = pl.program_id(1)
    i = pl.program_id(2)
    tq = qt_ref.shape[1]
    ncmp = kc_ref.shape[0]
    t0 = i * tq
    tile4 = lambda a: jnp.concatenate([a] * GQA_GROUP, axis=1)
    t_col = t0 + lax.broadcasted_iota(jnp.int32, (ncmp, tq), 1)
    cmp_end = CMP_STRIDE * lax.broadcasted_iota(jnp.int32, (ncmp, tq), 0) + (CMP_LEN - 1)
    ok = tile4(t_col >= cmp_end)
    any_ok = tile4((t0 + lax.broadcasted_iota(jnp.int32, (1, tq), 1)) >= (CMP_LEN - 1))
    t0f = t0.astype(F32)
    q_aug = jnp.concatenate(
        [jnp.concatenate([qt_ref[g * HEAD_DIM:(g + 1) * HEAD_DIM, :] for g in range(GQA_GROUP)], axis=1),
         jnp.concatenate([_alibi_query_rows(_slope(k_head, g), t0f, HEAD_DIM, tq) for g in range(GQA_GROUP)],
                         axis=1).astype(BF16)], axis=0)
    s = jnp.where(ok, _nn(kc_ref[...], q_aug), NEG)
    e = jnp.exp2(s - jnp.max(s, axis=0, keepdims=True))
    ones = jnp.where(lax.broadcasted_iota(jnp.int32, (16, ncmp), 0) == 0, 1.0, 0.0).astype(BF16)
    o = _nn(jnp.concatenate([vct_ref[...], ones], axis=0), e.astype(BF16))
    inv = jnp.where(any_ok, 1.0 / o[HEAD_DIM:HEAD_DIM + 1], 0.0)
    p = e * inv
    p_sum = sum(p[:, g * tq:(g + 1) * tq] for g in range(GQA_GROUP))
    out = o[:HEAD_DIM] * inv
    for g in range(GQA_GROUP):
        gate = _sigmoid(gates_ref[g:g + 1, :])
        oc_ref[g * HEAD_DIM:(g + 1) * HEAD_DIM, :] = out[:, g * tq:(g + 1) * tq] * gate

    j_i = lax.broadcasted_iota(jnp.int32, (MAX_SLC_BLOCKS, ncmp), 0)
    n_i = lax.broadcasted_iota(jnp.int32, (MAX_SLC_BLOCKS, ncmp), 1)
    overlap = ((CMP_STRIDE * n_i <= SLC_LEN * j_i + (SLC_LEN - 1))
               & (CMP_STRIDE * n_i + (CMP_LEN - 1) >= SLC_LEN * j_i)).astype(BF16)
    p_hi = p_sum.astype(BF16)
    r1 = p_sum - p_hi.astype(F32)
    p_mid = r1.astype(BF16)
    p_lo = (r1 - p_mid.astype(F32)).astype(BF16)
    score = _nn(overlap, p_hi) + _nn(overlap, p_mid) + _nn(overlap, p_lo)

    j_row = lax.broadcasted_iota(jnp.int32, (MAX_SLC_BLOCKS, tq), 0)
    t_blk = t0 + lax.broadcasted_iota(jnp.int32, (MAX_SLC_BLOCKS, tq), 1)
    cur = t_blk // SLC_LEN
    valid = (SLC_LEN * j_row <= t_blk) & (j_row < n_slc)
    forced = (j_row == 0) | (j_row == cur) | (j_row == cur - 1)
    cand = valid & jnp.logical_not(forced)
    n_forced = 1 + (cur[0:1] >= 1).astype(jnp.int32) + (cur[0:1] >= 2).astype(jnp.int32)
    remaining = jnp.broadcast_to(topk - n_forced, (8, tq))
    score = jnp.where(cand, score, -jnp.inf)
    n_groups = tq // PICK_LANES
    for lb in range(n_groups):
        score_ref[lb] = score[:, lb * PICK_LANES:(lb + 1) * PICK_LANES]
        left_ref[lb] = remaining[:, lb * PICK_LANES:(lb + 1) * PICK_LANES]
    j_f = lax.broadcasted_iota(jnp.int32, (MAX_SLC_BLOCKS, LANES), 0).astype(F32)

    def pick_rounds(lb, rounds):
        halves = [score_ref[lb, :, h * LANES:(h + 1) * LANES] for h in range(PICK_LANES // LANES)]
        left = [left_ref[lb, 0:1, h * LANES:(h + 1) * LANES] for h in range(PICK_LANES // LANES)]
        for r in rounds:
            for h, sc in enumerate(halves):
                m = jnp.max(sc, axis=0, keepdims=True)
                first = jnp.min(jnp.where(sc == m, j_f, float(MAX_SLC_BLOCKS)), axis=0, keepdims=True)
                first = jnp.where(left[h] > r, first, -1.0)
                halves[h] = jnp.where(j_f == first, -jnp.inf, sc)
        score_ref[lb] = jnp.concatenate(halves, axis=1)

    base_rounds = topk - 3

    def group(lb, carry):
        pick_rounds(lb, range(base_rounds))
        return carry

    lax.fori_loop(0, n_groups, group, 0)

    @pl.when(t0 < 2 * SLC_LEN)
    def _early_queries():
        pick_rounds(0, range(base_rounds, topk - 1))

    picked = jnp.concatenate([score_ref[lb] for lb in range(n_groups)], axis=1) == -jnp.inf
    bias_ref[...] = jnp.where((valid & forced) | (cand & picked), 0.0, NEG).astype(BF16)


def _cmp_attn_call(qt, cmp_kv, cmp_kv_t, gates, *, n_slc, topk):
    B, _, S = qt.shape
    ncmp = cmp_kv.shape[3]
    tq = min(CMP_Q_TILE, S)
    assert topk >= 3 and tq >= 2 * SLC_LEN
    kern = functools.partial(_cmp_attn_kernel, n_slc=n_slc, topk=topk)
    return pl.pallas_call(
        kern,
        grid=(B, N_KV_HEADS, S // tq),
        in_specs=[pl.BlockSpec((None, GROUP_WIDTH, tq), lambda b, k, i: (b, k, i)),
                  pl.BlockSpec((None, None, None, ncmp, 2 * HEAD_DIM), lambda b, k, i: (0, b, k, 0, 0)),
                  pl.BlockSpec((None, None, None, HEAD_DIM, ncmp), lambda b, k, i: (1, b, k, 0, 0)),
                  pl.BlockSpec((None, None, GATE_ROWS, tq), lambda b, k, i: (b, k, 0, i))],
        out_specs=[pl.BlockSpec((None, GROUP_WIDTH, tq), lambda b, k, i: (b, k, i)),
                   pl.BlockSpec((None, None, MAX_SLC_BLOCKS, tq), lambda b, k, i: (b, k, 0, i))],
        out_shape=[jax.ShapeDtypeStruct((B, ATTN_WIDTH, S), F32),
                   jax.ShapeDtypeStruct((B, N_KV_HEADS, MAX_SLC_BLOCKS, S), BF16)],
        scratch_shapes=[pltpu.VMEM((tq // PICK_LANES, MAX_SLC_BLOCKS, PICK_LANES), F32),
                        pltpu.VMEM((tq // PICK_LANES, 8, PICK_LANES), jnp.int32)],
        compiler_params=_params("arbitrary", "arbitrary", "arbitrary"),
        name="cmp_attn",
    )(qt, cmp_kv, cmp_kv_t, gates)


V_ROWS = HEAD_DIM + 16


def _slc_attn_kernel(qt_ref, bias_ref, ks_ref, vt_ref, gates_ref, os_ref,
                     kaug_ref, vh_ref, qaug_ref, m_ref, acc_ref, sa_ref, sb_ref, ta_ref, tb_ref):
    k_head = pl.program_id(1)
    i = pl.program_id(2)
    tq = qt_ref.shape[1]
    n_tiles = kaug_ref.shape[0]
    t0 = i * tq

    @pl.when(i == 0)
    def _build_keys():
        def fill(c, carry):
            r0 = pl.multiple_of(c * tq, tq)
            pos = r0 + lax.broadcasted_iota(jnp.int32, (tq, HEAD_DIM), 0)
            kaug_ref[c, :, 0:HEAD_DIM] = _pick_head_lanes(ks_ref[pl.ds(r0, tq), :], k_head)
            kaug_ref[c, :, HEAD_DIM:2 * HEAD_DIM] = _alibi_key_cols(pos).astype(BF16)
            blk = (r0 + lax.broadcasted_iota(jnp.int32, (tq, MAX_SLC_BLOCKS), 0)) // SLC_LEN
            hot = blk == lax.broadcasted_iota(jnp.int32, (tq, MAX_SLC_BLOCKS), 1)
            kaug_ref[c, :, 2 * HEAD_DIM:] = jnp.where(hot, 1.0, 0.0).astype(BF16)
            return carry

        lax.fori_loop(0, n_tiles, fill, 0)
        ones_row = lax.broadcasted_iota(jnp.int32, (V_ROWS - HEAD_DIM, tq), 0) == 0
        for c in range(n_tiles):
            vh_ref[c, 0:HEAD_DIM, :] = _pick_head_rows(vt_ref[:, c * tq:(c + 1) * tq], k_head)
            vh_ref[c, HEAD_DIM:, :] = jnp.where(ones_row, 1.0, 0.0).astype(BF16)

    t0f = t0.astype(F32)
    bias = bias_ref[...]
    for g in range(GQA_GROUP):
        aux = _alibi_query_rows(_slope(k_head, g), t0f, HEAD_DIM, tq)
        qaug_ref[0:HEAD_DIM, g * tq:(g + 1) * tq] = qt_ref[g * HEAD_DIM:(g + 1) * HEAD_DIM, :]
        qaug_ref[HEAD_DIM:2 * HEAD_DIM, g * tq:(g + 1) * tq] = aux.astype(BF16)
        qaug_ref[2 * HEAD_DIM:, g * tq:(g + 1) * tq] = bias
    m_ref[...] = jnp.full(m_ref.shape, NEG, F32)
    acc_ref[...] = jnp.zeros(acc_ref.shape, F32)

    s_bufs = (sa_ref, sb_ref)

    tmax_bufs = (ta_ref, tb_ref)

    def scores_head(k_tile, g, slot, causal):
        cols = slice(g * tq, (g + 1) * tq)
        s = _nn(k_tile, qaug_ref[:, cols])
        if causal:
            key = lax.broadcasted_iota(jnp.int32, (tq, tq), 0)
            qry = lax.broadcasted_iota(jnp.int32, (tq, tq), 1)
            s = jnp.where(key <= qry, s, NEG)
        s_bufs[slot][:, cols] = s
        tmax_bufs[slot][:, cols] = jnp.max(s, axis=0, keepdims=True)

    def scores(j, slot, causal):
        k_tile = kaug_ref[j]
        for g in range(GQA_GROUP):
            scores_head(k_tile, g, slot, causal)

    def step(j, slot, prefetch):
        v_tile = vh_ref[j]
        k_next = kaug_ref[j + 1] if prefetch is not None else None
        for g in range(GQA_GROUP):
            cols = slice(g * tq, (g + 1) * tq)
            if prefetch is not None:
                scores_head(k_next, g, 1 - slot, prefetch == 'causal')
            m_old = m_ref[:, cols]
            m_new = jnp.maximum(m_old, tmax_bufs[slot][:, cols])
            alpha = jnp.exp2(m_old - m_new)
            p = jnp.exp2(s_bufs[slot][:, cols] - m_new).astype(BF16)
            acc_ref[:, cols] = alpha * acc_ref[:, cols] + _nn(v_tile, p)
            m_ref[:, cols] = m_new

    @pl.when(i == 0)
    def _first_is_causal():
        scores(0, 0, True)

    @pl.when(i > 0)
    def _first_is_plain():
        scores(0, 0, False)

    def pair(p, carry):
        step(2 * p, 0, 'plain')
        step(2 * p + 1, 1, 'plain')
        return carry

    n_pairs = jnp.maximum(i - 1, 0) // 2
    lax.fori_loop(0, n_pairs, pair, 0)

    @pl.when(i == 0)
    def _tail_single():
        step(0, 0, None)

    @pl.when(i % 2 == 1)
    def _tail_odd():
        step(i - 1, 0, 'causal')
        step(i, 1, None)

    @pl.when((i % 2 == 0) & (i > 0))
    def _tail_even():
        step(i - 2, 0, 'plain')
        step(i - 1, 1, 'causal')
        step(i, 0, None)

    o = acc_ref[0:HEAD_DIM, :] / acc_ref[HEAD_DIM:HEAD_DIM + 1, :]
    for g in range(GQA_GROUP):
        gate = _sigmoid(gates_ref[GQA_GROUP + g:GQA_GROUP + g + 1, :])
        os_ref[g * HEAD_DIM:(g + 1) * HEAD_DIM, :] = o[:, g * tq:(g + 1) * tq] * gate


def _slc_attn_call(qt, bias, kk, vt, gates):
    B, _, S = qt.shape
    tq = min(SLC_TILE, S)
    return pl.pallas_call(
        _slc_attn_kernel,
        grid=(B, N_KV_HEADS, S // tq),
        in_specs=[pl.BlockSpec((None, GROUP_WIDTH, tq), lambda b, k, i: (b, k, i)),
                  pl.BlockSpec((None, None, MAX_SLC_BLOCKS, tq), lambda b, k, i: (b, k, 0, i)),
                  pl.BlockSpec((None, S, KV_WIDTH), lambda b, k, i: (b, 0, 0)),
                  pl.BlockSpec((None, KV_WIDTH, S), lambda b, k, i: (b, 0, 0)),
                  pl.BlockSpec((None, None, GATE_ROWS, tq), lambda b, k, i: (b, k, 0, i))],
        out_specs=pl.BlockSpec((None, GROUP_WIDTH, tq), lambda b, k, i: (b, k, i)),
        out_shape=jax.ShapeDtypeStruct((B, ATTN_WIDTH, S), F32),
        scratch_shapes=[pltpu.VMEM((S // tq, tq, 2 * LANES), BF16),
                        pltpu.VMEM((S // tq, V_ROWS, tq), BF16),
                        pltpu.VMEM((2 * LANES, GQA_GROUP * tq), BF16),
                        pltpu.VMEM((1, GQA_GROUP * tq), F32),
                        pltpu.VMEM((V_ROWS, GQA_GROUP * tq), F32),
                        pltpu.VMEM((tq, GQA_GROUP * tq), F32),
                        pltpu.VMEM((tq, GQA_GROUP * tq), F32),
                        pltpu.VMEM((1, GQA_GROUP * tq), F32),
                        pltpu.VMEM((1, GQA_GROUP * tq), F32)],
        compiler_params=_params("arbitrary", "arbitrary", "arbitrary"),
        name="slc_attn",
    )(qt, bias, kk, vt, gates)


def _win_attn_kernel(qt_ref, k2_ref, k1_ref, k0_ref, v2_ref, v1_ref, v0_ref, gates_ref, ow_ref):
    k_head = pl.program_id(1)
    i = pl.program_id(2)
    tq = qt_ref.shape[1]
    t0f = (i * tq).astype(F32)
    qs = jnp.concatenate([qt_ref[g * HEAD_DIM:(g + 1) * HEAD_DIM, :] for g in range(GQA_GROUP)], axis=1)
    alibi = jnp.concatenate([_alibi_query_rows(_slope(k_head, g), t0f, KV_WIDTH, tq, row8=NEG)
                             for g in range(GQA_GROUP)], axis=1).astype(BF16)
    q2 = jnp.concatenate([jnp.where(k_head == k, qs, jnp.zeros_like(qs)) for k in range(N_KV_HEADS)] + [alibi],
                         axis=0)
    key = lax.broadcasted_iota(jnp.int32, (tq, tq), 0)
    qry = lax.broadcasted_iota(jnp.int32, (tq, tq), 1)
    keeps = (qry < key, None, key <= qry)
    ones = jnp.where(lax.broadcasted_iota(jnp.int32, (16, tq), 0) == 0, 1.0, 0.0).astype(BF16)
    lane = lax.broadcasted_iota(jnp.int32, (tq, KV_WIDTH), 1)
    cols_in_tile = _alibi_key_cols(lax.broadcasted_iota(jnp.int32, (tq, KV_WIDTH), 0))
    high_lanes = jnp.where((lane == 0) | (lane == 2), 1.0, 0.0)
    flag_lane = jnp.where(lane == 8, 1.0, 0.0)
    k_augs, v_tiles = [], []
    for back, k_ref, v_ref in zip((2, 1, 0), (k2_ref, k1_ref, k0_ref), (v2_ref, v1_ref, v0_ref)):
        tile_high = ((i - back) * (tq // LANES)).astype(F32)
        before_start = jnp.where(i < back, 1.0, 0.0)
        k_aux = cols_in_tile + high_lanes * tile_high + flag_lane * before_start
        k_augs.append(jnp.concatenate([k_ref[...], k_aux.astype(BF16)], axis=1))
        v_tiles.append(jnp.concatenate([v_ref[...], ones], axis=0))
    mask = lambda keep, s: s if keep is None else jnp.where(keep, s, NEG)
    scores = [[mask(keep, _nn(k_aug, q2[:, g * tq:(g + 1) * tq])) for k_aug, keep in zip(k_augs, keeps)]
              for g in range(GQA_GROUP)]
    for g in range(GQA_GROUP):
        m = functools.reduce(jnp.maximum, [jnp.max(s, axis=0, keepdims=True) for s in scores[g]])
        o2 = sum(_nn(v, jnp.exp2(s - m).astype(BF16)) for v, s in zip(v_tiles, scores[g]))
        o = _pick_head_rows(o2[:KV_WIDTH], k_head) / o2[KV_WIDTH:KV_WIDTH + 1]
        gate = _sigmoid(gates_ref[2 * GQA_GROUP + g:2 * GQA_GROUP + g + 1, :])
        ow_ref[g * HEAD_DIM:(g + 1) * HEAD_DIM, :] = o * gate


def _win_attn_call(qt, kk, vt, gates):
    B, _, S = qt.shape
    tq = min(Q_TILE, S)
    assert WINDOW == 2 * tq
    k_spec = lambda back: pl.BlockSpec((None, tq, KV_WIDTH), lambda b, k, i: (b, jnp.maximum(i - back, 0), 1))
    v_spec = lambda back: pl.BlockSpec((None, KV_WIDTH, tq), lambda b, k, i: (b, 1, jnp.maximum(i - back, 0)))
    return pl.pallas_call(
        _win_attn_kernel,
        grid=(B, N_KV_HEADS, S // tq),
        in_specs=[pl.BlockSpec((None, GROUP_WIDTH, tq), lambda b, k, i: (b, k, i)),
                  k_spec(2), k_spec(1), k_spec(0), v_spec(2), v_spec(1), v_spec(0),
                  pl.BlockSpec((None, None, GATE_ROWS, tq), lambda b, k, i: (b, k, 0, i))],
        out_specs=pl.BlockSpec((None, GROUP_WIDTH, tq), lambda b, k, i: (b, k, i)),
        out_shape=jax.ShapeDtypeStruct((B, ATTN_WIDTH, S), F32),
        compiler_params=_params("arbitrary", "arbitrary", "arbitrary"),
        name="win_attn",
    )(qt, kk, kk, kk, vt, vt, vt, gates)


def _group_rms(a, gain):
    c, tm = a.shape
    a3 = a.reshape(c // HEAD_DIM, HEAD_DIM, tm)
    ms = jnp.mean(a3 * a3, axis=1, keepdims=True)
    return (a3 * lax.rsqrt(ms + LN_EPS) * gain).reshape(c, tm)


def _out_proj_kernel(oc_ref, os_ref, ow_ref, mlp_ref, x_ref, g1_ref, og_ref, wo_ref, lng_ref, lnb_ref, o_ref, *, alpha):
    n_attn = ATTN_WIDTH // HEAD_DIM
    attn = _group_rms(oc_ref[...] + os_ref[...] + ow_ref[...], og_ref[:n_attn])
    mlp = _group_rms(mlp_ref[...], og_ref[n_attn:])
    y_t = jnp.concatenate([attn, mlp], axis=0).astype(BF16)
    y = _tn(y_t, wo_ref[...])
    z = alpha * x_ref[...] + (1.0 + g1_ref[...]) * y
    o_ref[...] = _layer_norm_rows(z, lng_ref[...], lnb_ref[...])


def _out_proj_call(oc, os_, ow, mlp, x, g1, out_g, w_o, ln_g, ln_b, *, alpha):
    B, S, D = x.shape
    tm = min(TOK_TILE, S)
    cm = lambda width: pl.BlockSpec((None, width, tm), lambda b, i: (b, 0, i))
    row = pl.BlockSpec((1, D), lambda b, i: (0, 0))
    return pl.pallas_call(
        functools.partial(_out_proj_kernel, alpha=alpha),
        grid=(B, S // tm),
        in_specs=[cm(ATTN_WIDTH), cm(ATTN_WIDTH), cm(ATTN_WIDTH), cm(MLP_WIDTH),
                  pl.BlockSpec((None, tm, D), lambda b, i: (b, i, 0)),
                  pl.BlockSpec((None, 1, D), lambda b, i: (b, 0, 0)),
                  pl.BlockSpec(out_g.shape, lambda b, i: (0, 0, 0)),
                  pl.BlockSpec(w_o.shape, lambda b, i: (0, 0)),
                  row, row],
        out_specs=pl.BlockSpec((None, tm, D), lambda b, i: (b, i, 0)),
        out_shape=jax.ShapeDtypeStruct((B, S, D), F32),
        compiler_params=_params("arbitrary", "arbitrary"),
        name="out_proj",
    )(oc, os_, ow, mlp, x, g1, out_g, w_o, ln_g, ln_b)


def _ffn_kernel(x_ref, sc_ref, sh_ref, g2_ref, w1_ref, w3_ref, w2_ref, lng_ref, lnb_ref, o_ref, *, alpha, splits):
    x = x_ref[...]
    h = (x * (1.0 + sc_ref[...]) + sh_ref[...]).astype(BF16)
    f = None
    for lo, hi in splits:
        a = _nn(h, w1_ref[:, lo:hi])
        b = _nn(h, w3_ref[:, lo:hi])
        part = _nn((a * _sigmoid(a) * b).astype(BF16), w2_ref[lo:hi, :])
        f = part if f is None else f + part
    z = alpha * x + (1.0 + g2_ref[...]) * f
    o_ref[...] = _layer_norm_rows(z, lng_ref[...], lnb_ref[...])


def _ffn_call(x, sc, sh, g2, w1, w3, w2, ln_g, ln_b, *, alpha):
    B, S, D = x.shape
    d_ff = w1.shape[1]
    tm = min(TOK_TILE, S)
    half = (d_ff // 2 + 255) // 256 * 256
    splits = ((0, half), (half, d_ff))
    mod = pl.BlockSpec((None, 1, D), lambda b, i: (b, 0, 0))
    row = pl.BlockSpec((1, D), lambda b, i: (0, 0))
    resident = lambda shape: pl.BlockSpec(shape, lambda b, i: (0, 0), pipeline_mode=pl.Buffered(1))
    return pl.pallas_call(
        functools.partial(_ffn_kernel, alpha=alpha, splits=splits),
        grid=(B, S // tm),
        in_specs=[pl.BlockSpec((None, tm, D), lambda b, i: (b, i, 0)), mod, mod, mod,
                  resident(w1.shape), resident(w3.shape), resident(w2.shape), row, row],
        out_specs=pl.BlockSpec((None, tm, D), lambda b, i: (b, i, 0)),
        out_shape=jax.ShapeDtypeStruct((B, S, D), F32),
        compiler_params=_params("arbitrary", "arbitrary"),
        name="ffn",
    )(x, sc, sh, g2, w1, w3, w2, ln_g, ln_b)


def _in_proj_weights(w_in):
    sizes = (ATTN_WIDTH,) + (KV_WIDTH,) * 6 + (N_BRANCH * N_ATTN_HEADS, MLP_WIDTH, MLP_WIDTH)
    offs = [0]
    for s in sizes:
        offs.append(offs[-1] + s)
    col = lambda n: w_in[:, offs[n]:offs[n + 1]]
    q, kc, vc, ksl, vsl, kwn, vwn, gt, u, v = (col(n) for n in range(10))
    w_a = jnp.concatenate([ksl, kwn, kc, vc], axis=1).astype(BF16)
    D = w_in.shape[0]
    gt = gt.reshape(D, N_KV_HEADS, GQA_GROUP, N_BRANCH).transpose(1, 3, 2, 0)
    gt = gt.reshape(N_KV_HEADS, N_BRANCH * GQA_GROUP, D)
    gt = jnp.pad(gt, ((0, 0), (0, GATE_ROWS - N_BRANCH * GQA_GROUP), (0, 0)))
    gt = gt.reshape(N_KV_HEADS * GATE_ROWS, D)
    w_t = jnp.concatenate([q.T, vsl.T, vwn.T, gt, u.T, v.T], axis=0).astype(BF16)
    return w_a, w_t


def _compress_weights(cmp_w1):
    two, _, hid = cmp_w1.shape
    w = cmp_w1.reshape(two, 2, CMP_STRIDE, HEAD_DIM, hid)
    eye = jnp.eye(N_KV_HEADS, dtype=cmp_w1.dtype)
    big = jnp.einsum('thpdc,kj->tpkdjhc', w, eye)
    return big.reshape(two, CMP_STRIDE * N_KV_HEADS * HEAD_DIM, N_KV_HEADS * 2 * hid).astype(BF16)


def _hybrid_layer(x, mod, w_in, cmp_pos, cmp_w1, cmp_w2, vn_g, vn_b, w_s, b_s, out_g, w_o, ln1_g, ln1_b,
                  w1, w3, w2, ln2_g, ln2_b, *, alpha):
    B, S, D = x.shape
    assert S % Q_TILE == 0 and S % TOK_TILE == 0 and S // SLC_LEN <= MAX_SLC_BLOCKS
    sh1, sc1, g1, sh2, sc2, g2 = (mod[:, None, n * D:(n + 1) * D] for n in range(6))
    w_a, w_t = _in_proj_weights(w_in)
    kk, kvc, qt, vt, gates, mlp = _in_proj_call(
        x, sc1, sh1, w_a, w_t,
        vn_g.reshape(N_MLP_GROUPS, MLP_GROUP_DIM, 1), vn_b.reshape(N_MLP_GROUPS, MLP_GROUP_DIM, 1),
        jnp.swapaxes(w_s, 1, 2), b_s)
    ngrp = S // CMP_STRIDE
    tok = kvc.reshape(2, B, ngrp, CMP_STRIDE * KV_WIDTH)
    pos = jnp.broadcast_to(cmp_pos.reshape(2, 1, CMP_LEN * HEAD_DIM), (2, 8, CMP_LEN * HEAD_DIM))
    cmp_kv, cmp_kv_t = _compress_call(tok, _compress_weights(cmp_w1), cmp_w1.astype(BF16), pos, cmp_w2.astype(BF16))
    gates = gates.reshape(B, N_KV_HEADS, GATE_ROWS, S)
    n_slc = S // SLC_LEN
    oc, bias = _cmp_attn_call(qt, cmp_kv, cmp_kv_t, gates, n_slc=n_slc, topk=min(SLC_TOPK, n_slc))
    os_ = _slc_attn_call(qt, bias, kk, vt, gates)
    ow = _win_attn_call(qt, kk, vt, gates)
    row = lambda a: a.reshape(1, D)
    x = _out_proj_call(oc, os_, ow, mlp, x, g1, out_g.reshape(-1, HEAD_DIM, 1), w_o.astype(BF16),
                       row(ln1_g), row(ln1_b), alpha=alpha)
    return _ffn_call(x, sc2, sh2, g2, w1.astype(BF16), w3.astype(BF16), w2.astype(BF16),
                     row(ln2_g), row(ln2_b), alpha=alpha)


def kernel(x, c, w_ada, b_ada, w_in, cmp_pos, cmp_w1, cmp_w2, vn_g, vn_b, w_s, b_s, out_g, w_o, ln1_g, ln1_b,
           w1, w3, w2, ln2_g, ln2_b):
    depth = w_ada.shape[0]
    alpha = (2.0 * depth) ** 0.25
    mod = _ada_call(c, w_ada, b_ada)
    for l in range(depth):
        x = _hybrid_layer(x, mod[l], w_in[l], cmp_pos[l], cmp_w1[l], cmp_w2[l], vn_g[l], vn_b[l], w_s[l], b_s[l],
                          out_g[l], w_o[l], ln1_g[l], ln1_b[l], w1[l], w3[l], w2[l], ln2_g[l], ln2_b[l], alpha=alpha)
    return x
```

```python
import functools
import math

import jax
import jax.numpy as jnp
from jax import lax
from jax.experimental import pallas as pl
from jax.experimental.pallas import tpu as pltpu

F32 = jnp.float32
BF16 = jnp.bfloat16

HEAD_DIM = 64
N_KV_HEADS = 2
GQA_GROUP = 4
N_ATTN_HEADS = N_KV_HEADS * GQA_GROUP
ATTN_WIDTH = N_ATTN_HEADS * HEAD_DIM
KV_WIDTH = N_KV_HEADS * HEAD_DIM
GROUP_WIDTH = GQA_GROUP * HEAD_DIM
N_BRANCH = 3
MLP_GROUP_DIM = 64
N_MLP_GROUPS = 8
MLP_WIDTH = N_MLP_GROUPS * MLP_GROUP_DIM
CMP_LEN = 32
CMP_STRIDE = 16
CMP_HIDDEN = 256
SLC_LEN = 64
SLC_TOPK = 16
WINDOW = 512
CHUNK = 128
LN_EPS = 1e-5
NEG = -1e30
FORCED_SCORE = 1e6

LOG2E = math.log2(math.e)
Q_SCALE = HEAD_DIM ** -0.5 * LOG2E

LANES = 128
MAX_SLC_BLOCKS = LANES
TOK_TILE = 512
Q_TILE = 256
SLC_TILE = 512
CMP_Q_TILE = 512
PICK_LANES = 256
CMP_ROW_CHUNK = 128
VMEM_LIMIT = 56 * 1024 * 1024
GATE_ROWS = 16

ROW_VSLC = ATTN_WIDTH
ROW_VWIN = ROW_VSLC + KV_WIDTH
ROW_GATE = ROW_VWIN + KV_WIDTH
ROW_U = ROW_GATE + N_KV_HEADS * GATE_ROWS
ROW_V = ROW_U + MLP_WIDTH
ROWS_T = ROW_V + MLP_WIDTH
COLS_A = 4 * KV_WIDTH


def _nt(a, b):
    return lax.dot_general(a, b, (((1,), (1,)), ((), ())), preferred_element_type=F32)


def _tn(a, b):
    return lax.dot_general(a, b, (((0,), (0,)), ((), ())), preferred_element_type=F32)


def _nn(a, b):
    return jnp.dot(a, b, preferred_element_type=F32)


def _gelu(x):
    return x * (0.5 * (1.0 + jnp.tanh(math.sqrt(2.0 / math.pi) * (x + 0.044715 * (x * x * x)))))


def _sigmoid(x):
    return 1.0 / (1.0 + jnp.exp(-x))


def _layer_norm_rows(z, g, b):
    mu = jnp.mean(z, axis=-1, keepdims=True)
    zc = z - mu
    var = jnp.mean(zc * zc, axis=-1, keepdims=True)
    return zc * lax.rsqrt(var + LN_EPS) * g + b


def _slope(kv_head, g):
    slope = lambda h: LOG2E * 2.0 ** (-8.0 * (h + 1) / N_ATTN_HEADS)
    val = jnp.asarray(slope(g), F32)
    for k in range(1, N_KV_HEADS):
        val = jnp.where(kv_head == k, jnp.asarray(slope(k * GQA_GROUP + g), F32), val)
    return val


def _pick_head_lanes(x, kv_head):
    out = x[:, :HEAD_DIM]
    for k in range(1, N_KV_HEADS):
        out = jnp.where(kv_head == k, x[:, k * HEAD_DIM:(k + 1) * HEAD_DIM], out)
    return out


def _pick_head_rows(x, kv_head):
    out = x[:HEAD_DIM]
    for k in range(1, N_KV_HEADS):
        out = jnp.where(kv_head == k, x[k * HEAD_DIM:(k + 1) * HEAD_DIM], out)
    return out


def _bf16_part(x):
    return x.astype(BF16).astype(F32)


def _alibi_key_cols(pos):
    lane = lax.broadcasted_iota(jnp.int32, pos.shape, 1)
    cols = jnp.where(lane % 2 == 0, pos // LANES, pos % LANES)
    return jnp.where(lane < 4, cols, jnp.where(lane < 8, 1, 0)).astype(F32)


def _alibi_query_rows(c, t0f, rows, tq, row8=0.0):
    shape = (8, tq)
    row = lax.broadcasted_iota(jnp.int32, shape, 0)
    c = jnp.full(shape, c, F32)
    out = jnp.zeros(shape, F32)
    for n, part in enumerate((_bf16_part(c), _bf16_part(c - _bf16_part(c)))):
        shift = part * t0f
        vals = {2 * n: LANES * part, 2 * n + 1: part, 4 + 2 * n: -_bf16_part(shift), 5 + 2 * n: -(shift - _bf16_part(shift))}
        for r, val in vals.items():
            out = jnp.where(row == r, val, out)
    tail = jnp.where(lax.broadcasted_iota(jnp.int32, (rows - 8, tq), 0) == 0, row8, 0.0)
    return jnp.concatenate([out, tail], axis=0)


def _params(*sem):
    return pltpu.CompilerParams(dimension_semantics=sem, vmem_limit_bytes=VMEM_LIMIT)


def _ada_kernel(c_ref, w_ref, b_ref, o_ref):
    c = c_ref[...]
    c_act = (c * _sigmoid(c)).astype(BF16)
    o_ref[0] = _nn(c_act, w_ref[0].astype(BF16)) + b_ref[0]


def _ada_call(c, w_ada, b_ada):
    L, D, D6 = w_ada.shape
    B = c.shape[0]
    return pl.pallas_call(
        _ada_kernel,
        grid=(L, D6 // D),
        in_specs=[pl.BlockSpec((B, D), lambda l, n: (0, 0)),
                  pl.BlockSpec((1, D, D), lambda l, n: (l, 0, n)),
                  pl.BlockSpec((1, 1, D), lambda l, n: (l, 0, n))],
        out_specs=pl.BlockSpec((1, B, D), lambda l, n: (l, 0, n)),
        out_shape=jax.ShapeDtypeStruct((L, B, D6), F32),
        compiler_params=_params("arbitrary", "arbitrary"),
        name="ada_mod",
    )(c, w_ada, b_ada.reshape(L, 1, D6))


def _in_proj_kernel(x_ref, sc_ref, sh_ref, wa_ref, wt_ref, vng_ref, vnb_ref, wst_ref, bs_ref,
                    kk_ref, kvc_ref, qt_ref, vt_ref, gates_ref, mlp_ref):
    tm = x_ref.shape[0]
    h = (x_ref[...] * (1.0 + sc_ref[...]) + sh_ref[...]).astype(BF16)
    a = _nn(h, wa_ref[...])
    kk_ref[...] = a[:, :2 * KV_WIDTH].astype(BF16)
    kvc_ref[0] = a[:, 2 * KV_WIDTH:3 * KV_WIDTH].astype(BF16)
    kvc_ref[1] = a[:, 3 * KV_WIDTH:].astype(BF16)
    t = _nt(wt_ref[...], h)
    qt_ref[...] = (t[:ATTN_WIDTH] * Q_SCALE).astype(BF16)
    vt_ref[...] = t[ROW_VSLC:ROW_GATE].astype(BF16)
    gates_ref[...] = t[ROW_GATE:ROW_U]

    u = _gelu(t[ROW_U:ROW_V])
    v = _gelu(t[ROW_V:]).reshape(N_MLP_GROUPS, MLP_GROUP_DIM, tm)
    mu = jnp.mean(v, axis=1, keepdims=True)
    vc = v - mu
    var = jnp.mean(vc * vc, axis=1, keepdims=True)
    vn = vc * lax.rsqrt(var + LN_EPS) * vng_ref[...] + vnb_ref[...]
    n_chunks = tm // CHUNK
    s_idx = lax.broadcasted_iota(jnp.int32, (CHUNK, CHUNK), 0)
    t_idx = lax.broadcasted_iota(jnp.int32, (CHUNK, CHUNK), 1)
    for g in range(N_MLP_GROUPS):
        vg = vn[g].astype(BF16)
        stack = jnp.concatenate([vg[:, c * CHUNK:(c + 1) * CHUNK] for c in range(n_chunks)], axis=0)
        w_t = jnp.where(s_idx <= t_idx, wst_ref[g], 0.0).astype(BF16)
        sv = _nn(stack, w_t)
        bias = bs_ref[g:g + 1, :]
        for c in range(n_chunks):
            mlp_ref[g * MLP_GROUP_DIM:(g + 1) * MLP_GROUP_DIM, c * CHUNK:(c + 1) * CHUNK] = (
                u[g * MLP_GROUP_DIM:(g + 1) * MLP_GROUP_DIM, c * CHUNK:(c + 1) * CHUNK]
                * (sv[c * MLP_GROUP_DIM:(c + 1) * MLP_GROUP_DIM] + bias))


def _in_proj_call(x, sc, sh, w_a, w_t, vn_g, vn_b, w_st, b_s):
    B, S, D = x.shape
    tm = min(TOK_TILE, S)
    const2 = lambda b, i: (0, 0)
    const3 = lambda b, i: (0, 0, 0)
    cm = lambda rows: pl.BlockSpec((None, rows, tm), lambda b, i: (b, 0, i))
    return pl.pallas_call(
        _in_proj_kernel,
        grid=(B, S // tm),
        in_specs=[pl.BlockSpec((None, tm, D), lambda b, i: (b, i, 0)),
                  pl.BlockSpec((None, 1, D), lambda b, i: (b, 0, 0)),
                  pl.BlockSpec((None, 1, D), lambda b, i: (b, 0, 0)),
                  pl.BlockSpec(w_a.shape, const2),
                  pl.BlockSpec(w_t.shape, const2),
                  pl.BlockSpec(vn_g.shape, const3),
                  pl.BlockSpec(vn_b.shape, const3),
                  pl.BlockSpec(w_st.shape, const3),
                  pl.BlockSpec(b_s.shape, const2)],
        out_specs=[pl.BlockSpec((None, tm, 2 * KV_WIDTH), lambda b, i: (b, i, 0)),
                   pl.BlockSpec((2, None, tm, KV_WIDTH), lambda b, i: (0, b, i, 0)),
                   cm(ATTN_WIDTH), cm(2 * KV_WIDTH), cm(N_KV_HEADS * GATE_ROWS), cm(MLP_WIDTH)],
        out_shape=[jax.ShapeDtypeStruct((B, S, 2 * KV_WIDTH), BF16),
                   jax.ShapeDtypeStruct((2, B, S, KV_WIDTH), BF16),
                   jax.ShapeDtypeStruct((B, ATTN_WIDTH, S), BF16),
                   jax.ShapeDtypeStruct((B, 2 * KV_WIDTH, S), BF16),
                   jax.ShapeDtypeStruct((B, N_KV_HEADS * GATE_ROWS, S), F32),
                   jax.ShapeDtypeStruct((B, MLP_WIDTH, S), F32)],
        compiler_params=_params("arbitrary", "arbitrary"),
        name="in_proj",
    )(x, sc, sh, w_a, w_t, vn_g, vn_b, w_st, b_s)


def _compress_kernel(tok_ref, wbig_ref, w1_ref, pos_ref, w2_ref, out_ref, out_t_ref):
    ab = _nn(tok_ref[...], wbig_ref[...])
    bias = _nn(pos_ref[...].astype(BF16), w1_ref[...])[0:1]
    ngrp = ab.shape[0]
    for h in range(N_KV_HEADS):
        base = h * 2 * CMP_HIDDEN
        first = ab[:, base:base + CMP_HIDDEN]
        second = ab[:, base + CMP_HIDDEN:base + 2 * CMP_HIDDEN]
        hid = _gelu(first + pltpu.roll(second, ngrp - 1, 0) + bias)
        c = _nn(hid.astype(BF16), w2_ref[...])
        block_end = CMP_STRIDE * lax.broadcasted_iota(jnp.int32, (ngrp, HEAD_DIM), 0) + (CMP_LEN - 1)
        out_ref[h, :, 0:HEAD_DIM] = c.astype(BF16)
        out_ref[h, :, HEAD_DIM:] = _alibi_key_cols(block_end).astype(BF16)
        out_t_ref[h] = c.T.astype(BF16)


def _compress_call(tok, wbig, w1, pos, w2):
    two, B, ngrp, flat = tok.shape
    return pl.pallas_call(
        _compress_kernel,
        grid=(two, B),
        in_specs=[pl.BlockSpec((None, None, ngrp, flat), lambda t, b: (t, b, 0, 0)),
                  pl.BlockSpec((None,) + wbig.shape[1:], lambda t, b: (t, 0, 0)),
                  pl.BlockSpec((None,) + w1.shape[1:], lambda t, b: (t, 0, 0)),
                  pl.BlockSpec((None,) + pos.shape[1:], lambda t, b: (t, 0, 0)),
                  pl.BlockSpec((None,) + w2.shape[1:], lambda t, b: (t, 0, 0))],
        out_specs=[pl.BlockSpec((None, None, N_KV_HEADS, ngrp, 2 * HEAD_DIM), lambda t, b: (t, b, 0, 0, 0)),
                   pl.BlockSpec((None, None, N_KV_HEADS, HEAD_DIM, ngrp), lambda t, b: (t, b, 0, 0, 0))],
        out_shape=[jax.ShapeDtypeStruct((two, B, N_KV_HEADS, ngrp, 2 * HEAD_DIM), BF16),
                   jax.ShapeDtypeStruct((two, B, N_KV_HEADS, HEAD_DIM, ngrp), BF16)],
        compiler_params=_params("arbitrary", "arbitrary"),
        name="compress",
    )(tok, wbig, w1, pos, w2)


def _cmp_attn_kernel(qt_ref, kc_ref, vct_ref, gates_ref, oc_ref, bias_ref, imp_ref, score_ref, left_ref,
                     *, n_slc, topk):
    k_head = pl.program_id(1)
    i = pl.program_id(2)
    tq = qt_ref.shape[1]
    ncmp = kc_ref.shape[0]
    t0 = i * tq
    tile4 = lambda a: jnp.concatenate([a] * GQA_GROUP, axis=1)
    any_ok = tile4((t0 + lax.broadcasted_iota(jnp.int32, (1, tq), 1)) >= (CMP_LEN - 1))
    t0f = t0.astype(F32)
    q_aug = jnp.concatenate(
        [jnp.concatenate([qt_ref[g * HEAD_DIM:(g + 1) * HEAD_DIM, :] for g in range(GQA_GROUP)], axis=1),
         jnp.concatenate([_alibi_query_rows(_slope(k_head, g), t0f, HEAD_DIM, tq) for g in range(GQA_GROUP)],
                         axis=1).astype(BF16)], axis=0)

    def attend(nrows):
        t_col = t0 + lax.broadcasted_iota(jnp.int32, (nrows, tq), 1)
        cmp_end = CMP_STRIDE * lax.broadcasted_iota(jnp.int32, (nrows, tq), 0) + (CMP_LEN - 1)
        s = jnp.where(tile4(t_col >= cmp_end), _nn(kc_ref[0:nrows, :], q_aug), NEG)
        e = jnp.exp2(s - jnp.max(s, axis=0, keepdims=True))
        ones = jnp.where(lax.broadcasted_iota(jnp.int32, (16, nrows), 0) == 0, 1.0, 0.0).astype(BF16)
        o = _nn(jnp.concatenate([vct_ref[:, 0:nrows], ones], axis=0), e.astype(BF16))
        inv = jnp.where(any_ok, 1.0 / o[HEAD_DIM:HEAD_DIM + 1], 0.0)
        p = e * inv
        p_sum = sum(p[:, g * tq:(g + 1) * tq] for g in range(GQA_GROUP))
        out = o[:HEAD_DIM] * inv
        for g in range(GQA_GROUP):
            gate = _sigmoid(gates_ref[g:g + 1, :])
            oc_ref[g * HEAD_DIM:(g + 1) * HEAD_DIM, :] = out[:, g * tq:(g + 1) * tq] * gate
        j_i = lax.broadcasted_iota(jnp.int32, (MAX_SLC_BLOCKS, nrows), 0)
        n_i = lax.broadcasted_iota(jnp.int32, (MAX_SLC_BLOCKS, nrows), 1)
        overlap = ((CMP_STRIDE * n_i <= SLC_LEN * j_i + (SLC_LEN - 1))
                   & (CMP_STRIDE * n_i + (CMP_LEN - 1) >= SLC_LEN * j_i)).astype(BF16)
        p_hi = p_sum.astype(BF16)
        r1 = p_sum - p_hi.astype(F32)
        p_mid = r1.astype(BF16)
        p_lo = (r1 - p_mid.astype(F32)).astype(BF16)
        imp_ref[...] = _nn(overlap, p_hi) + _nn(overlap, p_mid) + _nn(overlap, p_lo)

    n_chunks = ncmp // CMP_ROW_CHUNK
    last_chunk = jnp.minimum(((t0 + tq - CMP_LEN) // CMP_STRIDE) // CMP_ROW_CHUNK, n_chunks - 1)
    for c in range(n_chunks):
        pl.when(last_chunk == c)(functools.partial(attend, (c + 1) * CMP_ROW_CHUNK))
    score = imp_ref[...]

    j_row = lax.broadcasted_iota(jnp.int32, (MAX_SLC_BLOCKS, tq), 0)
    t_blk = t0 + lax.broadcasted_iota(jnp.int32, (MAX_SLC_BLOCKS, tq), 1)
    cur = t_blk // SLC_LEN
    valid = (SLC_LEN * j_row <= t_blk) & (j_row < n_slc)
    forced = (j_row == 0) | (j_row == cur) | (j_row == cur - 1)
    cand = valid & jnp.logical_not(forced)
    n_forced = 1 + (cur[0:1] >= 1).astype(jnp.int32) + (cur[0:1] >= 2).astype(jnp.int32)
    remaining = jnp.broadcast_to(topk - n_forced, (8, tq))
    score = jnp.where(cand, score, -jnp.inf)
    n_groups = tq // PICK_LANES
    for lb in range(n_groups):
        score_ref[lb] = score[:, lb * PICK_LANES:(lb + 1) * PICK_LANES]
        left_ref[lb] = remaining[:, lb * PICK_LANES:(lb + 1) * PICK_LANES]
    j_f = lax.broadcasted_iota(jnp.int32, (MAX_SLC_BLOCKS, LANES), 0).astype(F32)

    def pick_rounds(lb, rounds, nrow):
        halves = [score_ref[lb, 0:nrow, h * LANES:(h + 1) * LANES] for h in range(PICK_LANES // LANES)]
        left = [left_ref[lb, 0:1, h * LANES:(h + 1) * LANES] for h in range(PICK_LANES // LANES)]
        j_n = j_f[0:nrow]
        for r in rounds:
            for h, sc in enumerate(halves):
                m = jnp.max(sc, axis=0, keepdims=True)
                first = jnp.min(jnp.where(sc == m, j_n, float(MAX_SLC_BLOCKS)), axis=0, keepdims=True)
                first = jnp.where(left[h] > r, first, -1.0)
                halves[h] = jnp.where(j_n == first, -jnp.inf, sc)
        score_ref[lb, 0:nrow, :] = jnp.concatenate(halves, axis=1)

    base_rounds = topk - 3
    blocks_started = (t0 + tq) // SLC_LEN
    row_buckets = [r for r in (32, 64) if r < MAX_SLC_BLOCKS] + [MAX_SLC_BLOCKS]
    for lo, nrow in zip([0] + row_buckets[:-1], row_buckets):
        in_bucket = (blocks_started > lo) if nrow == MAX_SLC_BLOCKS else ((blocks_started > lo) & (blocks_started <= nrow))

        @pl.when(in_bucket)
        def _pick(nrow=nrow):
            def group(lb, carry):
                pick_rounds(lb, range(base_rounds), nrow)
                return carry

            lax.fori_loop(0, n_groups, group, 0)

    @pl.when(t0 < 2 * SLC_LEN)
    def _early_queries():
        pick_rounds(0, range(base_rounds, topk - 1), row_buckets[0])

    picked = jnp.concatenate([score_ref[lb] for lb in range(n_groups)], axis=1) == -jnp.inf
    bias_ref[...] = jnp.where((valid & forced) | (cand & picked), 0.0, NEG).astype(BF16)


def _cmp_attn_call(qt, cmp_kv, cmp_kv_t, gates, *, n_slc, topk):
    B, _, S = qt.shape
    ncmp = cmp_kv.shape[3]
    tq = min(CMP_Q_TILE, S)
    assert topk >= 3 and tq >= 2 * SLC_LEN
    kern = functools.partial(_cmp_attn_kernel, n_slc=n_slc, topk=topk)
    return pl.pallas_call(
        kern,
        grid=(B, N_KV_HEADS, S // tq),
        in_specs=[pl.BlockSpec((None, GROUP_WIDTH, tq), lambda b, k, i: (b, k, i)),
                  pl.BlockSpec((None, None, None, ncmp, 2 * HEAD_DIM), lambda b, k, i: (0, b, k, 0, 0)),
                  pl.BlockSpec((None, None, None, HEAD_DIM, ncmp), lambda b, k, i: (1, b, k, 0, 0)),
                  pl.BlockSpec((None, None, GATE_ROWS, tq), lambda b, k, i: (b, k, 0, i))],
        out_specs=[pl.BlockSpec((None, GROUP_WIDTH, tq), lambda b, k, i: (b, k, i)),
                   pl.BlockSpec((None, None, MAX_SLC_BLOCKS, tq), lambda b, k, i: (b, k, 0, i))],
        out_shape=[jax.ShapeDtypeStruct((B, ATTN_WIDTH, S), F32),
                   jax.ShapeDtypeStruct((B, N_KV_HEADS, MAX_SLC_BLOCKS, S), BF16)],
        scratch_shapes=[pltpu.VMEM((MAX_SLC_BLOCKS, tq), F32),
                        pltpu.VMEM((tq // PICK_LANES, MAX_SLC_BLOCKS, PICK_LANES), F32),
                        pltpu.VMEM((tq // PICK_LANES, 8, PICK_LANES), jnp.int32)],
        compiler_params=_params("arbitrary", "arbitrary", "arbitrary"),
        name="cmp_attn",
    )(qt, cmp_kv, cmp_kv_t, gates)


V_ROWS = HEAD_DIM + 16


def _slc_attn_kernel(qt_ref, bias_ref, ks_ref, vt_ref, gates_ref, os_ref,
                     kaug_ref, vh_ref, qaug_ref, m_ref, acc_ref, sa_ref, sb_ref, ta_ref, tb_ref):
    k_head = pl.program_id(1)
    i = pl.program_id(2)
    tq = qt_ref.shape[1]
    n_tiles = kaug_ref.shape[0]
    t0 = i * tq

    @pl.when(i == 0)
    def _build_keys():
        def fill(c, carry):
            r0 = pl.multiple_of(c * tq, tq)
            kaug_ref[c, :, 0:HEAD_DIM] = _pick_head_lanes(ks_ref[pl.ds(r0, tq), :], k_head)
            return carry

        lax.fori_loop(0, n_tiles, fill, 0)
        for c in range(n_tiles):
            vh_ref[c, 0:HEAD_DIM, :] = _pick_head_rows(vt_ref[:, c * tq:(c + 1) * tq], k_head)

    @pl.when((i == 0) & (k_head == 0) & (pl.program_id(0) == 0))
    def _build_static():
        def fill(c, carry):
            r0 = pl.multiple_of(c * tq, tq)
            pos = r0 + lax.broadcasted_iota(jnp.int32, (tq, HEAD_DIM), 0)
            kaug_ref[c, :, HEAD_DIM:2 * HEAD_DIM] = _alibi_key_cols(pos).astype(BF16)
            blk = (r0 + lax.broadcasted_iota(jnp.int32, (tq, MAX_SLC_BLOCKS), 0)) // SLC_LEN
            hot = blk == lax.broadcasted_iota(jnp.int32, (tq, MAX_SLC_BLOCKS), 1)
            kaug_ref[c, :, 2 * HEAD_DIM:] = jnp.where(hot, 1.0, 0.0).astype(BF16)
            return carry

        lax.fori_loop(0, n_tiles, fill, 0)
        ones_row = lax.broadcasted_iota(jnp.int32, (V_ROWS - HEAD_DIM, tq), 0) == 0
        for c in range(n_tiles):
            vh_ref[c, HEAD_DIM:, :] = jnp.where(ones_row, 1.0, 0.0).astype(BF16)

    t0f = t0.astype(F32)
    bias = bias_ref[...]
    for g in range(GQA_GROUP):
        aux = _alibi_query_rows(_slope(k_head, g), t0f, HEAD_DIM, tq)
        qaug_ref[0:HEAD_DIM, g * tq:(g + 1) * tq] = qt_ref[g * HEAD_DIM:(g + 1) * HEAD_DIM, :]
        qaug_ref[HEAD_DIM:2 * HEAD_DIM, g * tq:(g + 1) * tq] = aux.astype(BF16)
        qaug_ref[2 * HEAD_DIM:, g * tq:(g + 1) * tq] = bias
    m_ref[...] = jnp.full(m_ref.shape, NEG, F32)
    acc_ref[...] = jnp.zeros(acc_ref.shape, F32)

    s_bufs = (sa_ref, sb_ref)

    tmax_bufs = (ta_ref, tb_ref)

    def scores_head(k_tile, g, slot, causal):
        cols = slice(g * tq, (g + 1) * tq)
        s = _nn(k_tile, qaug_ref[:, cols])
        if causal:
            key = lax.broadcasted_iota(jnp.int32, (tq, tq), 0)
            qry = lax.broadcasted_iota(jnp.int32, (tq, tq), 1)
            s = jnp.where(key <= qry, s, NEG)
        s_bufs[slot][:, cols] = s
        tmax_bufs[slot][:, cols] = jnp.max(s, axis=0, keepdims=True)

    def scores(j, slot, causal):
        k_tile = kaug_ref[j]
        for g in range(GQA_GROUP):
            scores_head(k_tile, g, slot, causal)

    def step(j, slot, prefetch):
        v_tile = vh_ref[j]
        k_next = kaug_ref[j + 1] if prefetch is not None else None
        for g in range(GQA_GROUP):
            cols = slice(g * tq, (g + 1) * tq)
            if prefetch is not None:
                scores_head(k_next, g, 1 - slot, prefetch == 'causal')
            m_old = m_ref[:, cols]
            m_new = jnp.maximum(m_old, tmax_bufs[slot][:, cols])
            alpha = jnp.exp2(m_old - m_new)
            p = jnp.exp2(s_bufs[slot][:, cols] - m_new).astype(BF16)
            acc_ref[:, cols] = alpha * acc_ref[:, cols] + _nn(v_tile, p)
            m_ref[:, cols] = m_new

    @pl.when(i == 0)
    def _first_is_causal():
        scores(0, 0, True)

    @pl.when(i > 0)
    def _first_is_plain():
        scores(0, 0, False)

    def pair(p, carry):
        step(2 * p, 0, 'plain')
        step(2 * p + 1, 1, 'plain')
        return carry

    n_pairs = jnp.maximum(i - 1, 0) // 2
    lax.fori_loop(0, n_pairs, pair, 0)

    @pl.when(i == 0)
    def _tail_single():
        step(0, 0, None)

    @pl.when(i % 2 == 1)
    def _tail_odd():
        step(i - 1, 0, 'causal')
        step(i, 1, None)

    @pl.when((i % 2 == 0) & (i > 0))
    def _tail_even():
        step(i - 2, 0, 'plain')
        step(i - 1, 1, 'causal')
        step(i, 0, None)

    o = acc_ref[0:HEAD_DIM, :] / acc_ref[HEAD_DIM:HEAD_DIM + 1, :]
    for g in range(GQA_GROUP):
        gate = _sigmoid(gates_ref[GQA_GROUP + g:GQA_GROUP + g + 1, :])
        os_ref[g * HEAD_DIM:(g + 1) * HEAD_DIM, :] = o[:, g * tq:(g + 1) * tq] * gate


def _slc_attn_call(qt, bias, kk, vt, gates):
    B, _, S = qt.shape
    tq = min(SLC_TILE, S)
    return pl.pallas_call(
        _slc_attn_kernel,
        grid=(B, N_KV_HEADS, S // tq),
        in_specs=[pl.BlockSpec((None, GROUP_WIDTH, tq), lambda b, k, i: (b, k, i)),
                  pl.BlockSpec((None, None, MAX_SLC_BLOCKS, tq), lambda b, k, i: (b, k, 0, i)),
                  pl.BlockSpec((None, S, KV_WIDTH), lambda b, k, i: (b, 0, 0)),
                  pl.BlockSpec((None, KV_WIDTH, S), lambda b, k, i: (b, 0, 0)),
                  pl.BlockSpec((None, None, GATE_ROWS, tq), lambda b, k, i: (b, k, 0, i))],
        out_specs=pl.BlockSpec((None, GROUP_WIDTH, tq), lambda b, k, i: (b, k, i)),
        out_shape=jax.ShapeDtypeStruct((B, ATTN_WIDTH, S), F32),
        scratch_shapes=[pltpu.VMEM((S // tq, tq, 2 * LANES), BF16),
                        pltpu.VMEM((S // tq, V_ROWS, tq), BF16),
                        pltpu.VMEM((2 * LANES, GQA_GROUP * tq), BF16),
                        pltpu.VMEM((1, GQA_GROUP * tq), F32),
                        pltpu.VMEM((V_ROWS, GQA_GROUP * tq), F32),
                        pltpu.VMEM((tq, GQA_GROUP * tq), F32),
                        pltpu.VMEM((tq, GQA_GROUP * tq), F32),
                        pltpu.VMEM((1, GQA_GROUP * tq), F32),
                        pltpu.VMEM((1, GQA_GROUP * tq), F32)],
        compiler_params=_params("arbitrary", "arbitrary", "arbitrary"),
        name="slc_attn",
    )(qt, bias, kk, vt, gates)


def _win_attn_kernel(qt_ref, k2_ref, k1_ref, k0_ref, v2_ref, v1_ref, v0_ref, gates_ref, ow_ref):
    k_head = pl.program_id(1)
    i = pl.program_id(2)
    tq = qt_ref.shape[1]
    t0f = (i * tq).astype(F32)
    qs = jnp.concatenate([qt_ref[g * HEAD_DIM:(g + 1) * HEAD_DIM, :] for g in range(GQA_GROUP)], axis=1)
    alibi = jnp.concatenate([_alibi_query_rows(_slope(k_head, g), t0f, KV_WIDTH, tq, row8=NEG)
                             for g in range(GQA_GROUP)], axis=1).astype(BF16)
    q2 = jnp.concatenate([jnp.where(k_head == k, qs, jnp.zeros_like(qs)) for k in range(N_KV_HEADS)] + [alibi],
                         axis=0)
    key = lax.broadcasted_iota(jnp.int32, (tq, tq), 0)
    qry = lax.broadcasted_iota(jnp.int32, (tq, tq), 1)
    keeps = (qry < key, None, key <= qry)
    ones = jnp.where(lax.broadcasted_iota(jnp.int32, (16, tq), 0) == 0, 1.0, 0.0).astype(BF16)
    lane = lax.broadcasted_iota(jnp.int32, (tq, KV_WIDTH), 1)
    cols_in_tile = _alibi_key_cols(lax.broadcasted_iota(jnp.int32, (tq, KV_WIDTH), 0))
    high_lanes = jnp.where((lane == 0) | (lane == 2), 1.0, 0.0)
    flag_lane = jnp.where(lane == 8, 1.0, 0.0)
    k_augs, v_tiles = [], []
    for back, k_ref, v_ref in zip((2, 1, 0), (k2_ref, k1_ref, k0_ref), (v2_ref, v1_ref, v0_ref)):
        tile_high = ((i - back) * (tq // LANES)).astype(F32)
        before_start = jnp.where(i < back, 1.0, 0.0)
        k_aux = cols_in_tile + high_lanes * tile_high + flag_lane * before_start
        k_augs.append(jnp.concatenate([k_ref[...], k_aux.astype(BF16)], axis=1))
        v_tiles.append(jnp.concatenate([v_ref[...], ones], axis=0))
    mask = lambda keep, s: s if keep is None else jnp.where(keep, s, NEG)
    scores = [[mask(keep, _nn(k_aug, q2[:, g * tq:(g + 1) * tq])) for k_aug, keep in zip(k_augs, keeps)]
              for g in range(GQA_GROUP)]
    for g in range(GQA_GROUP):
        m = functools.reduce(jnp.maximum, [jnp.max(s, axis=0, keepdims=True) for s in scores[g]])
        o2 = sum(_nn(v, jnp.exp2(s - m).astype(BF16)) for v, s in zip(v_tiles, scores[g]))
        o = _pick_head_rows(o2[:KV_WIDTH], k_head) / o2[KV_WIDTH:KV_WIDTH + 1]
        gate = _sigmoid(gates_ref[2 * GQA_GROUP + g:2 * GQA_GROUP + g + 1, :])
        ow_ref[g * HEAD_DIM:(g + 1) * HEAD_DIM, :] = o * gate


def _win_attn_call(qt, kk, vt, gates):
    B, _, S = qt.shape
    tq = min(Q_TILE, S)
    assert WINDOW == 2 * tq
    k_spec = lambda back: pl.BlockSpec((None, tq, KV_WIDTH), lambda b, k, i: (b, jnp.maximum(i - back, 0), 1))
    v_spec = lambda back: pl.BlockSpec((None, KV_WIDTH, tq), lambda b, k, i: (b, 1, jnp.maximum(i - back, 0)))
    return pl.pallas_call(
        _win_attn_kernel,
        grid=(B, N_KV_HEADS, S // tq),
        in_specs=[pl.BlockSpec((None, GROUP_WIDTH, tq), lambda b, k, i: (b, k, i)),
                  k_spec(2), k_spec(1), k_spec(0), v_spec(2), v_spec(1), v_spec(0),
                  pl.BlockSpec((None, None, GATE_ROWS, tq), lambda b, k, i: (b, k, 0, i))],
        out_specs=pl.BlockSpec((None, GROUP_WIDTH, tq), lambda b, k, i: (b, k, i)),
        out_shape=jax.ShapeDtypeStruct((B, ATTN_WIDTH, S), F32),
        compiler_params=_params("arbitrary", "arbitrary", "arbitrary"),
        name="win_attn",
    )(qt, kk, kk, kk, vt, vt, vt, gates)


def _group_rms(a, gain):
    c, tm = a.shape
    a3 = a.reshape(c // HEAD_DIM, HEAD_DIM, tm)
    ms = jnp.mean(a3 * a3, axis=1, keepdims=True)
    return (a3 * lax.rsqrt(ms + LN_EPS) * gain).reshape(c, tm)


def _out_proj_kernel(oc_ref, os_ref, ow_ref, mlp_ref, x_ref, g1_ref, og_ref, wo_ref, lng_ref, lnb_ref, o_ref, *, alpha):
    n_attn = ATTN_WIDTH // HEAD_DIM
    attn = _group_rms(oc_ref[...] + os_ref[...] + ow_ref[...], og_ref[:n_attn])
    mlp = _group_rms(mlp_ref[...], og_ref[n_attn:])
    y_t = jnp.concatenate([attn, mlp], axis=0).astype(BF16)
    y = _tn(y_t, wo_ref[...])
    z = alpha * x_ref[...] + (1.0 + g1_ref[...]) * y
    o_ref[...] = _layer_norm_rows(z, lng_ref[...], lnb_ref[...])


def _out_proj_call(oc, os_, ow, mlp, x, g1, out_g, w_o, ln_g, ln_b, *, alpha):
    B, S, D = x.shape
    tm = min(TOK_TILE, S)
    cm = lambda width: pl.BlockSpec((None, width, tm), lambda b, i: (b, 0, i))
    row = pl.BlockSpec((1, D), lambda b, i: (0, 0))
    return pl.pallas_call(
        functools.partial(_out_proj_kernel, alpha=alpha),
        grid=(B, S // tm),
        in_specs=[cm(ATTN_WIDTH), cm(ATTN_WIDTH), cm(ATTN_WIDTH), cm(MLP_WIDTH),
                  pl.BlockSpec((None, tm, D), lambda b, i: (b, i, 0)),
                  pl.BlockSpec((None, 1, D), lambda b, i: (b, 0, 0)),
                  pl.BlockSpec(out_g.shape, lambda b, i: (0, 0, 0)),
                  pl.BlockSpec(w_o.shape, lambda b, i: (0, 0)),
                  row, row],
        out_specs=pl.BlockSpec((None, tm, D), lambda b, i: (b, i, 0)),
        out_shape=jax.ShapeDtypeStruct((B, S, D), F32),
        compiler_params=_params("arbitrary", "arbitrary"),
        name="out_proj",
    )(oc, os_, ow, mlp, x, g1, out_g, w_o, ln_g, ln_b)


def _ffn_kernel(x_ref, sc_ref, sh_ref, g2_ref, w1_ref, w3_ref, w2_ref, lng_ref, lnb_ref, o_ref, *, alpha, splits):
    x = x_ref[...]
    h = (x * (1.0 + sc_ref[...]) + sh_ref[...]).astype(BF16)
    f = None
    for lo, hi in splits:
        a = _nn(h, w1_ref[:, lo:hi])
        b = _nn(h, w3_ref[:, lo:hi])
        part = _nn((a * _sigmoid(a) * b).astype(BF16), w2_ref[lo:hi, :])
        f = part if f is None else f + part
    z = alpha * x + (1.0 + g2_ref[...]) * f
    o_ref[...] = _layer_norm_rows(z, lng_ref[...], lnb_ref[...])


def _ffn_call(x, sc, sh, g2, w1, w3, w2, ln_g, ln_b, *, alpha):
    B, S, D = x.shape
    d_ff = w1.shape[1]
    tm = min(TOK_TILE, S)
    half = (d_ff // 2 + 255) // 256 * 256
    splits = ((0, half), (half, d_ff))
    mod = pl.BlockSpec((None, 1, D), lambda b, i: (b, 0, 0))
    row = pl.BlockSpec((1, D), lambda b, i: (0, 0))
    resident = lambda shape: pl.BlockSpec(shape, lambda b, i: (0, 0), pipeline_mode=pl.Buffered(1))
    return pl.pallas_call(
        functools.partial(_ffn_kernel, alpha=alpha, splits=splits),
        grid=(B, S // tm),
        in_specs=[pl.BlockSpec((None, tm, D), lambda b, i: (b, i, 0)), mod, mod, mod,
                  resident(w1.shape), resident(w3.shape), resident(w2.shape), row, row],
        out_specs=pl.BlockSpec((None, tm, D), lambda b, i: (b, i, 0)),
        out_shape=jax.ShapeDtypeStruct((B, S, D), F32),
        compiler_params=_params("arbitrary", "arbitrary"),
        name="ffn",
    )(x, sc, sh, g2, w1, w3, w2, ln_g, ln_b)


def _in_proj_weights(w_in):
    sizes = (ATTN_WIDTH,) + (KV_WIDTH,) * 6 + (N_BRANCH * N_ATTN_HEADS, MLP_WIDTH, MLP_WIDTH)
    offs = [0]
    for s in sizes:
        offs.append(offs[-1] + s)
    col = lambda n: w_in[:, offs[n]:offs[n + 1]]
    q, kc, vc, ksl, vsl, kwn, vwn, gt, u, v = (col(n) for n in range(10))
    w_a = jnp.concatenate([ksl, kwn, kc, vc], axis=1).astype(BF16)
    D = w_in.shape[0]
    gt = gt.reshape(D, N_KV_HEADS, GQA_GROUP, N_BRANCH).transpose(1, 3, 2, 0)
    gt = gt.reshape(N_KV_HEADS, N_BRANCH * GQA_GROUP, D)
    gt = jnp.pad(gt, ((0, 0), (0, GATE_ROWS - N_BRANCH * GQA_GROUP), (0, 0)))
    gt = gt.reshape(N_KV_HEADS * GATE_ROWS, D)
    w_t = jnp.concatenate([q.T, vsl.T, vwn.T, gt, u.T, v.T], axis=0).astype(BF16)
    return w_a, w_t


def _compress_weights(cmp_w1):
    two, _, hid = cmp_w1.shape
    w = cmp_w1.reshape(two, 2, CMP_STRIDE, HEAD_DIM, hid)
    eye = jnp.eye(N_KV_HEADS, dtype=cmp_w1.dtype)
    big = jnp.einsum('thpdc,kj->tpkdjhc', w, eye)
    return big.reshape(two, CMP_STRIDE * N_KV_HEADS * HEAD_DIM, N_KV_HEADS * 2 * hid).astype(BF16)


def _hybrid_layer(x, mod, w_in, cmp_pos, cmp_w1, cmp_w2, vn_g, vn_b, w_s, b_s, out_g, w_o, ln1_g, ln1_b,
                  w1, w3, w2, ln2_g, ln2_b, *, alpha):
    B, S, D = x.shape
    assert S % Q_TILE == 0 and S % TOK_TILE == 0 and S // SLC_LEN <= MAX_SLC_BLOCKS
    sh1, sc1, g1, sh2, sc2, g2 = (mod[:, None, n * D:(n + 1) * D] for n in range(6))
    w_a, w_t = _in_proj_weights(w_in)
    kk, kvc, qt, vt, gates, mlp = _in_proj_call(
        x, sc1, sh1, w_a, w_t,
        vn_g.reshape(N_MLP_GROUPS, MLP_GROUP_DIM, 1), vn_b.reshape(N_MLP_GROUPS, MLP_GROUP_DIM, 1),
        jnp.swapaxes(w_s, 1, 2), b_s)
    ngrp = S // CMP_STRIDE
    tok = kvc.reshape(2, B, ngrp, CMP_STRIDE * KV_WIDTH)
    pos = jnp.broadcast_to(cmp_pos.reshape(2, 1, CMP_LEN * HEAD_DIM), (2, 8, CMP_LEN * HEAD_DIM))
    cmp_kv, cmp_kv_t = _compress_call(tok, _compress_weights(cmp_w1), cmp_w1.astype(BF16), pos, cmp_w2.astype(BF16))
    gates = gates.reshape(B, N_KV_HEADS, GATE_ROWS, S)
    n_slc = S // SLC_LEN
    oc, bias = _cmp_attn_call(qt, cmp_kv, cmp_kv_t, gates, n_slc=n_slc, topk=min(SLC_TOPK, n_slc))
    os_ = _slc_attn_call(qt, bias, kk, vt, gates)
    ow = _win_attn_call(qt, kk, vt, gates)
    row = lambda a: a.reshape(1, D)
    x = _out_proj_call(oc, os_, ow, mlp, x, g1, out_g.reshape(-1, HEAD_DIM, 1), w_o.astype(BF16),
                       row(ln1_g), row(ln1_b), alpha=alpha)
    return _ffn_call(x, sc2, sh2, g2, w1.astype(BF16), w3.astype(BF16), w2.astype(BF16),
                     row(ln2_g), row(ln2_b), alpha=alpha)


def kernel(x, c, w_ada, b_ada, w_in, cmp_pos, cmp_w1, cmp_w2, vn_g, vn_b, w_s, b_s, out_g, w_o, ln1_g, ln1_b,
           w1, w3, w2, ln2_g, ln2_b):
    depth = w_ada.shape[0]
    alpha = (2.0 * depth) ** 0.25
    mod = _ada_call(c, w_ada, b_ada)
    for l in range(depth):
        x = _hybrid_layer(x, mod[l], w_in[l], cmp_pos[l], cmp_w1[l], cmp_w2[l], vn_g[l], vn_b[l], w_s[l], b_s[l],
                          out_g[l], w_o[l], ln1_g[l], ln1_b[l], w1[l], w3[l], w2[l], ln2_g[l], ln2_b[l], alpha=alpha)
    return x
```

```python
import functools
import math

import jax
import jax.numpy as jnp
from jax import lax
from jax.experimental import pallas as pl
from jax.experimental.pallas import tpu as pltpu

F32 = jnp.float32
BF16 = jnp.bfloat16

HEAD_DIM = 64
N_KV_HEADS = 2
GQA_GROUP = 4
N_ATTN_HEADS = N_KV_HEADS * GQA_GROUP
ATTN_WIDTH = N_ATTN_HEADS * HEAD_DIM
KV_WIDTH = N_KV_HEADS * HEAD_DIM
GROUP_WIDTH = GQA_GROUP * HEAD_DIM
N_BRANCH = 3
MLP_GROUP_DIM = 64
N_MLP_GROUPS = 8
MLP_WIDTH = N_MLP_GROUPS * MLP_GROUP_DIM
CMP_LEN = 32
CMP_STRIDE = 16
CMP_HIDDEN = 256
SLC_LEN = 64
SLC_TOPK = 16
WINDOW = 512
CHUNK = 128
LN_EPS = 1e-5
NEG = -1e30
FORCED_SCORE = 1e6

LOG2E = math.log2(math.e)
Q_SCALE = HEAD_DIM ** -0.5 * LOG2E

LANES = 128
MAX_SLC_BLOCKS = LANES
TOK_TILE = 512
Q_TILE = 256
SLC_TILE = 512
SLC_UNROLL = 4
CMP_Q_TILE = 512
PICK_LANES = 256
CMP_ROW_CHUNK = 128
VMEM_LIMIT = 56 * 1024 * 1024
GATE_ROWS = 16

ROW_V = MLP_WIDTH
ROW_Q = ROW_V + MLP_WIDTH
ROW_VSLC = ROW_Q + ATTN_WIDTH
ROW_VWIN = ROW_VSLC + KV_WIDTH
ROW_GATE = ROW_VWIN + KV_WIDTH
ROWS_T = ROW_GATE + N_KV_HEADS * GATE_ROWS
COLS_A = 4 * KV_WIDTH


def _nt(a, b):
    return lax.dot_general(a, b, (((1,), (1,)), ((), ())), preferred_element_type=F32)


def _tn(a, b):
    return lax.dot_general(a, b, (((0,), (0,)), ((), ())), preferred_element_type=F32)


def _nn(a, b):
    return jnp.dot(a, b, preferred_element_type=F32)


def _gelu(x):
    return x * (0.5 * (1.0 + jnp.tanh(math.sqrt(2.0 / math.pi) * (x + 0.044715 * (x * x * x)))))


def _sigmoid(x):
    return 1.0 / (1.0 + jnp.exp(-x))


def _layer_norm_rows(z, g, b):
    mu = jnp.mean(z, axis=-1, keepdims=True)
    zc = z - mu
    var = jnp.mean(zc * zc, axis=-1, keepdims=True)
    return zc * lax.rsqrt(var + LN_EPS) * g + b


def _slope(kv_head, g):
    slope = lambda h: LOG2E * 2.0 ** (-8.0 * (h + 1) / N_ATTN_HEADS)
    val = jnp.asarray(slope(g), F32)
    for k in range(1, N_KV_HEADS):
        val = jnp.where(kv_head == k, jnp.asarray(slope(k * GQA_GROUP + g), F32), val)
    return val


def _pick_head_lanes(x, kv_head):
    out = x[:, :HEAD_DIM]
    for k in range(1, N_KV_HEADS):
        out = jnp.where(kv_head == k, x[:, k * HEAD_DIM:(k + 1) * HEAD_DIM], out)
    return out


def _pick_head_rows(x, kv_head):
    out = x[:HEAD_DIM]
    for k in range(1, N_KV_HEADS):
        out = jnp.where(kv_head == k, x[k * HEAD_DIM:(k + 1) * HEAD_DIM], out)
    return out


def _bf16_part(x):
    return x.astype(BF16).astype(F32)


def _alibi_key_cols(pos):
    lane = lax.broadcasted_iota(jnp.int32, pos.shape, 1)
    cols = jnp.where(lane % 2 == 0, pos // LANES, pos % LANES)
    return jnp.where(lane < 4, cols, jnp.where(lane < 8, 1, 0)).astype(F32)


def _alibi_query_rows(c, t0f, rows, tq, row8=0.0):
    shape = (8, tq)
    row = lax.broadcasted_iota(jnp.int32, shape, 0)
    c = jnp.full(shape, c, F32)
    out = jnp.zeros(shape, F32)
    for n, part in enumerate((_bf16_part(c), _bf16_part(c - _bf16_part(c)))):
        shift = part * t0f
        vals = {2 * n: LANES * part, 2 * n + 1: part, 4 + 2 * n: -_bf16_part(shift), 5 + 2 * n: -(shift - _bf16_part(shift))}
        for r, val in vals.items():
            out = jnp.where(row == r, val, out)
    tail = jnp.where(lax.broadcasted_iota(jnp.int32, (rows - 8, tq), 0) == 0, row8, 0.0)
    return jnp.concatenate([out, tail], axis=0)


def _params(*sem):
    return pltpu.CompilerParams(dimension_semantics=sem, vmem_limit_bytes=VMEM_LIMIT)


def _ada_kernel(c_ref, w_ref, b_ref, o_ref):
    c = c_ref[...]
    c_act = (c * _sigmoid(c)).astype(BF16)
    o_ref[0] = _nn(c_act, w_ref[0].astype(BF16)) + b_ref[0]


def _ada_call(c, w_ada, b_ada):
    L, D, D6 = w_ada.shape
    B = c.shape[0]
    return pl.pallas_call(
        _ada_kernel,
        grid=(L, D6 // D),
        in_specs=[pl.BlockSpec((B, D), lambda l, n: (0, 0)),
                  pl.BlockSpec((1, D, D), lambda l, n: (l, 0, n)),
                  pl.BlockSpec((1, 1, D), lambda l, n: (l, 0, n))],
        out_specs=pl.BlockSpec((1, B, D), lambda l, n: (l, 0, n)),
        out_shape=jax.ShapeDtypeStruct((L, B, D6), F32),
        compiler_params=_params("arbitrary", "arbitrary"),
        name="ada_mod",
    )(c, w_ada, b_ada.reshape(L, 1, D6))


def _in_proj_kernel(x_ref, sc_ref, sh_ref, wa_ref, wt_ref, vng_ref, vnb_ref, wst_ref, bs_ref,
                    kk_ref, kvc_ref, qt_ref, vt_ref, gates_ref, mlp_ref):
    tm = x_ref.shape[0]
    h = (x_ref[...] * (1.0 + sc_ref[...]) + sh_ref[...]).astype(BF16)
    uv = _nt(wt_ref[0:ROW_Q, :], h)
    t = _nt(wt_ref[ROW_Q:, :], h)
    qt_ref[...] = (t[:ATTN_WIDTH] * Q_SCALE).astype(BF16)
    vt_ref[...] = t[ROW_VSLC - ROW_Q:ROW_GATE - ROW_Q].astype(BF16)
    gates_ref[...] = t[ROW_GATE - ROW_Q:]
    a = _nn(h, wa_ref[...])
    kk_ref[...] = a[:, :2 * KV_WIDTH].astype(BF16)
    kvc_ref[0] = a[:, 2 * KV_WIDTH:3 * KV_WIDTH].astype(BF16)
    kvc_ref[1] = a[:, 3 * KV_WIDTH:].astype(BF16)

    u = _gelu(uv[:ROW_V])
    v = _gelu(uv[ROW_V:]).reshape(N_MLP_GROUPS, MLP_GROUP_DIM, tm)
    mu = jnp.mean(v, axis=1, keepdims=True)
    vc = v - mu
    var = jnp.mean(vc * vc, axis=1, keepdims=True)
    vn = vc * lax.rsqrt(var + LN_EPS) * vng_ref[...] + vnb_ref[...]
    n_chunks = tm // CHUNK
    s_idx = lax.broadcasted_iota(jnp.int32, (CHUNK, CHUNK), 0)
    t_idx = lax.broadcasted_iota(jnp.int32, (CHUNK, CHUNK), 1)
    for g in range(N_MLP_GROUPS):
        vg = vn[g].astype(BF16)
        stack = jnp.concatenate([vg[:, c * CHUNK:(c + 1) * CHUNK] for c in range(n_chunks)], axis=0)
        w_t = jnp.where(s_idx <= t_idx, wst_ref[g], 0.0).astype(BF16)
        sv = _nn(stack, w_t)
        bias = bs_ref[g:g + 1, :]
        for c in range(n_chunks):
            mlp_ref[g * MLP_GROUP_DIM:(g + 1) * MLP_GROUP_DIM, c * CHUNK:(c + 1) * CHUNK] = (
                u[g * MLP_GROUP_DIM:(g + 1) * MLP_GROUP_DIM, c * CHUNK:(c + 1) * CHUNK]
                * (sv[c * MLP_GROUP_DIM:(c + 1) * MLP_GROUP_DIM] + bias)).astype(BF16)


def _in_proj_call(x, sc, sh, w_a, w_t, vn_g, vn_b, w_st, b_s):
    B, S, D = x.shape
    tm = min(TOK_TILE, S)
    const2 = lambda b, i: (0, 0)
    const3 = lambda b, i: (0, 0, 0)
    cm = lambda rows: pl.BlockSpec((None, rows, tm), lambda b, i: (b, 0, i))
    return pl.pallas_call(
        _in_proj_kernel,
        grid=(B, S // tm),
        in_specs=[pl.BlockSpec((None, tm, D), lambda b, i: (b, i, 0)),
                  pl.BlockSpec((None, 1, D), lambda b, i: (b, 0, 0)),
                  pl.BlockSpec((None, 1, D), lambda b, i: (b, 0, 0)),
                  pl.BlockSpec(w_a.shape, const2),
                  pl.BlockSpec(w_t.shape, const2),
                  pl.BlockSpec(vn_g.shape, const3),
                  pl.BlockSpec(vn_b.shape, const3),
                  pl.BlockSpec(w_st.shape, const3),
                  pl.BlockSpec(b_s.shape, const2)],
        out_specs=[pl.BlockSpec((None, tm, 2 * KV_WIDTH), lambda b, i: (b, i, 0)),
                   pl.BlockSpec((2, None, tm, KV_WIDTH), lambda b, i: (0, b, i, 0)),
                   cm(ATTN_WIDTH), cm(2 * KV_WIDTH), cm(N_KV_HEADS * GATE_ROWS), cm(MLP_WIDTH)],
        out_shape=[jax.ShapeDtypeStruct((B, S, 2 * KV_WIDTH), BF16),
                   jax.ShapeDtypeStruct((2, B, S, KV_WIDTH), BF16),
                   jax.ShapeDtypeStruct((B, ATTN_WIDTH, S), BF16),
                   jax.ShapeDtypeStruct((B, 2 * KV_WIDTH, S), BF16),
                   jax.ShapeDtypeStruct((B, N_KV_HEADS * GATE_ROWS, S), F32),
                   jax.ShapeDtypeStruct((B, MLP_WIDTH, S), BF16)],
        compiler_params=_params("arbitrary", "arbitrary"),
        name="in_proj",
    )(x, sc, sh, w_a, w_t, vn_g, vn_b, w_st, b_s)


def _compress_kernel(tok_ref, wbig_ref, w1_ref, pos_ref, w2_ref, out_ref, out_t_ref):
    ab = _nn(tok_ref[...], wbig_ref[...])
    bias = _nn(pos_ref[...].astype(BF16), w1_ref[...])[0:1]
    ngrp = ab.shape[0]
    for h in range(N_KV_HEADS):
        base = h * 2 * CMP_HIDDEN
        first = ab[:, base:base + CMP_HIDDEN]
        second = ab[:, base + CMP_HIDDEN:base + 2 * CMP_HIDDEN]
        hid = _gelu(first + pltpu.roll(second, ngrp - 1, 0) + bias)
        c = _nn(hid.astype(BF16), w2_ref[...])
        block_end = CMP_STRIDE * lax.broadcasted_iota(jnp.int32, (ngrp, HEAD_DIM), 0) + (CMP_LEN - 1)
        out_ref[h, :, 0:HEAD_DIM] = c.astype(BF16)
        out_ref[h, :, HEAD_DIM:] = _alibi_key_cols(block_end).astype(BF16)
        out_t_ref[h] = c.T.astype(BF16)


def _compress_call(tok, wbig, w1, pos, w2):
    two, B, ngrp, flat = tok.shape
    return pl.pallas_call(
        _compress_kernel,
        grid=(two, B),
        in_specs=[pl.BlockSpec((None, None, ngrp, flat), lambda t, b: (t, b, 0, 0)),
                  pl.BlockSpec((None,) + wbig.shape[1:], lambda t, b: (t, 0, 0)),
                  pl.BlockSpec((None,) + w1.shape[1:], lambda t, b: (t, 0, 0)),
                  pl.BlockSpec((None,) + pos.shape[1:], lambda t, b: (t, 0, 0)),
                  pl.BlockSpec((None,) + w2.shape[1:], lambda t, b: (t, 0, 0))],
        out_specs=[pl.BlockSpec((None, None, N_KV_HEADS, ngrp, 2 * HEAD_DIM), lambda t, b: (t, b, 0, 0, 0)),
                   pl.BlockSpec((None, None, N_KV_HEADS, HEAD_DIM, ngrp), lambda t, b: (t, b, 0, 0, 0))],
        out_shape=[jax.ShapeDtypeStruct((two, B, N_KV_HEADS, ngrp, 2 * HEAD_DIM), BF16),
                   jax.ShapeDtypeStruct((two, B, N_KV_HEADS, HEAD_DIM, ngrp), BF16)],
        compiler_params=_params("arbitrary", "arbitrary"),
        name="compress",
    )(tok, wbig, w1, pos, w2)


def _cmp_attn_kernel(qt_ref, kc_ref, vct_ref, gates_ref, oc_ref, bias_ref, imp_ref, score_ref, left_ref,
                     *, n_slc, topk):
    k_head = pl.program_id(1)
    i = pl.program_id(2)
    tq = qt_ref.shape[1]
    ncmp = kc_ref.shape[0]
    t0 = i * tq
    tile4 = lambda a: jnp.concatenate([a] * GQA_GROUP, axis=1)
    any_ok = tile4((t0 + lax.broadcasted_iota(jnp.int32, (1, tq), 1)) >= (CMP_LEN - 1))
    t0f = t0.astype(F32)
    q_aug = jnp.concatenate(
        [jnp.concatenate([qt_ref[g * HEAD_DIM:(g + 1) * HEAD_DIM, :] for g in range(GQA_GROUP)], axis=1),
         jnp.concatenate([_alibi_query_rows(_slope(k_head, g), t0f, HEAD_DIM, tq) for g in range(GQA_GROUP)],
                         axis=1).astype(BF16)], axis=0)

    def attend(nrows):
        t_col = t0 + lax.broadcasted_iota(jnp.int32, (nrows, tq), 1)
        cmp_end = CMP_STRIDE * lax.broadcasted_iota(jnp.int32, (nrows, tq), 0) + (CMP_LEN - 1)
        s = jnp.where(tile4(t_col >= cmp_end), _nn(kc_ref[0:nrows, :], q_aug), NEG)
        e = jnp.exp2(s - jnp.max(s, axis=0, keepdims=True))
        ones = jnp.where(lax.broadcasted_iota(jnp.int32, (16, nrows), 0) == 0, 1.0, 0.0).astype(BF16)
        o = _nn(jnp.concatenate([vct_ref[:, 0:nrows], ones], axis=0), e.astype(BF16))
        inv = jnp.where(any_ok, 1.0 / o[HEAD_DIM:HEAD_DIM + 1], 0.0)
        p = e * inv
        p_sum = sum(p[:, g * tq:(g + 1) * tq] for g in range(GQA_GROUP))
        out = o[:HEAD_DIM] * inv
        for g in range(GQA_GROUP):
            gate = _sigmoid(gates_ref[g:g + 1, :])
            oc_ref[g * HEAD_DIM:(g + 1) * HEAD_DIM, :] = out[:, g * tq:(g + 1) * tq] * gate
        j_i = lax.broadcasted_iota(jnp.int32, (MAX_SLC_BLOCKS, nrows), 0)
        n_i = lax.broadcasted_iota(jnp.int32, (MAX_SLC_BLOCKS, nrows), 1)
        overlap = ((CMP_STRIDE * n_i <= SLC_LEN * j_i + (SLC_LEN - 1))
                   & (CMP_STRIDE * n_i + (CMP_LEN - 1) >= SLC_LEN * j_i)).astype(BF16)
        p_hi = p_sum.astype(BF16)
        r1 = p_sum - p_hi.astype(F32)
        p_mid = r1.astype(BF16)
        p_lo = (r1 - p_mid.astype(F32)).astype(BF16)
        imp_ref[...] = _nn(overlap, p_hi) + _nn(overlap, p_mid) + _nn(overlap, p_lo)

    n_chunks = ncmp // CMP_ROW_CHUNK
    last_chunk = jnp.minimum(((t0 + tq - CMP_LEN) // CMP_STRIDE) // CMP_ROW_CHUNK, n_chunks - 1)
    for c in range(n_chunks):
        pl.when(last_chunk == c)(functools.partial(attend, (c + 1) * CMP_ROW_CHUNK))
    score = imp_ref[...]

    j_row = lax.broadcasted_iota(jnp.int32, (MAX_SLC_BLOCKS, tq), 0)
    t_blk = t0 + lax.broadcasted_iota(jnp.int32, (MAX_SLC_BLOCKS, tq), 1)
    cur = t_blk // SLC_LEN
    valid = (SLC_LEN * j_row <= t_blk) & (j_row < n_slc)
    forced = (j_row == 0) | (j_row == cur) | (j_row == cur - 1)
    cand = valid & jnp.logical_not(forced)
    n_forced = 1 + (cur[0:1] >= 1).astype(jnp.int32) + (cur[0:1] >= 2).astype(jnp.int32)
    remaining = jnp.broadcast_to(topk - n_forced, (8, tq))
    score = jnp.where(cand, score, -jnp.inf)
    n_groups = tq // PICK_LANES
    for lb in range(n_groups):
        score_ref[lb] = score[:, lb * PICK_LANES:(lb + 1) * PICK_LANES]
        left_ref[lb] = remaining[:, lb * PICK_LANES:(lb + 1) * PICK_LANES]
    j_f = lax.broadcasted_iota(jnp.int32, (MAX_SLC_BLOCKS, LANES), 0).astype(F32)

    def pick_rounds(lb, rounds, nrow):
        halves = [score_ref[lb, 0:nrow, h * LANES:(h + 1) * LANES] for h in range(PICK_LANES // LANES)]
        left = [left_ref[lb, 0:1, h * LANES:(h + 1) * LANES] for h in range(PICK_LANES // LANES)]
        j_n = j_f[0:nrow]
        for r in rounds:
            for h, sc in enumerate(halves):
                m = jnp.max(sc, axis=0, keepdims=True)
                first = jnp.min(jnp.where(sc == m, j_n, float(MAX_SLC_BLOCKS)), axis=0, keepdims=True)
                first = jnp.where(left[h] > r, first, -1.0)
                halves[h] = jnp.where(j_n == first, -jnp.inf, sc)
        score_ref[lb, 0:nrow, :] = jnp.concatenate(halves, axis=1)

    base_rounds = topk - 3
    blocks_started = (t0 + tq) // SLC_LEN
    row_buckets = [r for r in (32, 64) if r < MAX_SLC_BLOCKS] + [MAX_SLC_BLOCKS]
    for lo, nrow in zip([0] + row_buckets[:-1], row_buckets):
        in_bucket = (blocks_started > lo) if nrow == MAX_SLC_BLOCKS else ((blocks_started > lo) & (blocks_started <= nrow))

        @pl.when(in_bucket)
        def _pick(nrow=nrow):
            def group(lb, carry):
                pick_rounds(lb, range(base_rounds), nrow)
                return carry

            lax.fori_loop(0, n_groups, group, 0)

    @pl.when(t0 < 2 * SLC_LEN)
    def _early_queries():
        pick_rounds(0, range(base_rounds, topk - 1), row_buckets[0])

    picked = jnp.concatenate([score_ref[lb] for lb in range(n_groups)], axis=1) == -jnp.inf
    bias_ref[...] = jnp.where((valid & forced) | (cand & picked), 0.0, NEG).astype(BF16)


def _cmp_attn_call(qt, cmp_kv, cmp_kv_t, gates, *, n_slc, topk):
    B, _, S = qt.shape
    ncmp = cmp_kv.shape[3]
    tq = min(CMP_Q_TILE, S)
    assert topk >= 3 and tq >= 2 * SLC_LEN
    kern = functools.partial(_cmp_attn_kernel, n_slc=n_slc, topk=topk)
    return pl.pallas_call(
        kern,
        grid=(B, N_KV_HEADS, S // tq),
        in_specs=[pl.BlockSpec((None, GROUP_WIDTH, tq), lambda b, k, i: (b, k, i)),
                  pl.BlockSpec((None, None, None, ncmp, 2 * HEAD_DIM), lambda b, k, i: (0, b, k, 0, 0)),
                  pl.BlockSpec((None, None, None, HEAD_DIM, ncmp), lambda b, k, i: (1, b, k, 0, 0)),
                  pl.BlockSpec((None, None, GATE_ROWS, tq), lambda b, k, i: (b, k, 0, i))],
        out_specs=[pl.BlockSpec((None, GROUP_WIDTH, tq), lambda b, k, i: (b, k, i)),
                   pl.BlockSpec((None, None, MAX_SLC_BLOCKS, tq), lambda b, k, i: (b, k, 0, i))],
        out_shape=[jax.ShapeDtypeStruct((B, ATTN_WIDTH, S), F32),
                   jax.ShapeDtypeStruct((B, N_KV_HEADS, MAX_SLC_BLOCKS, S), BF16)],
        scratch_shapes=[pltpu.VMEM((MAX_SLC_BLOCKS, tq), F32),
                        pltpu.VMEM((tq // PICK_LANES, MAX_SLC_BLOCKS, PICK_LANES), F32),
                        pltpu.VMEM((tq // PICK_LANES, 8, PICK_LANES), jnp.int32)],
        compiler_params=_params("arbitrary", "arbitrary", "arbitrary"),
        name="cmp_attn",
    )(qt, cmp_kv, cmp_kv_t, gates)


V_ROWS = HEAD_DIM + 16


def _slc_attn_kernel(qt_ref, bias_ref, ks_ref, vt_ref, gates_ref, other_ref, attn_ref,
                     kaug_ref, vh_ref, qaug_ref, m_ref, acc_ref, sa_ref, sb_ref, ta_ref, tb_ref):
    k_head = pl.program_id(1)
    i = pl.program_id(2)
    tq = qt_ref.shape[1]
    n_tiles = kaug_ref.shape[0]
    t0 = i * tq

    @pl.when(i == 0)
    def _build_keys():
        def fill(c, carry):
            r0 = pl.multiple_of(c * tq, tq)
            kaug_ref[c, :, 0:HEAD_DIM] = _pick_head_lanes(ks_ref[pl.ds(r0, tq), :], k_head)
            return carry

        lax.fori_loop(0, n_tiles, fill, 0)
        for c in range(n_tiles):
            vh_ref[c, 0:HEAD_DIM, :] = _pick_head_rows(vt_ref[:, c * tq:(c + 1) * tq], k_head)

    @pl.when((i == 0) & (k_head == 0) & (pl.program_id(0) == 0))
    def _build_static():
        def fill(c, carry):
            r0 = pl.multiple_of(c * tq, tq)
            pos = r0 + lax.broadcasted_iota(jnp.int32, (tq, HEAD_DIM), 0)
            kaug_ref[c, :, HEAD_DIM:2 * HEAD_DIM] = _alibi_key_cols(pos).astype(BF16)
            blk = (r0 + lax.broadcasted_iota(jnp.int32, (tq, MAX_SLC_BLOCKS), 0)) // SLC_LEN
            hot = blk == lax.broadcasted_iota(jnp.int32, (tq, MAX_SLC_BLOCKS), 1)
            kaug_ref[c, :, 2 * HEAD_DIM:] = jnp.where(hot, 1.0, 0.0).astype(BF16)
            return carry

        lax.fori_loop(0, n_tiles, fill, 0)
        ones_row = lax.broadcasted_iota(jnp.int32, (V_ROWS - HEAD_DIM, tq), 0) == 0
        for c in range(n_tiles):
            vh_ref[c, HEAD_DIM:, :] = jnp.where(ones_row, 1.0, 0.0).astype(BF16)

    t0f = t0.astype(F32)
    bias = bias_ref[...]
    for g in range(GQA_GROUP):
        aux = _alibi_query_rows(_slope(k_head, g), t0f, HEAD_DIM, tq)
        qaug_ref[0:HEAD_DIM, g * tq:(g + 1) * tq] = qt_ref[g * HEAD_DIM:(g + 1) * HEAD_DIM, :]
        qaug_ref[HEAD_DIM:2 * HEAD_DIM, g * tq:(g + 1) * tq] = aux.astype(BF16)
        qaug_ref[2 * HEAD_DIM:, g * tq:(g + 1) * tq] = bias
    m_ref[...] = jnp.full(m_ref.shape, NEG, F32)
    acc_ref[...] = jnp.zeros(acc_ref.shape, F32)

    s_bufs = (sa_ref, sb_ref)

    tmax_bufs = (ta_ref, tb_ref)

    def scores_head(k_tile, g, slot, causal):
        cols = slice(g * tq, (g + 1) * tq)
        s = _nn(k_tile, qaug_ref[:, cols])
        if causal:
            key = lax.broadcasted_iota(jnp.int32, (tq, tq), 0)
            qry = lax.broadcasted_iota(jnp.int32, (tq, tq), 1)
            s = jnp.where(key <= qry, s, NEG)
        s_bufs[slot][:, cols] = s
        tmax_bufs[slot][:, cols] = jnp.max(s, axis=0, keepdims=True)

    def scores(j, slot, causal):
        k_tile = kaug_ref[j]
        for g in range(GQA_GROUP):
            scores_head(k_tile, g, slot, causal)

    def step(j, slot, prefetch):
        v_tile = vh_ref[j]
        k_next = kaug_ref[j + 1] if prefetch is not None else None
        for g in range(GQA_GROUP):
            cols = slice(g * tq, (g + 1) * tq)
            if prefetch is not None:
                scores_head(k_next, g, 1 - slot, prefetch == 'causal')
            m_old = m_ref[:, cols]
            m_new = jnp.maximum(m_old, tmax_bufs[slot][:, cols])
            alpha = jnp.exp2(m_old - m_new)
            p = jnp.exp2(s_bufs[slot][:, cols] - m_new).astype(BF16)
            acc_ref[:, cols] = alpha * acc_ref[:, cols] + _nn(v_tile, p)
            m_ref[:, cols] = m_new

    @pl.when(i == 0)
    def _first_is_causal():
        scores(0, 0, True)

    @pl.when(i > 0)
    def _first_is_plain():
        scores(0, 0, False)

    def trip(p, carry):
        for u in range(SLC_UNROLL):
            step(SLC_UNROLL * p + u, u % 2, 'plain')
        return carry

    n_trips = jnp.maximum(i - 1, 0) // SLC_UNROLL
    lax.fori_loop(0, n_trips, trip, 0)
    first_left = SLC_UNROLL * n_trips
    for n_left in range(1, SLC_UNROLL + 2):
        @pl.when(i - first_left + 1 == n_left)
        def _tail(n_left=n_left):
            for u in range(n_left):
                step(first_left + u, u % 2, (None, 'causal')[u == n_left - 2] if u >= n_left - 2 else 'plain')

    o = acc_ref[0:HEAD_DIM, :] / acc_ref[HEAD_DIM:HEAD_DIM + 1, :]
    for g in range(GQA_GROUP):
        gate = _sigmoid(gates_ref[GQA_GROUP + g:GQA_GROUP + g + 1, :])
        rows = slice(g * HEAD_DIM, (g + 1) * HEAD_DIM)
        attn_ref[rows, :] = (other_ref[rows, :] + o[:, g * tq:(g + 1) * tq] * gate).astype(BF16)


def _slc_attn_call(qt, bias, kk, vt, gates, other):
    B, _, S = qt.shape
    tq = min(SLC_TILE, S)
    return pl.pallas_call(
        _slc_attn_kernel,
        grid=(B, N_KV_HEADS, S // tq),
        in_specs=[pl.BlockSpec((None, GROUP_WIDTH, tq), lambda b, k, i: (b, k, i)),
                  pl.BlockSpec((None, None, MAX_SLC_BLOCKS, tq), lambda b, k, i: (b, k, 0, i)),
                  pl.BlockSpec((None, S, KV_WIDTH), lambda b, k, i: (b, 0, 0)),
                  pl.BlockSpec((None, KV_WIDTH, S), lambda b, k, i: (b, 0, 0)),
                  pl.BlockSpec((None, None, GATE_ROWS, tq), lambda b, k, i: (b, k, 0, i)),
                  pl.BlockSpec((None, GROUP_WIDTH, tq), lambda b, k, i: (b, k, i))],
        out_specs=pl.BlockSpec((None, GROUP_WIDTH, tq), lambda b, k, i: (b, k, i)),
        out_shape=jax.ShapeDtypeStruct((B, ATTN_WIDTH, S), BF16),
        scratch_shapes=[pltpu.VMEM((S // tq, tq, 2 * LANES), BF16),
                        pltpu.VMEM((S // tq, V_ROWS, tq), BF16),
                        pltpu.VMEM((2 * LANES, GQA_GROUP * tq), BF16),
                        pltpu.VMEM((1, GQA_GROUP * tq), F32),
                        pltpu.VMEM((V_ROWS, GQA_GROUP * tq), F32),
                        pltpu.VMEM((tq, GQA_GROUP * tq), F32),
                        pltpu.VMEM((tq, GQA_GROUP * tq), F32),
                        pltpu.VMEM((1, GQA_GROUP * tq), F32),
                        pltpu.VMEM((1, GQA_GROUP * tq), F32)],
        compiler_params=_params("arbitrary", "arbitrary", "arbitrary"),
        name="slc_attn",
    )(qt, bias, kk, vt, gates, other)


def _win_attn_kernel(qt_ref, kprev_ref, kcur_ref, vprev_ref, vcur_ref, gates_ref, other_ref, ow_ref):
    k_head = pl.program_id(1)
    i = pl.program_id(2)
    tq = qt_ref.shape[1] // 2
    key = lax.broadcasted_iota(jnp.int32, (tq, tq), 0)
    qry = lax.broadcasted_iota(jnp.int32, (tq, tq), 1)
    keeps = (qry < key, None, key <= qry)
    ones = jnp.where(lax.broadcasted_iota(jnp.int32, (16, tq), 0) == 0, 1.0, 0.0).astype(BF16)
    lane = lax.broadcasted_iota(jnp.int32, (tq, KV_WIDTH), 1)
    cols_in_tile = _alibi_key_cols(lax.broadcasted_iota(jnp.int32, (tq, KV_WIDTH), 0))
    high_lanes = jnp.where((lane == 0) | (lane == 2), 1.0, 0.0)
    flag_lane = jnp.where(lane == 8, 1.0, 0.0)
    k_augs, v_tiles = [], []
    for n, (k_ref, v_ref) in enumerate(((kprev_ref, vprev_ref), (kprev_ref, vprev_ref),
                                        (kcur_ref, vcur_ref), (kcur_ref, vcur_ref))):
        half = slice((n % 2) * tq, (n % 2 + 1) * tq)
        tile = 2 * i - 2 + n
        k_aux = (cols_in_tile + high_lanes * (tile * (tq // LANES)).astype(F32)
                 + flag_lane * jnp.where(tile < 0, 1.0, 0.0))
        k_augs.append(jnp.concatenate([k_ref[half, :], k_aux.astype(BF16)], axis=1))
        v_tiles.append(jnp.concatenate([v_ref[:, half], ones], axis=0))

    q2s = []
    for h in range(2):
        cols = slice(h * tq, (h + 1) * tq)
        t0f = ((2 * i + h) * tq).astype(F32)
        qs = jnp.concatenate([qt_ref[g * HEAD_DIM:(g + 1) * HEAD_DIM, cols] for g in range(GQA_GROUP)], axis=1)
        alibi = jnp.concatenate([_alibi_query_rows(_slope(k_head, g), t0f, KV_WIDTH, tq, row8=NEG)
                                 for g in range(GQA_GROUP)], axis=1).astype(BF16)
        q2s.append(jnp.concatenate([jnp.where(k_head == k, qs, jnp.zeros_like(qs)) for k in range(N_KV_HEADS)]
                                   + [alibi], axis=0))
    mask = lambda keep, s: s if keep is None else jnp.where(keep, s, NEG)
    scores = [[[mask(keep, _nn(k_aug, q2s[h][:, g * tq:(g + 1) * tq]))
                for k_aug, keep in zip(k_augs[h:h + 3], keeps)]
               for g in range(GQA_GROUP)] for h in range(2)]
    for h in range(2):
        for g in range(GQA_GROUP):
            m = functools.reduce(jnp.maximum, [jnp.max(s, axis=0, keepdims=True) for s in scores[h][g]])
            o2 = sum(_nn(v, jnp.exp2(s - m).astype(BF16)) for v, s in zip(v_tiles[h:h + 3], scores[h][g]))
            o = _pick_head_rows(o2[:KV_WIDTH], k_head) / o2[KV_WIDTH:KV_WIDTH + 1]
            gate = _sigmoid(gates_ref[2 * GQA_GROUP + g:2 * GQA_GROUP + g + 1, h * tq:(h + 1) * tq])
            rows, cols = slice(g * HEAD_DIM, (g + 1) * HEAD_DIM), slice(h * tq, (h + 1) * tq)
            ow_ref[rows, cols] = other_ref[rows, cols] + o * gate


def _win_attn_call(qt, kk, vt, gates, other):
    B, _, S = qt.shape
    tq = min(Q_TILE, S)
    assert WINDOW == 2 * tq and S % (2 * tq) == 0
    k_spec = lambda back: pl.BlockSpec((None, 2 * tq, KV_WIDTH), lambda b, k, i: (b, jnp.maximum(i - back, 0), 1))
    v_spec = lambda back: pl.BlockSpec((None, KV_WIDTH, 2 * tq), lambda b, k, i: (b, 1, jnp.maximum(i - back, 0)))
    return pl.pallas_call(
        _win_attn_kernel,
        grid=(B, N_KV_HEADS, S // (2 * tq)),
        in_specs=[pl.BlockSpec((None, GROUP_WIDTH, 2 * tq), lambda b, k, i: (b, k, i)),
                  k_spec(1), k_spec(0), v_spec(1), v_spec(0),
                  pl.BlockSpec((None, None, GATE_ROWS, 2 * tq), lambda b, k, i: (b, k, 0, i)),
                  pl.BlockSpec((None, GROUP_WIDTH, 2 * tq), lambda b, k, i: (b, k, i))],
        out_specs=pl.BlockSpec((None, GROUP_WIDTH, 2 * tq), lambda b, k, i: (b, k, i)),
        out_shape=jax.ShapeDtypeStruct((B, ATTN_WIDTH, S), F32),
        compiler_params=_params("arbitrary", "arbitrary", "arbitrary"),
        name="win_attn",
    )(qt, kk, kk, vt, vt, gates, other)


def _group_rms(a, gain):
    c, tm = a.shape
    a3 = a.reshape(c // HEAD_DIM, HEAD_DIM, tm)
    ms = jnp.mean(a3 * a3, axis=1, keepdims=True)
    return (a3 * lax.rsqrt(ms + LN_EPS) * gain).reshape(c, tm)


def _out_proj_kernel(attn_ref, mlp_ref, x_ref, g1_ref, og_ref, wo_ref, lng_ref, lnb_ref, o_ref, *, alpha):
    n_attn = ATTN_WIDTH // HEAD_DIM
    attn = _group_rms(attn_ref[...].astype(F32), og_ref[:n_attn])
    mlp = _group_rms(mlp_ref[...].astype(F32), og_ref[n_attn:])
    y_t = jnp.concatenate([attn, mlp], axis=0).astype(BF16)
    y = _tn(y_t, wo_ref[...])
    z = alpha * x_ref[...] + (1.0 + g1_ref[...]) * y
    o_ref[...] = _layer_norm_rows(z, lng_ref[...], lnb_ref[...])


def _out_proj_call(attn, mlp, x, g1, out_g, w_o, ln_g, ln_b, *, alpha):
    B, S, D = x.shape
    tm = min(TOK_TILE, S)
    cm = lambda width: pl.BlockSpec((None, width, tm), lambda b, i: (b, 0, i))
    row = pl.BlockSpec((1, D), lambda b, i: (0, 0))
    return pl.pallas_call(
        functools.partial(_out_proj_kernel, alpha=alpha),
        grid=(B, S // tm),
        in_specs=[cm(ATTN_WIDTH), cm(MLP_WIDTH),
                  pl.BlockSpec((None, tm, D), lambda b, i: (b, i, 0)),
                  pl.BlockSpec((None, 1, D), lambda b, i: (b, 0, 0)),
                  pl.BlockSpec(out_g.shape, lambda b, i: (0, 0, 0)),
                  pl.BlockSpec(w_o.shape, lambda b, i: (0, 0)),
                  row, row],
        out_specs=pl.BlockSpec((None, tm, D), lambda b, i: (b, i, 0)),
        out_shape=jax.ShapeDtypeStruct((B, S, D), F32),
        compiler_params=_params("arbitrary", "arbitrary"),
        name="out_proj",
    )(attn, mlp, x, g1, out_g, w_o, ln_g, ln_b)


def _ffn_kernel(x_ref, sc_ref, sh_ref, g2_ref, w1_ref, w3_ref, w2_ref, lng_ref, lnb_ref, o_ref, *, alpha, splits):
    x = x_ref[...]
    h = (x * (1.0 + sc_ref[...]) + sh_ref[...]).astype(BF16)
    f = None
    for lo, hi in splits:
        a = _nn(h, w1_ref[:, lo:hi])
        b = _nn(h, w3_ref[:, lo:hi])
        part = _nn((a * _sigmoid(a) * b).astype(BF16), w2_ref[lo:hi, :])
        f = part if f is None else f + part
    z = alpha * x + (1.0 + g2_ref[...]) * f
    o_ref[...] = _layer_norm_rows(z, lng_ref[...], lnb_ref[...])


def _ffn_call(x, sc, sh, g2, w1, w3, w2, ln_g, ln_b, *, alpha):
    B, S, D = x.shape
    d_ff = w1.shape[1]
    tm = min(TOK_TILE, S)
    half = (d_ff // 2 + 255) // 256 * 256
    splits = ((0, half), (half, d_ff))
    mod = pl.BlockSpec((None, 1, D), lambda b, i: (b, 0, 0))
    row = pl.BlockSpec((1, D), lambda b, i: (0, 0))
    resident = lambda shape: pl.BlockSpec(shape, lambda b, i: (0, 0), pipeline_mode=pl.Buffered(1))
    return pl.pallas_call(
        functools.partial(_ffn_kernel, alpha=alpha, splits=splits),
        grid=(B, S // tm),
        in_specs=[pl.BlockSpec((None, tm, D), lambda b, i: (b, i, 0)), mod, mod, mod,
                  resident(w1.shape), resident(w3.shape), resident(w2.shape), row, row],
        out_specs=pl.BlockSpec((None, tm, D), lambda b, i: (b, i, 0)),
        out_shape=jax.ShapeDtypeStruct((B, S, D), F32),
        compiler_params=_params("arbitrary", "arbitrary"),
        name="ffn",
    )(x, sc, sh, g2, w1, w3, w2, ln_g, ln_b)


def _in_proj_weights(w_in):
    sizes = (ATTN_WIDTH,) + (KV_WIDTH,) * 6 + (N_BRANCH * N_ATTN_HEADS, MLP_WIDTH, MLP_WIDTH)
    offs = [0]
    for s in sizes:
        offs.append(offs[-1] + s)
    col = lambda n: w_in[:, offs[n]:offs[n + 1]]
    q, kc, vc, ksl, vsl, kwn, vwn, gt, u, v = (col(n) for n in range(10))
    w_a = jnp.concatenate([ksl, kwn, kc, vc], axis=1).astype(BF16)
    D = w_in.shape[0]
    gt = gt.reshape(D, N_KV_HEADS, GQA_GROUP, N_BRANCH).transpose(1, 3, 2, 0)
    gt = gt.reshape(N_KV_HEADS, N_BRANCH * GQA_GROUP, D)
    gt = jnp.pad(gt, ((0, 0), (0, GATE_ROWS - N_BRANCH * GQA_GROUP), (0, 0)))
    gt = gt.reshape(N_KV_HEADS * GATE_ROWS, D)
    w_t = jnp.concatenate([u.T, v.T, q.T, vsl.T, vwn.T, gt], axis=0).astype(BF16)
    return w_a, w_t


def _compress_weights(cmp_w1):
    two, _, hid = cmp_w1.shape
    w = cmp_w1.reshape(two, 2, CMP_STRIDE, HEAD_DIM, hid)
    eye = jnp.eye(N_KV_HEADS, dtype=cmp_w1.dtype)
    big = jnp.einsum('thpdc,kj->tpkdjhc', w, eye)
    return big.reshape(two, CMP_STRIDE * N_KV_HEADS * HEAD_DIM, N_KV_HEADS * 2 * hid).astype(BF16)


def _hybrid_layer(x, mod, w_in, cmp_pos, cmp_w1, cmp_w2, vn_g, vn_b, w_s, b_s, out_g, w_o, ln1_g, ln1_b,
                  w1, w3, w2, ln2_g, ln2_b, *, alpha):
    B, S, D = x.shape
    assert S % Q_TILE == 0 and S % TOK_TILE == 0 and S // SLC_LEN <= MAX_SLC_BLOCKS
    sh1, sc1, g1, sh2, sc2, g2 = (mod[:, None, n * D:(n + 1) * D] for n in range(6))
    w_a, w_t = _in_proj_weights(w_in)
    kk, kvc, qt, vt, gates, mlp = _in_proj_call(
        x, sc1, sh1, w_a, w_t,
        vn_g.reshape(N_MLP_GROUPS, MLP_GROUP_DIM, 1), vn_b.reshape(N_MLP_GROUPS, MLP_GROUP_DIM, 1),
        jnp.swapaxes(w_s, 1, 2), b_s)
    ngrp = S // CMP_STRIDE
    tok = kvc.reshape(2, B, ngrp, CMP_STRIDE * KV_WIDTH)
    pos = jnp.broadcast_to(cmp_pos.reshape(2, 1, CMP_LEN * HEAD_DIM), (2, 8, CMP_LEN * HEAD_DIM))
    cmp_kv, cmp_kv_t = _compress_call(tok, _compress_weights(cmp_w1), cmp_w1.astype(BF16), pos, cmp_w2.astype(BF16))
    gates = gates.reshape(B, N_KV_HEADS, GATE_ROWS, S)
    n_slc = S // SLC_LEN
    oc, bias = _cmp_attn_call(qt, cmp_kv, cmp_kv_t, gates, n_slc=n_slc, topk=min(SLC_TOPK, n_slc))
    ocw = _win_attn_call(qt, kk, vt, gates, oc)
    attn = _slc_attn_call(qt, bias, kk, vt, gates, ocw)
    row = lambda a: a.reshape(1, D)
    x = _out_proj_call(attn, mlp, x, g1, out_g.reshape(-1, HEAD_DIM, 1), w_o.astype(BF16),
                       row(ln1_g), row(ln1_b), alpha=alpha)
    return _ffn_call(x, sc2, sh2, g2, w1.astype(BF16), w3.astype(BF16), w2.astype(BF16),
                     row(ln2_g), row(ln2_b), alpha=alpha)


def kernel(x, c, w_ada, b_ada, w_in, cmp_pos, cmp_w1, cmp_w2, vn_g, vn_b, w_s, b_s, out_g, w_o, ln1_g, ln1_b,
           w1, w3, w2, ln2_g, ln2_b):
    depth = w_ada.shape[0]
    alpha = (2.0 * depth) ** 0.25
    mod = _ada_call(c, w_ada, b_ada)
    for l in range(depth):
        x = _hybrid_layer(x, mod[l], w_in[l], cmp_pos[l], cmp_w1[l], cmp_w2[l], vn_g[l], vn_b[l], w_s[l], b_s[l],
                          out_g[l], w_o[l], ln1_g[l], ln1_b[l], w1[l], w3[l], w2[l], ln2_g[l], ln2_b[l], alpha=alpha)
    return x
```

```python
import functools
import math

import jax
import jax.numpy as jnp
from jax import lax
from jax.experimental import pallas as pl
from jax.experimental.pallas import tpu as pltpu

F32 = jnp.float32
BF16 = jnp.bfloat16

HEAD_DIM = 64
N_KV_HEADS = 2
GQA_GROUP = 4
N_ATTN_HEADS = N_KV_HEADS * GQA_GROUP
ATTN_WIDTH = N_ATTN_HEADS * HEAD_DIM
KV_WIDTH = N_KV_HEADS * HEAD_DIM
GROUP_WIDTH = GQA_GROUP * HEAD_DIM
N_BRANCH = 3
MLP_GROUP_DIM = 64
N_MLP_GROUPS = 8
MLP_WIDTH = N_MLP_GROUPS * MLP_GROUP_DIM
CMP_LEN = 32
CMP_STRIDE = 16
CMP_HIDDEN = 256
SLC_LEN = 64
SLC_TOPK = 16
WINDOW = 512
CHUNK = 128
LN_EPS = 1e-5
NEG = -1e30
FORCED_SCORE = 1e6

LOG2E = math.log2(math.e)
Q_SCALE = HEAD_DIM ** -0.5 * LOG2E

LANES = 128
MAX_SLC_BLOCKS = LANES
TOK_TILE = 512
Q_TILE = 256
SLC_TILE = 512
SLC_UNROLL = 4
CMP_Q_TILE = 512
PICK_LANES = 256
CMP_ROW_CHUNK = 128
VMEM_LIMIT = 56 * 1024 * 1024
GATE_ROWS = 16

ROW_V = MLP_WIDTH
ROW_Q = ROW_V + MLP_WIDTH
ROW_VSLC = ROW_Q + ATTN_WIDTH
ROW_VWIN = ROW_VSLC + KV_WIDTH
ROW_GATE = ROW_VWIN + KV_WIDTH
ROWS_T = ROW_GATE + N_KV_HEADS * GATE_ROWS
COLS_A = 4 * KV_WIDTH


def _nt(a, b):
    return lax.dot_general(a, b, (((1,), (1,)), ((), ())), preferred_element_type=F32)


def _tn(a, b):
    return lax.dot_general(a, b, (((0,), (0,)), ((), ())), preferred_element_type=F32)


def _nn(a, b):
    return jnp.dot(a, b, preferred_element_type=F32)


def _gelu(x):
    return x * (0.5 * (1.0 + jnp.tanh(math.sqrt(2.0 / math.pi) * (x + 0.044715 * (x * x * x)))))


def _sigmoid(x):
    return 1.0 / (1.0 + jnp.exp(-x))


def _layer_norm_rows(z, g, b):
    mu = jnp.mean(z, axis=-1, keepdims=True)
    zc = z - mu
    var = jnp.mean(zc * zc, axis=-1, keepdims=True)
    return zc * lax.rsqrt(var + LN_EPS) * g + b


def _slope(kv_head, g):
    slope = lambda h: LOG2E * 2.0 ** (-8.0 * (h + 1) / N_ATTN_HEADS)
    val = jnp.asarray(slope(g), F32)
    for k in range(1, N_KV_HEADS):
        val = jnp.where(kv_head == k, jnp.asarray(slope(k * GQA_GROUP + g), F32), val)
    return val


def _pick_head_lanes(x, kv_head):
    out = x[:, :HEAD_DIM]
    for k in range(1, N_KV_HEADS):
        out = jnp.where(kv_head == k, x[:, k * HEAD_DIM:(k + 1) * HEAD_DIM], out)
    return out


def _pick_head_rows(x, kv_head):
    out = x[:HEAD_DIM]
    for k in range(1, N_KV_HEADS):
        out = jnp.where(kv_head == k, x[k * HEAD_DIM:(k + 1) * HEAD_DIM], out)
    return out


def _bf16_part(x):
    return x.astype(BF16).astype(F32)


def _alibi_key_cols(pos):
    lane = lax.broadcasted_iota(jnp.int32, pos.shape, 1)
    cols = jnp.where(lane % 2 == 0, pos // LANES, pos % LANES)
    return jnp.where(lane < 4, cols, jnp.where(lane < 8, 1, 0)).astype(F32)


def _alibi_query_rows(c, t0f, rows, tq, row8=0.0):
    shape = (8, tq)
    row = lax.broadcasted_iota(jnp.int32, shape, 0)
    c = jnp.full(shape, c, F32)
    out = jnp.zeros(shape, F32)
    for n, part in enumerate((_bf16_part(c), _bf16_part(c - _bf16_part(c)))):
        shift = part * t0f
        vals = {2 * n: LANES * part, 2 * n + 1: part, 4 + 2 * n: -_bf16_part(shift), 5 + 2 * n: -(shift - _bf16_part(shift))}
        for r, val in vals.items():
            out = jnp.where(row == r, val, out)
    tail = jnp.where(lax.broadcasted_iota(jnp.int32, (rows - 8, tq), 0) == 0, row8, 0.0)
    return jnp.concatenate([out, tail], axis=0)


def _params(*sem):
    return pltpu.CompilerParams(dimension_semantics=sem, vmem_limit_bytes=VMEM_LIMIT)


def _ada_kernel(c_ref, w_ref, b_ref, o_ref):
    c = c_ref[...]
    c_act = (c * _sigmoid(c)).astype(BF16)
    o_ref[0] = _nn(c_act, w_ref[0].astype(BF16)) + b_ref[0]


def _ada_call(c, w_ada, b_ada):
    L, D, D6 = w_ada.shape
    B = c.shape[0]
    return pl.pallas_call(
        _ada_kernel,
        grid=(L, D6 // D),
        in_specs=[pl.BlockSpec((B, D), lambda l, n: (0, 0)),
                  pl.BlockSpec((1, D, D), lambda l, n: (l, 0, n)),
                  pl.BlockSpec((1, 1, D), lambda l, n: (l, 0, n))],
        out_specs=pl.BlockSpec((1, B, D), lambda l, n: (l, 0, n)),
        out_shape=jax.ShapeDtypeStruct((L, B, D6), F32),
        compiler_params=_params("arbitrary", "arbitrary"),
        name="ada_mod",
    )(c, w_ada, b_ada.reshape(L, 1, D6))


def _in_proj_kernel(x_ref, sc_ref, sh_ref, wa_ref, wt_ref, vng_ref, vnb_ref, wst_ref, bs_ref,
                    kk_ref, kvc_ref, qt_ref, vt_ref, gates_ref, mlp_ref):
    tm = x_ref.shape[0]
    h = (x_ref[...] * (1.0 + sc_ref[...]) + sh_ref[...]).astype(BF16)
    uv = _nt(wt_ref[0:ROW_Q, :], h)
    t = _nt(wt_ref[ROW_Q:, :], h)
    qt_ref[...] = (t[:ATTN_WIDTH] * Q_SCALE).astype(BF16)
    vt_ref[...] = t[ROW_VSLC - ROW_Q:ROW_GATE - ROW_Q].astype(BF16)
    gates_ref[...] = t[ROW_GATE - ROW_Q:]
    a = _nn(h, wa_ref[...])
    kk_ref[...] = a[:, :2 * KV_WIDTH].astype(BF16)
    kvc_ref[0] = a[:, 2 * KV_WIDTH:3 * KV_WIDTH].astype(BF16)
    kvc_ref[1] = a[:, 3 * KV_WIDTH:].astype(BF16)

    u = _gelu(uv[:ROW_V])
    v = _gelu(uv[ROW_V:]).reshape(N_MLP_GROUPS, MLP_GROUP_DIM, tm)
    mu = jnp.mean(v, axis=1, keepdims=True)
    vc = v - mu
    var = jnp.mean(vc * vc, axis=1, keepdims=True)
    vn = vc * lax.rsqrt(var + LN_EPS) * vng_ref[...] + vnb_ref[...]
    n_chunks = tm // CHUNK
    s_idx = lax.broadcasted_iota(jnp.int32, (CHUNK, CHUNK), 0)
    t_idx = lax.broadcasted_iota(jnp.int32, (CHUNK, CHUNK), 1)
    for g in range(N_MLP_GROUPS):
        vg = vn[g].astype(BF16)
        stack = jnp.concatenate([vg[:, c * CHUNK:(c + 1) * CHUNK] for c in range(n_chunks)], axis=0)
        w_t = jnp.where(s_idx <= t_idx, wst_ref[g], 0.0).astype(BF16)
        sv = _nn(stack, w_t)
        bias = bs_ref[g:g + 1, :]
        for c in range(n_chunks):
            mlp_ref[g * MLP_GROUP_DIM:(g + 1) * MLP_GROUP_DIM, c * CHUNK:(c + 1) * CHUNK] = (
                u[g * MLP_GROUP_DIM:(g + 1) * MLP_GROUP_DIM, c * CHUNK:(c + 1) * CHUNK]
                * (sv[c * MLP_GROUP_DIM:(c + 1) * MLP_GROUP_DIM] + bias)).astype(BF16)


def _in_proj_call(x, sc, sh, w_a, w_t, vn_g, vn_b, w_st, b_s):
    B, S, D = x.shape
    tm = min(TOK_TILE, S)
    const2 = lambda b, i: (0, 0)
    const3 = lambda b, i: (0, 0, 0)
    cm = lambda rows: pl.BlockSpec((None, rows, tm), lambda b, i: (b, 0, i))
    return pl.pallas_call(
        _in_proj_kernel,
        grid=(B, S // tm),
        in_specs=[pl.BlockSpec((None, tm, D), lambda b, i: (b, i, 0)),
                  pl.BlockSpec((None, 1, D), lambda b, i: (b, 0, 0)),
                  pl.BlockSpec((None, 1, D), lambda b, i: (b, 0, 0)),
                  pl.BlockSpec(w_a.shape, const2),
                  pl.BlockSpec(w_t.shape, const2),
                  pl.BlockSpec(vn_g.shape, const3),
                  pl.BlockSpec(vn_b.shape, const3),
                  pl.BlockSpec(w_st.shape, const3),
                  pl.BlockSpec(b_s.shape, const2)],
        out_specs=[pl.BlockSpec((None, tm, 2 * KV_WIDTH), lambda b, i: (b, i, 0)),
                   pl.BlockSpec((2, None, tm, KV_WIDTH), lambda b, i: (0, b, i, 0)),
                   cm(ATTN_WIDTH), cm(2 * KV_WIDTH), cm(N_KV_HEADS * GATE_ROWS), cm(MLP_WIDTH)],
        out_shape=[jax.ShapeDtypeStruct((B, S, 2 * KV_WIDTH), BF16),
                   jax.ShapeDtypeStruct((2, B, S, KV_WIDTH), BF16),
                   jax.ShapeDtypeStruct((B, ATTN_WIDTH, S), BF16),
                   jax.ShapeDtypeStruct((B, 2 * KV_WIDTH, S), BF16),
                   jax.ShapeDtypeStruct((B, N_KV_HEADS * GATE_ROWS, S), F32),
                   jax.ShapeDtypeStruct((B, MLP_WIDTH, S), BF16)],
        compiler_params=_params("arbitrary", "arbitrary"),
        name="in_proj",
    )(x, sc, sh, w_a, w_t, vn_g, vn_b, w_st, b_s)


def _compress_kernel(tok_ref, wbig_ref, w1_ref, pos_ref, w2_ref, out_ref, out_t_ref):
    ab = _nn(tok_ref[...], wbig_ref[...])
    bias = _nn(pos_ref[...].astype(BF16), w1_ref[...])[0:1]
    ngrp = ab.shape[0]
    for h in range(N_KV_HEADS):
        base = h * 2 * CMP_HIDDEN
        first = ab[:, base:base + CMP_HIDDEN]
        second = ab[:, base + CMP_HIDDEN:base + 2 * CMP_HIDDEN]
        hid = _gelu(first + pltpu.roll(second, ngrp - 1, 0) + bias)
        c = _nn(hid.astype(BF16), w2_ref[...])
        block_end = CMP_STRIDE * lax.broadcasted_iota(jnp.int32, (ngrp, HEAD_DIM), 0) + (CMP_LEN - 1)
        out_ref[h, :, 0:HEAD_DIM] = c.astype(BF16)
        out_ref[h, :, HEAD_DIM:] = _alibi_key_cols(block_end).astype(BF16)
        out_t_ref[h] = c.T.astype(BF16)


def _compress_call(tok, wbig, w1, pos, w2):
    two, B, ngrp, flat = tok.shape
    return pl.pallas_call(
        _compress_kernel,
        grid=(two, B),
        in_specs=[pl.BlockSpec((None, None, ngrp, flat), lambda t, b: (t, b, 0, 0)),
                  pl.BlockSpec((None,) + wbig.shape[1:], lambda t, b: (t, 0, 0)),
                  pl.BlockSpec((None,) + w1.shape[1:], lambda t, b: (t, 0, 0)),
                  pl.BlockSpec((None,) + pos.shape[1:], lambda t, b: (t, 0, 0)),
                  pl.BlockSpec((None,) + w2.shape[1:], lambda t, b: (t, 0, 0))],
        out_specs=[pl.BlockSpec((None, None, N_KV_HEADS, ngrp, 2 * HEAD_DIM), lambda t, b: (t, b, 0, 0, 0)),
                   pl.BlockSpec((None, None, N_KV_HEADS, HEAD_DIM, ngrp), lambda t, b: (t, b, 0, 0, 0))],
        out_shape=[jax.ShapeDtypeStruct((two, B, N_KV_HEADS, ngrp, 2 * HEAD_DIM), BF16),
                   jax.ShapeDtypeStruct((two, B, N_KV_HEADS, HEAD_DIM, ngrp), BF16)],
        compiler_params=_params("arbitrary", "arbitrary"),
        name="compress",
    )(tok, wbig, w1, pos, w2)


def _cmp_attn_kernel(qt_ref, kc_ref, vct_ref, gates_ref, oc_ref, bias_ref, imp_ref, score_ref, left_ref,
                     *, n_slc, topk):
    k_head = pl.program_id(1)
    i = pl.program_id(2)
    tq = qt_ref.shape[1]
    ncmp = kc_ref.shape[0]
    t0 = i * tq
    any_ok = (t0 + lax.broadcasted_iota(jnp.int32, (1, tq), 1)) >= (CMP_LEN - 1)
    t0f = t0.astype(F32)
    q_aug = jnp.concatenate(
        [jnp.concatenate([qt_ref[g * HEAD_DIM:(g + 1) * HEAD_DIM, :] for g in range(GQA_GROUP)], axis=1),
         jnp.concatenate([_alibi_query_rows(_slope(k_head, g), t0f, HEAD_DIM, tq) for g in range(GQA_GROUP)],
                         axis=1).astype(BF16)], axis=0)

    def attend(nrows):
        t_col = t0 + lax.broadcasted_iota(jnp.int32, (nrows, tq), 1)
        cmp_end = CMP_STRIDE * lax.broadcasted_iota(jnp.int32, (nrows, tq), 0) + (CMP_LEN - 1)
        ok = t_col >= cmp_end
        kc = kc_ref[0:nrows, :]
        ones = jnp.where(lax.broadcasted_iota(jnp.int32, (16, nrows), 0) == 0, 1.0, 0.0).astype(BF16)
        v_aug = jnp.concatenate([vct_ref[:, 0:nrows], ones], axis=0)
        s_heads = [jnp.where(ok, _nn(kc, q_aug[:, g * tq:(g + 1) * tq]), NEG) for g in range(GQA_GROUP)]
        p_sum = None
        for g, s in enumerate(s_heads):
            e = jnp.exp2(s - jnp.max(s, axis=0, keepdims=True))
            o = _nn(v_aug, e.astype(BF16))
            inv = jnp.where(any_ok, 1.0 / o[HEAD_DIM:HEAD_DIM + 1], 0.0)
            p = e * inv
            p_sum = p if p_sum is None else p_sum + p
            gate = _sigmoid(gates_ref[g:g + 1, :])
            oc_ref[g * HEAD_DIM:(g + 1) * HEAD_DIM, :] = o[:HEAD_DIM] * (inv * gate)
        j_i = lax.broadcasted_iota(jnp.int32, (MAX_SLC_BLOCKS, nrows), 0)
        n_i = lax.broadcasted_iota(jnp.int32, (MAX_SLC_BLOCKS, nrows), 1)
        overlap = ((CMP_STRIDE * n_i <= SLC_LEN * j_i + (SLC_LEN - 1))
                   & (CMP_STRIDE * n_i + (CMP_LEN - 1) >= SLC_LEN * j_i)).astype(BF16)
        p_hi = p_sum.astype(BF16)
        r1 = p_sum - p_hi.astype(F32)
        p_mid = r1.astype(BF16)
        p_lo = (r1 - p_mid.astype(F32)).astype(BF16)
        imp_ref[...] = _nn(overlap, p_hi) + _nn(overlap, p_mid) + _nn(overlap, p_lo)

    n_chunks = ncmp // CMP_ROW_CHUNK
    last_chunk = jnp.minimum(((t0 + tq - CMP_LEN) // CMP_STRIDE) // CMP_ROW_CHUNK, n_chunks - 1)
    for c in range(n_chunks):
        pl.when(last_chunk == c)(functools.partial(attend, (c + 1) * CMP_ROW_CHUNK))
    score = imp_ref[...]

    j_row = lax.broadcasted_iota(jnp.int32, (MAX_SLC_BLOCKS, tq), 0)
    t_blk = t0 + lax.broadcasted_iota(jnp.int32, (MAX_SLC_BLOCKS, tq), 1)
    cur = t_blk // SLC_LEN
    valid = (SLC_LEN * j_row <= t_blk) & (j_row < n_slc)
    forced = (j_row == 0) | (j_row == cur) | (j_row == cur - 1)
    cand = valid & jnp.logical_not(forced)
    n_forced = 1 + (cur[0:1] >= 1).astype(jnp.int32) + (cur[0:1] >= 2).astype(jnp.int32)
    remaining = jnp.broadcast_to(topk - n_forced, (8, tq))
    score = jnp.where(cand, score, -jnp.inf)
    n_groups = tq // PICK_LANES
    for lb in range(n_groups):
        score_ref[lb] = score[:, lb * PICK_LANES:(lb + 1) * PICK_LANES]
        left_ref[lb] = remaining[:, lb * PICK_LANES:(lb + 1) * PICK_LANES]
    j_f = lax.broadcasted_iota(jnp.int32, (MAX_SLC_BLOCKS, LANES), 0).astype(F32)

    def pick_rounds(lb, rounds, nrow):
        halves = [score_ref[lb, 0:nrow, h * LANES:(h + 1) * LANES] for h in range(PICK_LANES // LANES)]
        left = [left_ref[lb, 0:1, h * LANES:(h + 1) * LANES] for h in range(PICK_LANES // LANES)]
        j_n = j_f[0:nrow]
        for r in rounds:
            for h, sc in enumerate(halves):
                m = jnp.max(sc, axis=0, keepdims=True)
                first = jnp.min(jnp.where(sc == m, j_n, float(MAX_SLC_BLOCKS)), axis=0, keepdims=True)
                first = jnp.where(left[h] > r, first, -1.0)
                halves[h] = jnp.where(j_n == first, -jnp.inf, sc)
        score_ref[lb, 0:nrow, :] = jnp.concatenate(halves, axis=1)

    base_rounds = topk - 3
    blocks_started = (t0 + tq) // SLC_LEN
    row_buckets = [r for r in (32, 64) if r < MAX_SLC_BLOCKS] + [MAX_SLC_BLOCKS]
    for lo, nrow in zip([0] + row_buckets[:-1], row_buckets):
        in_bucket = (blocks_started > lo) if nrow == MAX_SLC_BLOCKS else ((blocks_started > lo) & (blocks_started <= nrow))

        @pl.when(in_bucket)
        def _pick(nrow=nrow):
            def group(lb, carry):
                pick_rounds(lb, range(base_rounds), nrow)
                return carry

            lax.fori_loop(0, n_groups, group, 0)

    @pl.when(t0 < 2 * SLC_LEN)
    def _early_queries():
        pick_rounds(0, range(base_rounds, topk - 1), row_buckets[0])

    picked = jnp.concatenate([score_ref[lb] for lb in range(n_groups)], axis=1) == -jnp.inf
    bias_ref[...] = jnp.where((valid & forced) | (cand & picked), 0.0, NEG).astype(BF16)


def _cmp_attn_call(qt, cmp_kv, cmp_kv_t, gates, *, n_slc, topk):
    B, _, S = qt.shape
    ncmp = cmp_kv.shape[3]
    tq = min(CMP_Q_TILE, S)
    assert topk >= 3 and tq >= 2 * SLC_LEN
    kern = functools.partial(_cmp_attn_kernel, n_slc=n_slc, topk=topk)
    return pl.pallas_call(
        kern,
        grid=(B, N_KV_HEADS, S // tq),
        in_specs=[pl.BlockSpec((None, GROUP_WIDTH, tq), lambda b, k, i: (b, k, i)),
                  pl.BlockSpec((None, None, None, ncmp, 2 * HEAD_DIM), lambda b, k, i: (0, b, k, 0, 0)),
                  pl.BlockSpec((None, None, None, HEAD_DIM, ncmp), lambda b, k, i: (1, b, k, 0, 0)),
                  pl.BlockSpec((None, None, GATE_ROWS, tq), lambda b, k, i: (b, k, 0, i))],
        out_specs=[pl.BlockSpec((None, GROUP_WIDTH, tq), lambda b, k, i: (b, k, i)),
                   pl.BlockSpec((None, None, MAX_SLC_BLOCKS, tq), lambda b, k, i: (b, k, 0, i))],
        out_shape=[jax.ShapeDtypeStruct((B, ATTN_WIDTH, S), F32),
                   jax.ShapeDtypeStruct((B, N_KV_HEADS, MAX_SLC_BLOCKS, S), BF16)],
        scratch_shapes=[pltpu.VMEM((MAX_SLC_BLOCKS, tq), F32),
                        pltpu.VMEM((tq // PICK_LANES, MAX_SLC_BLOCKS, PICK_LANES), F32),
                        pltpu.VMEM((tq // PICK_LANES, 8, PICK_LANES), jnp.int32)],
        compiler_params=_params("arbitrary", "arbitrary", "arbitrary"),
        name="cmp_attn",
    )(qt, cmp_kv, cmp_kv_t, gates)


V_ROWS = HEAD_DIM + 16
SLC_HEAD_GROUPS = ((0, 1), (2, 3))
SKIP_MARGIN = 176.0


def _slc_attn_kernel(qt_ref, bias_ref, ks_ref, vt_ref, gates_ref, other_ref, attn_ref,
                     kaug_ref, vh_ref, qaug_ref, m_ref, acc_ref, sa_ref, sb_ref, ta_ref, tb_ref, ksq_ref):
    k_head = pl.program_id(1)
    i = pl.program_id(2)
    tq = qt_ref.shape[1]
    n_tiles = kaug_ref.shape[0]
    t0 = i * tq

    @pl.when(i == 0)
    def _build_keys():
        def fill(c, k_sq_max):
            r0 = pl.multiple_of(c * tq, tq)
            k_tile = _pick_head_lanes(ks_ref[pl.ds(r0, tq), :], k_head)
            kaug_ref[c, :, 0:HEAD_DIM] = k_tile
            k_f = k_tile.astype(F32)
            return jnp.maximum(k_sq_max, jnp.max(jnp.sum(k_f * k_f, axis=1, keepdims=True)))

        ksq_ref[0] = lax.fori_loop(0, n_tiles, fill, jnp.float32(0.0))
        for c in range(n_tiles):
            vh_ref[c, 0:HEAD_DIM, :] = _pick_head_rows(vt_ref[:, c * tq:(c + 1) * tq], k_head)

    @pl.when((i == 0) & (k_head == 0) & (pl.program_id(0) == 0))
    def _build_static():
        def fill(c, carry):
            r0 = pl.multiple_of(c * tq, tq)
            pos = r0 + lax.broadcasted_iota(jnp.int32, (tq, HEAD_DIM), 0)
            kaug_ref[c, :, HEAD_DIM:2 * HEAD_DIM] = _alibi_key_cols(pos).astype(BF16)
            blk = (r0 + lax.broadcasted_iota(jnp.int32, (tq, MAX_SLC_BLOCKS), 0)) // SLC_LEN
            hot = blk == lax.broadcasted_iota(jnp.int32, (tq, MAX_SLC_BLOCKS), 1)
            kaug_ref[c, :, 2 * HEAD_DIM:] = jnp.where(hot, 1.0, 0.0).astype(BF16)
            return carry

        lax.fori_loop(0, n_tiles, fill, 0)
        ones_row = lax.broadcasted_iota(jnp.int32, (V_ROWS - HEAD_DIM, tq), 0) == 0
        for c in range(n_tiles):
            vh_ref[c, HEAD_DIM:, :] = jnp.where(ones_row, 1.0, 0.0).astype(BF16)

    t0f = t0.astype(F32)
    bias = bias_ref[...]
    first_tile = []
    for g in range(GQA_GROUP):
        c_g = _slope(k_head, g)
        q_g = qt_ref[g * HEAD_DIM:(g + 1) * HEAD_DIM, :]
        aux = _alibi_query_rows(c_g, t0f, HEAD_DIM, tq)
        qaug_ref[0:HEAD_DIM, g * tq:(g + 1) * tq] = q_g
        qaug_ref[HEAD_DIM:2 * HEAD_DIM, g * tq:(g + 1) * tq] = aux.astype(BF16)
        qaug_ref[2 * HEAD_DIM:, g * tq:(g + 1) * tq] = bias
        q_f = q_g.astype(F32)
        qk = jnp.sqrt(jnp.max(jnp.sum(q_f * q_f, axis=0, keepdims=True), axis=1, keepdims=True) * ksq_ref[0])
        dist_needed = jnp.minimum((2.0 * qk[0, 0] + SKIP_MARGIN) / c_g, 1e6).astype(jnp.int32)
        first_tile.append(i - jnp.minimum(i, (dist_needed - 1) // tq + 1))
    m_ref[...] = jnp.full(m_ref.shape, NEG, F32)
    acc_ref[...] = jnp.zeros(acc_ref.shape, F32)

    s_bufs = (sa_ref, sb_ref)

    tmax_bufs = (ta_ref, tb_ref)

    def scores_head(k_tile, g, slot, causal):
        cols = slice(g * tq, (g + 1) * tq)
        s = _nn(k_tile, qaug_ref[:, cols])
        if causal:
            key = lax.broadcasted_iota(jnp.int32, (tq, tq), 0)
            qry = lax.broadcasted_iota(jnp.int32, (tq, tq), 1)
            s = jnp.where(key <= qry, s, NEG)
        s_bufs[slot][:, cols] = s
        tmax_bufs[slot][:, cols] = jnp.max(s, axis=0, keepdims=True)

    def scores(j, slot, causal, heads):
        k_tile = kaug_ref[j]
        for g in heads:
            scores_head(k_tile, g, slot, causal)

    def step(j, slot, prefetch, heads):
        v_tile = vh_ref[j]
        k_next = kaug_ref[j + 1] if prefetch is not None else None
        for g in heads:
            cols = slice(g * tq, (g + 1) * tq)
            if prefetch is not None:
                scores_head(k_next, g, 1 - slot, prefetch == 'causal')
            m_old = m_ref[:, cols]
            m_new = jnp.maximum(m_old, tmax_bufs[slot][:, cols])
            alpha = jnp.exp2(m_old - m_new)
            p = jnp.exp2(s_bufs[slot][:, cols] - m_new).astype(BF16)
            acc_ref[:, cols] = alpha * acc_ref[:, cols] + _nn(v_tile, p)
            m_ref[:, cols] = m_new

    def run(heads, lo):
        n_plain = i - lo

        @pl.when(n_plain == 0)
        def _first_is_causal():
            scores(i, 0, True, heads)

        @pl.when(n_plain > 0)
        def _first_is_plain():
            scores(lo, 0, False, heads)

        def trip(p, carry):
            for u in range(SLC_UNROLL):
                step(lo + SLC_UNROLL * p + u, u % 2, 'plain', heads)
            return carry

        n_trips = jnp.maximum(n_plain - 1, 0) // SLC_UNROLL
        lax.fori_loop(0, n_trips, trip, 0)
        done = SLC_UNROLL * n_trips
        for n_left in range(1, SLC_UNROLL + 2):
            @pl.when(n_plain - done + 1 == n_left)
            def _tail(n_left=n_left):
                for u in range(n_left):
                    prefetch = (None, 'causal')[u == n_left - 2] if u >= n_left - 2 else 'plain'
                    step(lo + done + u, u % 2, prefetch, heads)

    for heads in SLC_HEAD_GROUPS:
        run(heads, functools.reduce(jnp.minimum, [first_tile[g] for g in heads]))

    o = acc_ref[0:HEAD_DIM, :] / acc_ref[HEAD_DIM:HEAD_DIM + 1, :]
    for g in range(GQA_GROUP):
        gate = _sigmoid(gates_ref[GQA_GROUP + g:GQA_GROUP + g + 1, :])
        rows = slice(g * HEAD_DIM, (g + 1) * HEAD_DIM)
        attn_ref[rows, :] = (other_ref[rows, :] + o[:, g * tq:(g + 1) * tq] * gate).astype(BF16)


def _slc_attn_call(qt, bias, kk, vt, gates, other):
    B, _, S = qt.shape
    tq = min(SLC_TILE, S)
    return pl.pallas_call(
        _slc_attn_kernel,
        grid=(B, N_KV_HEADS, S // tq),
        in_specs=[pl.BlockSpec((None, GROUP_WIDTH, tq), lambda b, k, i: (b, k, i)),
                  pl.BlockSpec((None, None, MAX_SLC_BLOCKS, tq), lambda b, k, i: (b, k, 0, i)),
                  pl.BlockSpec((None, S, KV_WIDTH), lambda b, k, i: (b, 0, 0)),
                  pl.BlockSpec((None, KV_WIDTH, S), lambda b, k, i: (b, 0, 0)),
                  pl.BlockSpec((None, None, GATE_ROWS, tq), lambda b, k, i: (b, k, 0, i)),
                  pl.BlockSpec((None, GROUP_WIDTH, tq), lambda b, k, i: (b, k, i))],
        out_specs=pl.BlockSpec((None, GROUP_WIDTH, tq), lambda b, k, i: (b, k, i)),
        out_shape=jax.ShapeDtypeStruct((B, ATTN_WIDTH, S), BF16),
        scratch_shapes=[pltpu.VMEM((S // tq, tq, 2 * LANES), BF16),
                        pltpu.VMEM((S // tq, V_ROWS, tq), BF16),
                        pltpu.VMEM((2 * LANES, GQA_GROUP * tq), BF16),
                        pltpu.VMEM((1, GQA_GROUP * tq), F32),
                        pltpu.VMEM((V_ROWS, GQA_GROUP * tq), F32),
                        pltpu.VMEM((tq, GQA_GROUP * tq), F32),
                        pltpu.VMEM((tq, GQA_GROUP * tq), F32),
                        pltpu.VMEM((1, GQA_GROUP * tq), F32),
                        pltpu.VMEM((1, GQA_GROUP * tq), F32),
                        pltpu.SMEM((1,), F32)],
        compiler_params=_params("arbitrary", "arbitrary", "arbitrary"),
        name="slc_attn",
    )(qt, bias, kk, vt, gates, other)


def _win_attn_kernel(qt_ref, kprev_ref, kcur_ref, vprev_ref, vcur_ref, gates_ref, other_ref, ow_ref):
    k_head = pl.program_id(1)
    i = pl.program_id(2)
    tq = qt_ref.shape[1] // 2
    key = lax.broadcasted_iota(jnp.int32, (tq, tq), 0)
    qry = lax.broadcasted_iota(jnp.int32, (tq, tq), 1)
    keeps = (qry < key, None, key <= qry)
    ones = jnp.where(lax.broadcasted_iota(jnp.int32, (16, tq), 0) == 0, 1.0, 0.0).astype(BF16)
    lane = lax.broadcasted_iota(jnp.int32, (tq, KV_WIDTH), 1)
    cols_in_tile = _alibi_key_cols(lax.broadcasted_iota(jnp.int32, (tq, KV_WIDTH), 0))
    high_lanes = jnp.where((lane == 0) | (lane == 2), 1.0, 0.0)
    flag_lane = jnp.where(lane == 8, 1.0, 0.0)
    k_augs, v_tiles = [], []
    for n, (k_ref, v_ref) in enumerate(((kprev_ref, vprev_ref), (kprev_ref, vprev_ref),
                                        (kcur_ref, vcur_ref), (kcur_ref, vcur_ref))):
        half = slice((n % 2) * tq, (n % 2 + 1) * tq)
        tile = 2 * i - 2 + n
        k_aux = (cols_in_tile + high_lanes * (tile * (tq // LANES)).astype(F32)
                 + flag_lane * jnp.where(tile < 0, 1.0, 0.0))
        k_augs.append(jnp.concatenate([k_ref[half, :], k_aux.astype(BF16)], axis=1))
        v_tiles.append(jnp.concatenate([v_ref[:, half], ones], axis=0))

    q2s = []
    for h in range(2):
        cols = slice(h * tq, (h + 1) * tq)
        t0f = ((2 * i + h) * tq).astype(F32)
        qs = jnp.concatenate([qt_ref[g * HEAD_DIM:(g + 1) * HEAD_DIM, cols] for g in range(GQA_GROUP)], axis=1)
        alibi = jnp.concatenate([_alibi_query_rows(_slope(k_head, g), t0f, KV_WIDTH, tq, row8=NEG)
                                 for g in range(GQA_GROUP)], axis=1).astype(BF16)
        q2s.append(jnp.concatenate([jnp.where(k_head == k, qs, jnp.zeros_like(qs)) for k in range(N_KV_HEADS)]
                                   + [alibi], axis=0))
    mask = lambda keep, s: s if keep is None else jnp.where(keep, s, NEG)
    scores = [[[mask(keep, _nn(k_aug, q2s[h][:, g * tq:(g + 1) * tq]))
                for k_aug, keep in zip(k_augs[h:h + 3], keeps)]
               for g in range(GQA_GROUP)] for h in range(2)]
    for h in range(2):
        for g in range(GQA_GROUP):
            m = functools.reduce(jnp.maximum, [jnp.max(s, axis=0, keepdims=True) for s in scores[h][g]])
            o2 = sum(_nn(v, jnp.exp2(s - m).astype(BF16)) for v, s in zip(v_tiles[h:h + 3], scores[h][g]))
            o = _pick_head_rows(o2[:KV_WIDTH], k_head) / o2[KV_WIDTH:KV_WIDTH + 1]
            gate = _sigmoid(gates_ref[2 * GQA_GROUP + g:2 * GQA_GROUP + g + 1, h * tq:(h + 1) * tq])
            rows, cols = slice(g * HEAD_DIM, (g + 1) * HEAD_DIM), slice(h * tq, (h + 1) * tq)
            ow_ref[rows, cols] = other_ref[rows, cols] + o * gate


def _win_attn_call(qt, kk, vt, gates, other):
    B, _, S = qt.shape
    tq = min(Q_TILE, S)
    assert WINDOW == 2 * tq and S % (2 * tq) == 0
    k_spec = lambda back: pl.BlockSpec((None, 2 * tq, KV_WIDTH), lambda b, k, i: (b, jnp.maximum(i - back, 0), 1))
    v_spec = lambda back: pl.BlockSpec((None, KV_WIDTH, 2 * tq), lambda b, k, i: (b, 1, jnp.maximum(i - back, 0)))
    return pl.pallas_call(
        _win_attn_kernel,
        grid=(B, N_KV_HEADS, S // (2 * tq)),
        in_specs=[pl.BlockSpec((None, GROUP_WIDTH, 2 * tq), lambda b, k, i: (b, k, i)),
                  k_spec(1), k_spec(0), v_spec(1), v_spec(0),
                  pl.BlockSpec((None, None, GATE_ROWS, 2 * tq), lambda b, k, i: (b, k, 0, i)),
                  pl.BlockSpec((None, GROUP_WIDTH, 2 * tq), lambda b, k, i: (b, k, i))],
        out_specs=pl.BlockSpec((None, GROUP_WIDTH, 2 * tq), lambda b, k, i: (b, k, i)),
        out_shape=jax.ShapeDtypeStruct((B, ATTN_WIDTH, S), F32),
        compiler_params=_params("arbitrary", "arbitrary", "arbitrary"),
        name="win_attn",
    )(qt, kk, kk, vt, vt, gates, other)


def _group_rms(a, gain):
    c, tm = a.shape
    a3 = a.reshape(c // HEAD_DIM, HEAD_DIM, tm)
    ms = jnp.mean(a3 * a3, axis=1, keepdims=True)
    return (a3 * lax.rsqrt(ms + LN_EPS) * gain).reshape(c, tm)


def _out_proj_kernel(attn_ref, mlp_ref, x_ref, g1_ref, og_ref, wo_ref, lng_ref, lnb_ref, o_ref, *, alpha):
    n_attn = ATTN_WIDTH // HEAD_DIM
    attn = _group_rms(attn_ref[...].astype(F32), og_ref[:n_attn])
    mlp = _group_rms(mlp_ref[...].astype(F32), og_ref[n_attn:])
    y_t = jnp.concatenate([attn, mlp], axis=0).astype(BF16)
    y = _tn(y_t, wo_ref[...])
    z = alpha * x_ref[...] + (1.0 + g1_ref[...]) * y
    o_ref[...] = _layer_norm_rows(z, lng_ref[...], lnb_ref[...])


def _out_proj_call(attn, mlp, x, g1, out_g, w_o, ln_g, ln_b, *, alpha):
    B, S, D = x.shape
    tm = min(TOK_TILE, S)
    cm = lambda width: pl.BlockSpec((None, width, tm), lambda b, i: (b, 0, i))
    row = pl.BlockSpec((1, D), lambda b, i: (0, 0))
    return pl.pallas_call(
        functools.partial(_out_proj_kernel, alpha=alpha),
        grid=(B, S // tm),
        in_specs=[cm(ATTN_WIDTH), cm(MLP_WIDTH),
                  pl.BlockSpec((None, tm, D), lambda b, i: (b, i, 0)),
                  pl.BlockSpec((None, 1, D), lambda b, i: (b, 0, 0)),
                  pl.BlockSpec(out_g.shape, lambda b, i: (0, 0, 0)),
                  pl.BlockSpec(w_o.shape, lambda b, i: (0, 0)),
                  row, row],
        out_specs=pl.BlockSpec((None, tm, D), lambda b, i: (b, i, 0)),
        out_shape=jax.ShapeDtypeStruct((B, S, D), F32),
        compiler_params=_params("arbitrary", "arbitrary"),
        name="out_proj",
    )(attn, mlp, x, g1, out_g, w_o, ln_g, ln_b)


def _ffn_kernel(x_ref, sc_ref, sh_ref, g2_ref, w1_ref, w3_ref, w2_ref, lng_ref, lnb_ref, o_ref, *, alpha, splits):
    x = x_ref[...]
    h = (x * (1.0 + sc_ref[...]) + sh_ref[...]).astype(BF16)
    f = None
    for lo, hi in splits:
        a = _nn(h, w1_ref[:, lo:hi])
        b = _nn(h, w3_ref[:, lo:hi])
        part = _nn((a * _sigmoid(a) * b).astype(BF16), w2_ref[lo:hi, :])
        f = part if f is None else f + part
    z = alpha * x + (1.0 + g2_ref[...]) * f
    o_ref[...] = _layer_norm_rows(z, lng_ref[...], lnb_ref[...])


def _ffn_call(x, sc, sh, g2, w1, w3, w2, ln_g, ln_b, *, alpha):
    B, S, D = x.shape
    d_ff = w1.shape[1]
    tm = min(TOK_TILE, S)
    half = (d_ff // 2 + 255) // 256 * 256
    splits = ((0, half), (half, d_ff))
    mod = pl.BlockSpec((None, 1, D), lambda b, i: (b, 0, 0))
    row = pl.BlockSpec((1, D), lambda b, i: (0, 0))
    resident = lambda shape: pl.BlockSpec(shape, lambda b, i: (0, 0), pipeline_mode=pl.Buffered(1))
    return pl.pallas_call(
        functools.partial(_ffn_kernel, alpha=alpha, splits=splits),
        grid=(B, S // tm),
        in_specs=[pl.BlockSpec((None, tm, D), lambda b, i: (b, i, 0)), mod, mod, mod,
                  resident(w1.shape), resident(w3.shape), resident(w2.shape), row, row],
        out_specs=pl.BlockSpec((None, tm, D), lambda b, i: (b, i, 0)),
        out_shape=jax.ShapeDtypeStruct((B, S, D), F32),
        compiler_params=_params("arbitrary", "arbitrary"),
        name="ffn",
    )(x, sc, sh, g2, w1, w3, w2, ln_g, ln_b)


def _in_proj_weights(w_in):
    sizes = (ATTN_WIDTH,) + (KV_WIDTH,) * 6 + (N_BRANCH * N_ATTN_HEADS, MLP_WIDTH, MLP_WIDTH)
    offs = [0]
    for s in sizes:
        offs.append(offs[-1] + s)
    col = lambda n: w_in[:, offs[n]:offs[n + 1]]
    q, kc, vc, ksl, vsl, kwn, vwn, gt, u, v = (col(n) for n in range(10))
    w_a = jnp.concatenate([ksl, kwn, kc, vc], axis=1).astype(BF16)
    D = w_in.shape[0]
    gt = gt.reshape(D, N_KV_HEADS, GQA_GROUP, N_BRANCH).transpose(1, 3, 2, 0)
    gt = gt.reshape(N_KV_HEADS, N_BRANCH * GQA_GROUP, D)
    gt = jnp.pad(gt, ((0, 0), (0, GATE_ROWS - N_BRANCH * GQA_GROUP), (0, 0)))
    gt = gt.reshape(N_KV_HEADS * GATE_ROWS, D)
    w_t = jnp.concatenate([u.T, v.T, q.T, vsl.T, vwn.T, gt], axis=0).astype(BF16)
    return w_a, w_t


def _compress_weights(cmp_w1):
    two, _, hid = cmp_w1.shape
    w = cmp_w1.reshape(two, 2, CMP_STRIDE, HEAD_DIM, hid)
    eye = jnp.eye(N_KV_HEADS, dtype=cmp_w1.dtype)
    big = jnp.einsum('thpdc,kj->tpkdjhc', w, eye)
    return big.reshape(two, CMP_STRIDE * N_KV_HEADS * HEAD_DIM, N_KV_HEADS * 2 * hid).astype(BF16)


def _hybrid_layer(x, mod, w_in, cmp_pos, cmp_w1, cmp_w2, vn_g, vn_b, w_s, b_s, out_g, w_o, ln1_g, ln1_b,
                  w1, w3, w2, ln2_g, ln2_b, *, alpha):
    B, S, D = x.shape
    assert S % Q_TILE == 0 and S % TOK_TILE == 0 and S // SLC_LEN <= MAX_SLC_BLOCKS
    sh1, sc1, g1, sh2, sc2, g2 = (mod[:, None, n * D:(n + 1) * D] for n in range(6))
    w_a, w_t = _in_proj_weights(w_in)
    kk, kvc, qt, vt, gates, mlp = _in_proj_call(
        x, sc1, sh1, w_a, w_t,
        vn_g.reshape(N_MLP_GROUPS, MLP_GROUP_DIM, 1), vn_b.reshape(N_MLP_GROUPS, MLP_GROUP_DIM, 1),
        jnp.swapaxes(w_s, 1, 2), b_s)
    ngrp = S // CMP_STRIDE
    tok = kvc.reshape(2, B, ngrp, CMP_STRIDE * KV_WIDTH)
    pos = jnp.broadcast_to(cmp_pos.reshape(2, 1, CMP_LEN * HEAD_DIM), (2, 8, CMP_LEN * HEAD_DIM))
    cmp_kv, cmp_kv_t = _compress_call(tok, _compress_weights(cmp_w1), cmp_w1.astype(BF16), pos, cmp_w2.astype(BF16))
    gates = gates.reshape(B, N_KV_HEADS, GATE_ROWS, S)
    n_slc = S // SLC_LEN
    oc, bias = _cmp_attn_call(qt, cmp_kv, cmp_kv_t, gates, n_slc=n_slc, topk=min(SLC_TOPK, n_slc))
    ocw = _win_attn_call(qt, kk, vt, gates, oc)
    attn = _slc_attn_call(qt, bias, kk, vt, gates, ocw)
    row = lambda a: a.reshape(1, D)
    x = _out_proj_call(attn, mlp, x, g1, out_g.reshape(-1, HEAD_DIM, 1), w_o.astype(BF16),
                       row(ln1_g), row(ln1_b), alpha=alpha)
    return _ffn_call(x, sc2, sh2, g2, w1.astype(BF16), w3.astype(BF16), w2.astype(BF16),
                     row(ln2_g), row(ln2_b), alpha=alpha)


def kernel(x, c, w_ada, b_ada, w_in, cmp_pos, cmp_w1, cmp_w2, vn_g, vn_b, w_s, b_s, out_g, w_o, ln1_g, ln1_b,
           w1, w3, w2, ln2_g, ln2_b):
    depth = w_ada.shape[0]
    alpha = (2.0 * depth) ** 0.25
    mod = _ada_call(c, w_ada, b_ada)
    for l in range(depth):
        x = _hybrid_layer(x, mod[l], w_in[l], cmp_pos[l], cmp_w1[l], cmp_w2[l], vn_g[l], vn_b[l], w_s[l], b_s[l],
                          out_g[l], w_o[l], ln1_g[l], ln1_b[l], w1[l], w3[l], w2[l], ln2_g[l], ln2_b[l], alpha=alpha)
    return x
```

```python
import functools
import math

import jax
import jax.numpy as jnp
from jax import lax
from jax.experimental import pallas as pl
from jax.experimental.pallas import tpu as pltpu

F32 = jnp.float32
BF16 = jnp.bfloat16

HEAD_DIM = 64
N_KV_HEADS = 2
GQA_GROUP = 4
N_ATTN_HEADS = N_KV_HEADS * GQA_GROUP
ATTN_WIDTH = N_ATTN_HEADS * HEAD_DIM
KV_WIDTH = N_KV_HEADS * HEAD_DIM
GROUP_WIDTH = GQA_GROUP * HEAD_DIM
N_BRANCH = 3
MLP_GROUP_DIM = 64
N_MLP_GROUPS = 8
MLP_WIDTH = N_MLP_GROUPS * MLP_GROUP_DIM
CMP_LEN = 32
CMP_STRIDE = 16
CMP_HIDDEN = 256
SLC_LEN = 64
SLC_TOPK = 16
WINDOW = 512
CHUNK = 128
LN_EPS = 1e-5
NEG = -1e30
FORCED_SCORE = 1e6

LOG2E = math.log2(math.e)
Q_SCALE = HEAD_DIM ** -0.5 * LOG2E

LANES = 128
MAX_SLC_BLOCKS = LANES
TOK_TILE = 512
Q_TILE = 256
SLC_TILE = 512
SLC_UNROLL = 4
CMP_Q_TILE = 512
PICK_LANES = 256
CMP_ROW_CHUNK = 128
VMEM_LIMIT = 56 * 1024 * 1024
GATE_ROWS = 16

ROW_V = MLP_WIDTH
ROW_Q = ROW_V + MLP_WIDTH
ROW_VSLC = ROW_Q + ATTN_WIDTH
ROW_VWIN = ROW_VSLC + KV_WIDTH
ROW_GATE = ROW_VWIN + KV_WIDTH
ROWS_T = ROW_GATE + N_KV_HEADS * GATE_ROWS
COLS_A = 4 * KV_WIDTH


def _nt(a, b):
    return lax.dot_general(a, b, (((1,), (1,)), ((), ())), preferred_element_type=F32)


def _tn(a, b):
    return lax.dot_general(a, b, (((0,), (0,)), ((), ())), preferred_element_type=F32)


def _nn(a, b):
    return jnp.dot(a, b, preferred_element_type=F32)


def _gelu(x):
    return x * (0.5 * (1.0 + jnp.tanh(math.sqrt(2.0 / math.pi) * (x + 0.044715 * (x * x * x)))))


def _sigmoid(x):
    return 1.0 / (1.0 + jnp.exp(-x))


def _layer_norm_rows(z, g, b):
    mu = jnp.mean(z, axis=-1, keepdims=True)
    zc = z - mu
    var = jnp.mean(zc * zc, axis=-1, keepdims=True)
    return zc * lax.rsqrt(var + LN_EPS) * g + b


def _slope(kv_head, g):
    slope = lambda h: LOG2E * 2.0 ** (-8.0 * (h + 1) / N_ATTN_HEADS)
    val = jnp.asarray(slope(g), F32)
    for k in range(1, N_KV_HEADS):
        val = jnp.where(kv_head == k, jnp.asarray(slope(k * GQA_GROUP + g), F32), val)
    return val


def _pick_head_lanes(x, kv_head):
    out = x[:, :HEAD_DIM]
    for k in range(1, N_KV_HEADS):
        out = jnp.where(kv_head == k, x[:, k * HEAD_DIM:(k + 1) * HEAD_DIM], out)
    return out


def _pick_head_rows(x, kv_head):
    out = x[:HEAD_DIM]
    for k in range(1, N_KV_HEADS):
        out = jnp.where(kv_head == k, x[k * HEAD_DIM:(k + 1) * HEAD_DIM], out)
    return out


def _bf16_part(x):
    return x.astype(BF16).astype(F32)


def _alibi_key_cols(pos):
    lane = lax.broadcasted_iota(jnp.int32, pos.shape, 1)
    cols = jnp.where(lane % 2 == 0, pos // LANES, pos % LANES)
    return jnp.where(lane < 4, cols, jnp.where(lane < 8, 1, 0)).astype(F32)


def _alibi_query_rows(c, t0f, rows, tq, row8=0.0):
    shape = (8, tq)
    row = lax.broadcasted_iota(jnp.int32, shape, 0)
    c = jnp.full(shape, c, F32)
    out = jnp.zeros(shape, F32)
    for n, part in enumerate((_bf16_part(c), _bf16_part(c - _bf16_part(c)))):
        shift = part * t0f
        vals = {2 * n: LANES * part, 2 * n + 1: part, 4 + 2 * n: -_bf16_part(shift), 5 + 2 * n: -(shift - _bf16_part(shift))}
        for r, val in vals.items():
            out = jnp.where(row == r, val, out)
    tail = jnp.where(lax.broadcasted_iota(jnp.int32, (rows - 8, tq), 0) == 0, row8, 0.0)
    return jnp.concatenate([out, tail], axis=0)


def _params(*sem):
    return pltpu.CompilerParams(dimension_semantics=sem, vmem_limit_bytes=VMEM_LIMIT)


def _ada_kernel(c_ref, w_ref, b_ref, o_ref):
    c = c_ref[...]
    c_act = (c * _sigmoid(c)).astype(BF16)
    o_ref[0] = _nn(c_act, w_ref[0].astype(BF16)) + b_ref[0]


def _ada_call(c, w_ada, b_ada):
    L, D, D6 = w_ada.shape
    B = c.shape[0]
    return pl.pallas_call(
        _ada_kernel,
        grid=(L, D6 // D),
        in_specs=[pl.BlockSpec((B, D), lambda l, n: (0, 0)),
                  pl.BlockSpec((1, D, D), lambda l, n: (l, 0, n)),
                  pl.BlockSpec((1, 1, D), lambda l, n: (l, 0, n))],
        out_specs=pl.BlockSpec((1, B, D), lambda l, n: (l, 0, n)),
        out_shape=jax.ShapeDtypeStruct((L, B, D6), F32),
        compiler_params=_params("arbitrary", "arbitrary"),
        name="ada_mod",
    )(c, w_ada, b_ada.reshape(L, 1, D6))


def _in_proj_kernel(x_ref, sc_ref, sh_ref, wa_ref, wt_ref, vng_ref, vnb_ref, wst_ref, bs_ref,
                    kk_ref, kvc_ref, qt_ref, vt_ref, gates_ref, mlp_ref):
    tm = x_ref.shape[0]
    h = (x_ref[...] * (1.0 + sc_ref[...]) + sh_ref[...]).astype(BF16)
    uv = _nt(wt_ref[0:ROW_Q, :], h)
    t = _nt(wt_ref[ROW_Q:, :], h)
    qt_ref[...] = (t[:ATTN_WIDTH] * Q_SCALE).astype(BF16)
    vt_ref[...] = t[ROW_VSLC - ROW_Q:ROW_GATE - ROW_Q].astype(BF16)
    gates_ref[...] = t[ROW_GATE - ROW_Q:]
    a = _nn(h, wa_ref[...])
    kk_ref[...] = a[:, :2 * KV_WIDTH].astype(BF16)
    kvc_ref[0] = a[:, 2 * KV_WIDTH:3 * KV_WIDTH].astype(BF16)
    kvc_ref[1] = a[:, 3 * KV_WIDTH:].astype(BF16)

    u = _gelu(uv[:ROW_V])
    v = _gelu(uv[ROW_V:]).reshape(N_MLP_GROUPS, MLP_GROUP_DIM, tm)
    mu = jnp.mean(v, axis=1, keepdims=True)
    vc = v - mu
    var = jnp.mean(vc * vc, axis=1, keepdims=True)
    vn = vc * lax.rsqrt(var + LN_EPS) * vng_ref[...] + vnb_ref[...]
    n_chunks = tm // CHUNK
    s_idx = lax.broadcasted_iota(jnp.int32, (CHUNK, CHUNK), 0)
    t_idx = lax.broadcasted_iota(jnp.int32, (CHUNK, CHUNK), 1)
    for g in range(N_MLP_GROUPS):
        vg = vn[g].astype(BF16)
        stack = jnp.concatenate([vg[:, c * CHUNK:(c + 1) * CHUNK] for c in range(n_chunks)], axis=0)
        w_t = jnp.where(s_idx <= t_idx, wst_ref[g], 0.0).astype(BF16)
        sv = _nn(stack, w_t)
        bias = bs_ref[g:g + 1, :]
        for c in range(n_chunks):
            mlp_ref[g * MLP_GROUP_DIM:(g + 1) * MLP_GROUP_DIM, c * CHUNK:(c + 1) * CHUNK] = (
                u[g * MLP_GROUP_DIM:(g + 1) * MLP_GROUP_DIM, c * CHUNK:(c + 1) * CHUNK]
                * (sv[c * MLP_GROUP_DIM:(c + 1) * MLP_GROUP_DIM] + bias)).astype(BF16)


def _in_proj_call(x, sc, sh, w_a, w_t, vn_g, vn_b, w_st, b_s):
    B, S, D = x.shape
    tm = min(TOK_TILE, S)
    const2 = lambda b, i: (0, 0)
    const3 = lambda b, i: (0, 0, 0)
    cm = lambda rows: pl.BlockSpec((None, rows, tm), lambda b, i: (b, 0, i))
    return pl.pallas_call(
        _in_proj_kernel,
        grid=(B, S // tm),
        in_specs=[pl.BlockSpec((None, tm, D), lambda b, i: (b, i, 0)),
                  pl.BlockSpec((None, 1, D), lambda b, i: (b, 0, 0)),
                  pl.BlockSpec((None, 1, D), lambda b, i: (b, 0, 0)),
                  pl.BlockSpec(w_a.shape, const2),
                  pl.BlockSpec(w_t.shape, const2),
                  pl.BlockSpec(vn_g.shape, const3),
                  pl.BlockSpec(vn_b.shape, const3),
                  pl.BlockSpec(w_st.shape, const3),
                  pl.BlockSpec(b_s.shape, const2)],
        out_specs=[pl.BlockSpec((None, tm, 2 * KV_WIDTH), lambda b, i: (b, i, 0)),
                   pl.BlockSpec((2, None, tm, KV_WIDTH), lambda b, i: (0, b, i, 0)),
                   cm(ATTN_WIDTH), cm(2 * KV_WIDTH), cm(N_KV_HEADS * GATE_ROWS), cm(MLP_WIDTH)],
        out_shape=[jax.ShapeDtypeStruct((B, S, 2 * KV_WIDTH), BF16),
                   jax.ShapeDtypeStruct((2, B, S, KV_WIDTH), BF16),
                   jax.ShapeDtypeStruct((B, ATTN_WIDTH, S), BF16),
                   jax.ShapeDtypeStruct((B, 2 * KV_WIDTH, S), BF16),
                   jax.ShapeDtypeStruct((B, N_KV_HEADS * GATE_ROWS, S), F32),
                   jax.ShapeDtypeStruct((B, MLP_WIDTH, S), BF16)],
        compiler_params=_params("arbitrary", "arbitrary"),
        name="in_proj",
    )(x, sc, sh, w_a, w_t, vn_g, vn_b, w_st, b_s)


def _compress_kernel(tok_ref, wbig_ref, w1_ref, pos_ref, w2_ref, out_ref, out_t_ref):
    ab = _nn(tok_ref[...], wbig_ref[...])
    bias = _nn(pos_ref[...].astype(BF16), w1_ref[...])[0:1]
    ngrp = ab.shape[0]
    for h in range(N_KV_HEADS):
        base = h * 2 * CMP_HIDDEN
        first = ab[:, base:base + CMP_HIDDEN]
        second = ab[:, base + CMP_HIDDEN:base + 2 * CMP_HIDDEN]
        hid = _gelu(first + pltpu.roll(second, ngrp - 1, 0) + bias)
        c = _nn(hid.astype(BF16), w2_ref[...])
        block_end = CMP_STRIDE * lax.broadcasted_iota(jnp.int32, (ngrp, HEAD_DIM), 0) + (CMP_LEN - 1)
        out_ref[h, :, 0:HEAD_DIM] = c.astype(BF16)
        out_ref[h, :, HEAD_DIM:] = _alibi_key_cols(block_end).astype(BF16)
        out_t_ref[h] = c.T.astype(BF16)


def _compress_call(tok, wbig, w1, pos, w2):
    two, B, ngrp, flat = tok.shape
    return pl.pallas_call(
        _compress_kernel,
        grid=(two, B),
        in_specs=[pl.BlockSpec((None, None, ngrp, flat), lambda t, b: (t, b, 0, 0)),
                  pl.BlockSpec((None,) + wbig.shape[1:], lambda t, b: (t, 0, 0)),
                  pl.BlockSpec((None,) + w1.shape[1:], lambda t, b: (t, 0, 0)),
                  pl.BlockSpec((None,) + pos.shape[1:], lambda t, b: (t, 0, 0)),
                  pl.BlockSpec((None,) + w2.shape[1:], lambda t, b: (t, 0, 0))],
        out_specs=[pl.BlockSpec((None, None, N_KV_HEADS, ngrp, 2 * HEAD_DIM), lambda t, b: (t, b, 0, 0, 0)),
                   pl.BlockSpec((None, None, N_KV_HEADS, HEAD_DIM, ngrp), lambda t, b: (t, b, 0, 0, 0))],
        out_shape=[jax.ShapeDtypeStruct((two, B, N_KV_HEADS, ngrp, 2 * HEAD_DIM), BF16),
                   jax.ShapeDtypeStruct((two, B, N_KV_HEADS, HEAD_DIM, ngrp), BF16)],
        compiler_params=_params("arbitrary", "arbitrary"),
        name="compress",
    )(tok, wbig, w1, pos, w2)


def _cmp_attn_kernel(qt_ref, kc_ref, vct_ref, gates_ref, oc_ref, bias_ref, imp_ref, score_ref, left_ref,
                     *, n_slc, topk):
    k_head = pl.program_id(1)
    i = pl.program_id(2)
    tq = qt_ref.shape[1]
    ncmp = kc_ref.shape[0]
    t0 = i * tq
    any_ok = (t0 + lax.broadcasted_iota(jnp.int32, (1, tq), 1)) >= (CMP_LEN - 1)
    t0f = t0.astype(F32)
    q_aug = jnp.concatenate(
        [jnp.concatenate([qt_ref[g * HEAD_DIM:(g + 1) * HEAD_DIM, :] for g in range(GQA_GROUP)], axis=1),
         jnp.concatenate([_alibi_query_rows(_slope(k_head, g), t0f, HEAD_DIM, tq) for g in range(GQA_GROUP)],
                         axis=1).astype(BF16)], axis=0)

    def attend(nrows):
        t_col = t0 + lax.broadcasted_iota(jnp.int32, (nrows, tq), 1)
        cmp_end = CMP_STRIDE * lax.broadcasted_iota(jnp.int32, (nrows, tq), 0) + (CMP_LEN - 1)
        ok = t_col >= cmp_end
        kc = kc_ref[0:nrows, :]
        ones = jnp.where(lax.broadcasted_iota(jnp.int32, (16, nrows), 0) == 0, 1.0, 0.0).astype(BF16)
        v_aug = jnp.concatenate([vct_ref[:, 0:nrows], ones], axis=0)
        s_heads = [jnp.where(ok, _nn(kc, q_aug[:, g * tq:(g + 1) * tq]), NEG) for g in range(GQA_GROUP)]
        p_sum = None
        for g, s in enumerate(s_heads):
            e = jnp.exp2(s - jnp.max(s, axis=0, keepdims=True))
            o = _nn(v_aug, e.astype(BF16))
            inv = jnp.where(any_ok, 1.0 / o[HEAD_DIM:HEAD_DIM + 1], 0.0)
            p = e * inv
            p_sum = p if p_sum is None else p_sum + p
            gate = _sigmoid(gates_ref[g:g + 1, :])
            oc_ref[g * HEAD_DIM:(g + 1) * HEAD_DIM, :] = o[:HEAD_DIM] * (inv * gate)
        j_i = lax.broadcasted_iota(jnp.int32, (MAX_SLC_BLOCKS, nrows), 0)
        n_i = lax.broadcasted_iota(jnp.int32, (MAX_SLC_BLOCKS, nrows), 1)
        overlap = ((CMP_STRIDE * n_i <= SLC_LEN * j_i + (SLC_LEN - 1))
                   & (CMP_STRIDE * n_i + (CMP_LEN - 1) >= SLC_LEN * j_i)).astype(BF16)
        p_hi = p_sum.astype(BF16)
        r1 = p_sum - p_hi.astype(F32)
        p_mid = r1.astype(BF16)
        p_lo = (r1 - p_mid.astype(F32)).astype(BF16)
        imp_ref[...] = _nn(overlap, p_hi) + _nn(overlap, p_mid) + _nn(overlap, p_lo)

    n_chunks = ncmp // CMP_ROW_CHUNK
    last_chunk = jnp.minimum(((t0 + tq - CMP_LEN) // CMP_STRIDE) // CMP_ROW_CHUNK, n_chunks - 1)
    for c in range(n_chunks):
        pl.when(last_chunk == c)(functools.partial(attend, (c + 1) * CMP_ROW_CHUNK))
    score = imp_ref[...]

    j_row = lax.broadcasted_iota(jnp.int32, (MAX_SLC_BLOCKS, tq), 0)
    t_blk = t0 + lax.broadcasted_iota(jnp.int32, (MAX_SLC_BLOCKS, tq), 1)
    cur = t_blk // SLC_LEN
    valid = (SLC_LEN * j_row <= t_blk) & (j_row < n_slc)
    forced = (j_row == 0) | (j_row == cur) | (j_row == cur - 1)
    cand = valid & jnp.logical_not(forced)
    n_forced = 1 + (cur[0:1] >= 1).astype(jnp.int32) + (cur[0:1] >= 2).astype(jnp.int32)
    remaining = jnp.broadcast_to(topk - n_forced, (8, tq))
    score = jnp.where(cand, score, -jnp.inf)
    n_groups = tq // PICK_LANES
    for lb in range(n_groups):
        score_ref[lb] = score[:, lb * PICK_LANES:(lb + 1) * PICK_LANES]
        left_ref[lb] = remaining[:, lb * PICK_LANES:(lb + 1) * PICK_LANES]
    j_f = lax.broadcasted_iota(jnp.int32, (MAX_SLC_BLOCKS, LANES), 0).astype(F32)

    def pick_rounds(lb, rounds, nrow):
        halves = [score_ref[lb, 0:nrow, h * LANES:(h + 1) * LANES] for h in range(PICK_LANES // LANES)]
        left = [left_ref[lb, 0:1, h * LANES:(h + 1) * LANES] for h in range(PICK_LANES // LANES)]
        j_n = j_f[0:nrow]
        for r in rounds:
            for h, sc in enumerate(halves):
                m = jnp.max(sc, axis=0, keepdims=True)
                first = jnp.min(jnp.where(sc == m, j_n, float(MAX_SLC_BLOCKS)), axis=0, keepdims=True)
                first = jnp.where(left[h] > r, first, -1.0)
                halves[h] = jnp.where(j_n == first, -jnp.inf, sc)
        score_ref[lb, 0:nrow, :] = jnp.concatenate(halves, axis=1)

    base_rounds = topk - 3
    blocks_started = (t0 + tq) // SLC_LEN
    row_buckets = [r for r in (32, 64) if r < MAX_SLC_BLOCKS] + [MAX_SLC_BLOCKS]
    for lo, nrow in zip([0] + row_buckets[:-1], row_buckets):
        in_bucket = (blocks_started > lo) if nrow == MAX_SLC_BLOCKS else ((blocks_started > lo) & (blocks_started <= nrow))

        @pl.when(in_bucket)
        def _pick(nrow=nrow):
            def group(lb, carry):
                pick_rounds(lb, range(base_rounds), nrow)
                return carry

            lax.fori_loop(0, n_groups, group, 0)

    @pl.when(t0 < 2 * SLC_LEN)
    def _early_queries():
        pick_rounds(0, range(base_rounds, topk - 1), row_buckets[0])

    picked = jnp.concatenate([score_ref[lb] for lb in range(n_groups)], axis=1) == -jnp.inf
    bias_ref[...] = jnp.where((valid & forced) | (cand & picked), 0.0, NEG).astype(BF16)


def _cmp_attn_call(qt, cmp_kv, cmp_kv_t, gates, *, n_slc, topk):
    B, _, S = qt.shape
    ncmp = cmp_kv.shape[3]
    tq = min(CMP_Q_TILE, S)
    assert topk >= 3 and tq >= 2 * SLC_LEN
    kern = functools.partial(_cmp_attn_kernel, n_slc=n_slc, topk=topk)
    return pl.pallas_call(
        kern,
        grid=(B, N_KV_HEADS, S // tq),
        in_specs=[pl.BlockSpec((None, GROUP_WIDTH, tq), lambda b, k, i: (b, k, i)),
                  pl.BlockSpec((None, None, None, ncmp, 2 * HEAD_DIM), lambda b, k, i: (0, b, k, 0, 0)),
                  pl.BlockSpec((None, None, None, HEAD_DIM, ncmp), lambda b, k, i: (1, b, k, 0, 0)),
                  pl.BlockSpec((None, None, GATE_ROWS, tq), lambda b, k, i: (b, k, 0, i))],
        out_specs=[pl.BlockSpec((None, GROUP_WIDTH, tq), lambda b, k, i: (b, k, i)),
                   pl.BlockSpec((None, None, MAX_SLC_BLOCKS, tq), lambda b, k, i: (b, k, 0, i))],
        out_shape=[jax.ShapeDtypeStruct((B, ATTN_WIDTH, S), F32),
                   jax.ShapeDtypeStruct((B, N_KV_HEADS, MAX_SLC_BLOCKS, S), BF16)],
        scratch_shapes=[pltpu.VMEM((MAX_SLC_BLOCKS, tq), F32),
                        pltpu.VMEM((tq // PICK_LANES, MAX_SLC_BLOCKS, PICK_LANES), F32),
                        pltpu.VMEM((tq // PICK_LANES, 8, PICK_LANES), jnp.int32)],
        compiler_params=_params("arbitrary", "arbitrary", "arbitrary"),
        name="cmp_attn",
    )(qt, cmp_kv, cmp_kv_t, gates)


V_ROWS = HEAD_DIM + 16
SLC_HEAD_GROUPS = ((0, 1), (2, 3))
SKIP_MARGIN = 176.0


def _slc_attn_kernel(qt_ref, bias_ref, ks_ref, vt_ref, gates_ref, other_ref, attn_ref,
                     kaug_ref, vh_ref, qaug_ref, m_ref, acc_ref, sa_ref, sb_ref, ta_ref, tb_ref, ksq_ref):
    k_head = pl.program_id(1)
    i = pl.program_id(2)
    tq = qt_ref.shape[1]
    n_tiles = kaug_ref.shape[0]
    t0 = i * tq

    @pl.when(i == 0)
    def _build_keys():
        def fill(c, k_sq_max):
            r0 = pl.multiple_of(c * tq, tq)
            k_tile = _pick_head_lanes(ks_ref[pl.ds(r0, tq), :], k_head)
            kaug_ref[c, :, 0:HEAD_DIM] = k_tile
            k_f = k_tile.astype(F32)
            return jnp.maximum(k_sq_max, jnp.max(jnp.sum(k_f * k_f, axis=1, keepdims=True)))

        ksq_ref[0] = lax.fori_loop(0, n_tiles, fill, jnp.float32(0.0))
        for c in range(n_tiles):
            vh_ref[c, 0:HEAD_DIM, :] = _pick_head_rows(vt_ref[:, c * tq:(c + 1) * tq], k_head)

    @pl.when((i == 0) & (k_head == 0) & (pl.program_id(0) == 0))
    def _build_static():
        def fill(c, carry):
            r0 = pl.multiple_of(c * tq, tq)
            pos = r0 + lax.broadcasted_iota(jnp.int32, (tq, HEAD_DIM), 0)
            kaug_ref[c, :, HEAD_DIM:2 * HEAD_DIM] = _alibi_key_cols(pos).astype(BF16)
            blk = (r0 + lax.broadcasted_iota(jnp.int32, (tq, MAX_SLC_BLOCKS), 0)) // SLC_LEN
            hot = blk == lax.broadcasted_iota(jnp.int32, (tq, MAX_SLC_BLOCKS), 1)
            kaug_ref[c, :, 2 * HEAD_DIM:] = jnp.where(hot, 1.0, 0.0).astype(BF16)
            return carry

        lax.fori_loop(0, n_tiles, fill, 0)
        ones_row = lax.broadcasted_iota(jnp.int32, (V_ROWS - HEAD_DIM, tq), 0) == 0
        for c in range(n_tiles):
            vh_ref[c, HEAD_DIM:, :] = jnp.where(ones_row, 1.0, 0.0).astype(BF16)

    t0f = t0.astype(F32)
    bias = bias_ref[...]
    first_tile = []
    for g in range(GQA_GROUP):
        c_g = _slope(k_head, g)
        q_g = qt_ref[g * HEAD_DIM:(g + 1) * HEAD_DIM, :]
        aux = _alibi_query_rows(c_g, t0f, HEAD_DIM, tq)
        qaug_ref[0:HEAD_DIM, g * tq:(g + 1) * tq] = q_g
        qaug_ref[HEAD_DIM:2 * HEAD_DIM, g * tq:(g + 1) * tq] = aux.astype(BF16)
        qaug_ref[2 * HEAD_DIM:, g * tq:(g + 1) * tq] = bias
        q_f = q_g.astype(F32)
        qk = jnp.sqrt(jnp.max(jnp.sum(q_f * q_f, axis=0, keepdims=True), axis=1, keepdims=True) * ksq_ref[0])
        dist_needed = jnp.minimum((2.0 * qk[0, 0] + SKIP_MARGIN) / c_g, 1e6).astype(jnp.int32)
        first_tile.append(i - jnp.minimum(i, (dist_needed - 1) // tq + 1))
    m_ref[...] = jnp.full(m_ref.shape, NEG, F32)
    acc_ref[...] = jnp.zeros(acc_ref.shape, F32)

    s_bufs = (sa_ref, sb_ref)

    tmax_bufs = (ta_ref, tb_ref)

    def scores_head(k_tile, g, slot, causal):
        cols = slice(g * tq, (g + 1) * tq)
        s = _nn(k_tile, qaug_ref[:, cols])
        if causal:
            key = lax.broadcasted_iota(jnp.int32, (tq, tq), 0)
            qry = lax.broadcasted_iota(jnp.int32, (tq, tq), 1)
            s = jnp.where(key <= qry, s, NEG)
        s_bufs[slot][:, cols] = s
        tmax_bufs[slot][:, cols] = jnp.max(s, axis=0, keepdims=True)

    def scores(j, slot, causal, heads):
        k_tile = kaug_ref[j]
        for g in heads:
            scores_head(k_tile, g, slot, causal)

    def step(j, slot, prefetch, heads):
        v_tile = vh_ref[j]
        k_next = kaug_ref[j + 1] if prefetch is not None else None
        for g in heads:
            cols = slice(g * tq, (g + 1) * tq)
            if prefetch is not None:
                scores_head(k_next, g, 1 - slot, prefetch == 'causal')
            m_old = m_ref[:, cols]
            m_new = jnp.maximum(m_old, tmax_bufs[slot][:, cols])
            alpha = jnp.exp2(m_old - m_new)
            p = jnp.exp2(s_bufs[slot][:, cols] - m_new).astype(BF16)
            acc_ref[:, cols] = alpha * acc_ref[:, cols] + _nn(v_tile, p)
            m_ref[:, cols] = m_new

    def run(heads, lo):
        n_plain = i - lo

        @pl.when(n_plain == 0)
        def _first_is_causal():
            scores(i, 0, True, heads)

        @pl.when(n_plain > 0)
        def _first_is_plain():
            scores(lo, 0, False, heads)

        def trip(p, carry):
            for u in range(SLC_UNROLL):
                step(lo + SLC_UNROLL * p + u, u % 2, 'plain', heads)
            return carry

        n_trips = jnp.maximum(n_plain - 1, 0) // SLC_UNROLL
        lax.fori_loop(0, n_trips, trip, 0)
        done = SLC_UNROLL * n_trips
        for n_left in range(1, SLC_UNROLL + 2):
            @pl.when(n_plain - done + 1 == n_left)
            def _tail(n_left=n_left):
                for u in range(n_left):
                    prefetch = (None, 'causal')[u == n_left - 2] if u >= n_left - 2 else 'plain'
                    step(lo + done + u, u % 2, prefetch, heads)

    starts = [functools.reduce(jnp.minimum, [first_tile[g] for g in heads]) for heads in SLC_HEAD_GROUPS]
    same_start = functools.reduce(jnp.logical_and, [s == starts[0] for s in starts[1:]])

    @pl.when(same_start)
    def _one_pass():
        run(tuple(range(GQA_GROUP)), starts[0])

    @pl.when(jnp.logical_not(same_start))
    def _pass_per_group():
        for heads, lo in zip(SLC_HEAD_GROUPS, starts):
            run(heads, lo)

    o = acc_ref[0:HEAD_DIM, :] / acc_ref[HEAD_DIM:HEAD_DIM + 1, :]
    for g in range(GQA_GROUP):
        gate = _sigmoid(gates_ref[GQA_GROUP + g:GQA_GROUP + g + 1, :])
        rows = slice(g * HEAD_DIM, (g + 1) * HEAD_DIM)
        attn_ref[rows, :] = (other_ref[rows, :] + o[:, g * tq:(g + 1) * tq] * gate).astype(BF16)


def _slc_attn_call(qt, bias, kk, vt, gates, other):
    B, _, S = qt.shape
    tq = min(SLC_TILE, S)
    return pl.pallas_call(
        _slc_attn_kernel,
        grid=(B, N_KV_HEADS, S // tq),
        in_specs=[pl.BlockSpec((None, GROUP_WIDTH, tq), lambda b, k, i: (b, k, i)),
                  pl.BlockSpec((None, None, MAX_SLC_BLOCKS, tq), lambda b, k, i: (b, k, 0, i)),
                  pl.BlockSpec((None, S, KV_WIDTH), lambda b, k, i: (b, 0, 0)),
                  pl.BlockSpec((None, KV_WIDTH, S), lambda b, k, i: (b, 0, 0)),
                  pl.BlockSpec((None, None, GATE_ROWS, tq), lambda b, k, i: (b, k, 0, i)),
                  pl.BlockSpec((None, GROUP_WIDTH, tq), lambda b, k, i: (b, k, i))],
        out_specs=pl.BlockSpec((None, GROUP_WIDTH, tq), lambda b, k, i: (b, k, i)),
        out_shape=jax.ShapeDtypeStruct((B, ATTN_WIDTH, S), BF16),
        scratch_shapes=[pltpu.VMEM((S // tq, tq, 2 * LANES), BF16),
                        pltpu.VMEM((S // tq, V_ROWS, tq), BF16),
                        pltpu.VMEM((2 * LANES, GQA_GROUP * tq), BF16),
                        pltpu.VMEM((1, GQA_GROUP * tq), F32),
                        pltpu.VMEM((V_ROWS, GQA_GROUP * tq), F32),
                        pltpu.VMEM((tq, GQA_GROUP * tq), F32),
                        pltpu.VMEM((tq, GQA_GROUP * tq), F32),
                        pltpu.VMEM((1, GQA_GROUP * tq), F32),
                        pltpu.VMEM((1, GQA_GROUP * tq), F32),
                        pltpu.SMEM((1,), F32)],
        compiler_params=_params("arbitrary", "arbitrary", "arbitrary"),
        name="slc_attn",
    )(qt, bias, kk, vt, gates, other)


def _win_attn_kernel(qt_ref, kprev_ref, kcur_ref, vprev_ref, vcur_ref, gates_ref, other_ref, ow_ref):
    k_head = pl.program_id(1)
    i = pl.program_id(2)
    tq = qt_ref.shape[1] // 2
    key = lax.broadcasted_iota(jnp.int32, (tq, tq), 0)
    qry = lax.broadcasted_iota(jnp.int32, (tq, tq), 1)
    keeps = (qry < key, None, key <= qry)
    ones = jnp.where(lax.broadcasted_iota(jnp.int32, (16, tq), 0) == 0, 1.0, 0.0).astype(BF16)
    lane = lax.broadcasted_iota(jnp.int32, (tq, KV_WIDTH), 1)
    cols_in_tile = _alibi_key_cols(lax.broadcasted_iota(jnp.int32, (tq, KV_WIDTH), 0))
    high_lanes = jnp.where((lane == 0) | (lane == 2), 1.0, 0.0)
    flag_lane = jnp.where(lane == 8, 1.0, 0.0)
    k_augs, v_tiles = [], []
    for n, (k_ref, v_ref) in enumerate(((kprev_ref, vprev_ref), (kprev_ref, vprev_ref),
                                        (kcur_ref, vcur_ref), (kcur_ref, vcur_ref))):
        half = slice((n % 2) * tq, (n % 2 + 1) * tq)
        tile = 2 * i - 2 + n
        k_aux = (cols_in_tile + high_lanes * (tile * (tq // LANES)).astype(F32)
                 + flag_lane * jnp.where(tile < 0, 1.0, 0.0))
        k_augs.append(jnp.concatenate([k_ref[half, :], k_aux.astype(BF16)], axis=1))
        v_tiles.append(jnp.concatenate([v_ref[:, half], ones], axis=0))

    q2s = []
    for h in range(2):
        cols = slice(h * tq, (h + 1) * tq)
        t0f = ((2 * i + h) * tq).astype(F32)
        qs = jnp.concatenate([qt_ref[g * HEAD_DIM:(g + 1) * HEAD_DIM, cols] for g in range(GQA_GROUP)], axis=1)
        alibi = jnp.concatenate([_alibi_query_rows(_slope(k_head, g), t0f, KV_WIDTH, tq, row8=NEG)
                                 for g in range(GQA_GROUP)], axis=1).astype(BF16)
        q2s.append(jnp.concatenate([jnp.where(k_head == k, qs, jnp.zeros_like(qs)) for k in range(N_KV_HEADS)]
                                   + [alibi], axis=0))
    mask = lambda keep, s: s if keep is None else jnp.where(keep, s, NEG)
    scores = [[[mask(keep, _nn(k_aug, q2s[h][:, g * tq:(g + 1) * tq]))
                for k_aug, keep in zip(k_augs[h:h + 3], keeps)]
               for g in range(GQA_GROUP)] for h in range(2)]
    for h in range(2):
        for g in range(GQA_GROUP):
            m = functools.reduce(jnp.maximum, [jnp.max(s, axis=0, keepdims=True) for s in scores[h][g]])
            o2 = sum(_nn(v, jnp.exp2(s - m).astype(BF16)) for v, s in zip(v_tiles[h:h + 3], scores[h][g]))
            o = _pick_head_rows(o2[:KV_WIDTH], k_head) / o2[KV_WIDTH:KV_WIDTH + 1]
            gate = _sigmoid(gates_ref[2 * GQA_GROUP + g:2 * GQA_GROUP + g + 1, h * tq:(h + 1) * tq])
            rows, cols = slice(g * HEAD_DIM, (g + 1) * HEAD_DIM), slice(h * tq, (h + 1) * tq)
            ow_ref[rows, cols] = other_ref[rows, cols] + o * gate


def _win_attn_call(qt, kk, vt, gates, other):
    B, _, S = qt.shape
    tq = min(Q_TILE, S)
    assert WINDOW == 2 * tq and S % (2 * tq) == 0
    k_spec = lambda back: pl.BlockSpec((None, 2 * tq, KV_WIDTH), lambda b, k, i: (b, jnp.maximum(i - back, 0), 1))
    v_spec = lambda back: pl.BlockSpec((None, KV_WIDTH, 2 * tq), lambda b, k, i: (b, 1, jnp.maximum(i - back, 0)))
    return pl.pallas_call(
        _win_attn_kernel,
        grid=(B, N_KV_HEADS, S // (2 * tq)),
        in_specs=[pl.BlockSpec((None, GROUP_WIDTH, 2 * tq), lambda b, k, i: (b, k, i)),
                  k_spec(1), k_spec(0), v_spec(1), v_spec(0),
                  pl.BlockSpec((None, None, GATE_ROWS, 2 * tq), lambda b, k, i: (b, k, 0, i)),
                  pl.BlockSpec((None, GROUP_WIDTH, 2 * tq), lambda b, k, i: (b, k, i))],
        out_specs=pl.BlockSpec((None, GROUP_WIDTH, 2 * tq), lambda b, k, i: (b, k, i)),
        out_shape=jax.ShapeDtypeStruct((B, ATTN_WIDTH, S), F32),
        compiler_params=_params("arbitrary", "arbitrary", "arbitrary"),
        name="win_attn",
    )(qt, kk, kk, vt, vt, gates, other)


def _group_rms(a, gain):
    c, tm = a.shape
    a3 = a.reshape(c // HEAD_DIM, HEAD_DIM, tm)
    ms = jnp.mean(a3 * a3, axis=1, keepdims=True)
    return (a3 * lax.rsqrt(ms + LN_EPS) * gain).reshape(c, tm)


def _out_proj_kernel(attn_ref, mlp_ref, x_ref, g1_ref, og_ref, wo_ref, lng_ref, lnb_ref, o_ref, *, alpha):
    n_attn = ATTN_WIDTH // HEAD_DIM
    attn = _group_rms(attn_ref[...].astype(F32), og_ref[:n_attn])
    mlp = _group_rms(mlp_ref[...].astype(F32), og_ref[n_attn:])
    y_t = jnp.concatenate([attn, mlp], axis=0).astype(BF16)
    y = _tn(y_t, wo_ref[...])
    z = alpha * x_ref[...] + (1.0 + g1_ref[...]) * y
    o_ref[...] = _layer_norm_rows(z, lng_ref[...], lnb_ref[...])


def _out_proj_call(attn, mlp, x, g1, out_g, w_o, ln_g, ln_b, *, alpha):
    B, S, D = x.shape
    tm = min(TOK_TILE, S)
    cm = lambda width: pl.BlockSpec((None, width, tm), lambda b, i: (b, 0, i))
    row = pl.BlockSpec((1, D), lambda b, i: (0, 0))
    return pl.pallas_call(
        functools.partial(_out_proj_kernel, alpha=alpha),
        grid=(B, S // tm),
        in_specs=[cm(ATTN_WIDTH), cm(MLP_WIDTH),
                  pl.BlockSpec((None, tm, D), lambda b, i: (b, i, 0)),
                  pl.BlockSpec((None, 1, D), lambda b, i: (b, 0, 0)),
                  pl.BlockSpec(out_g.shape, lambda b, i: (0, 0, 0)),
                  pl.BlockSpec(w_o.shape, lambda b, i: (0, 0)),
                  row, row],
        out_specs=pl.BlockSpec((None, tm, D), lambda b, i: (b, i, 0)),
        out_shape=jax.ShapeDtypeStruct((B, S, D), F32),
        compiler_params=_params("arbitrary", "arbitrary"),
        name="out_proj",
    )(attn, mlp, x, g1, out_g, w_o, ln_g, ln_b)


def _ffn_kernel(x_ref, sc_ref, sh_ref, g2_ref, w1_ref, w3_ref, w2_ref, lng_ref, lnb_ref, o_ref, *, alpha, splits):
    x = x_ref[...]
    h = (x * (1.0 + sc_ref[...]) + sh_ref[...]).astype(BF16)
    f = None
    for lo, hi in splits:
        a = _nn(h, w1_ref[:, lo:hi])
        b = _nn(h, w3_ref[:, lo:hi])
        part = _nn((a * _sigmoid(a) * b).astype(BF16), w2_ref[lo:hi, :])
        f = part if f is None else f + part
    z = alpha * x + (1.0 + g2_ref[...]) * f
    o_ref[...] = _layer_norm_rows(z, lng_ref[...], lnb_ref[...])


def _ffn_call(x, sc, sh, g2, w1, w3, w2, ln_g, ln_b, *, alpha):
    B, S, D = x.shape
    d_ff = w1.shape[1]
    tm = min(TOK_TILE, S)
    half = (d_ff // 2 + 255) // 256 * 256
    splits = ((0, half), (half, d_ff))
    mod = pl.BlockSpec((None, 1, D), lambda b, i: (b, 0, 0))
    row = pl.BlockSpec((1, D), lambda b, i: (0, 0))
    resident = lambda shape: pl.BlockSpec(shape, lambda b, i: (0, 0), pipeline_mode=pl.Buffered(1))
    return pl.pallas_call(
        functools.partial(_ffn_kernel, alpha=alpha, splits=splits),
        grid=(B, S // tm),
        in_specs=[pl.BlockSpec((None, tm, D), lambda b, i: (b, i, 0)), mod, mod, mod,
                  resident(w1.shape), resident(w3.shape), resident(w2.shape), row, row],
        out_specs=pl.BlockSpec((None, tm, D), lambda b, i: (b, i, 0)),
        out_shape=jax.ShapeDtypeStruct((B, S, D), F32),
        compiler_params=_params("arbitrary", "arbitrary"),
        name="ffn",
    )(x, sc, sh, g2, w1, w3, w2, ln_g, ln_b)


def _in_proj_weights(w_in):
    sizes = (ATTN_WIDTH,) + (KV_WIDTH,) * 6 + (N_BRANCH * N_ATTN_HEADS, MLP_WIDTH, MLP_WIDTH)
    offs = [0]
    for s in sizes:
        offs.append(offs[-1] + s)
    col = lambda n: w_in[:, offs[n]:offs[n + 1]]
    q, kc, vc, ksl, vsl, kwn, vwn, gt, u, v = (col(n) for n in range(10))
    w_a = jnp.concatenate([ksl, kwn, kc, vc], axis=1).astype(BF16)
    D = w_in.shape[0]
    gt = gt.reshape(D, N_KV_HEADS, GQA_GROUP, N_BRANCH).transpose(1, 3, 2, 0)
    gt = gt.reshape(N_KV_HEADS, N_BRANCH * GQA_GROUP, D)
    gt = jnp.pad(gt, ((0, 0), (0, GATE_ROWS - N_BRANCH * GQA_GROUP), (0, 0)))
    gt = gt.reshape(N_KV_HEADS * GATE_ROWS, D)
    w_t = jnp.concatenate([u.T, v.T, q.T, vsl.T, vwn.T, gt], axis=0).astype(BF16)
    return w_a, w_t


def _compress_weights(cmp_w1):
    two, _, hid = cmp_w1.shape
    w = cmp_w1.reshape(two, 2, CMP_STRIDE, HEAD_DIM, hid)
    eye = jnp.eye(N_KV_HEADS, dtype=cmp_w1.dtype)
    big = jnp.einsum('thpdc,kj->tpkdjhc', w, eye)
    return big.reshape(two, CMP_STRIDE * N_KV_HEADS * HEAD_DIM, N_KV_HEADS * 2 * hid).astype(BF16)


def _hybrid_layer(x, mod, w_in, cmp_pos, cmp_w1, cmp_w2, vn_g, vn_b, w_s, b_s, out_g, w_o, ln1_g, ln1_b,
                  w1, w3, w2, ln2_g, ln2_b, *, alpha):
    B, S, D = x.shape
    assert S % Q_TILE == 0 and S % TOK_TILE == 0 and S // SLC_LEN <= MAX_SLC_BLOCKS
    sh1, sc1, g1, sh2, sc2, g2 = (mod[:, None, n * D:(n + 1) * D] for n in range(6))
    w_a, w_t = _in_proj_weights(w_in)
    kk, kvc, qt, vt, gates, mlp = _in_proj_call(
        x, sc1, sh1, w_a, w_t,
        vn_g.reshape(N_MLP_GROUPS, MLP_GROUP_DIM, 1), vn_b.reshape(N_MLP_GROUPS, MLP_GROUP_DIM, 1),
        jnp.swapaxes(w_s, 1, 2), b_s)
    ngrp = S // CMP_STRIDE
    tok = kvc.reshape(2, B, ngrp, CMP_STRIDE * KV_WIDTH)
    pos = jnp.broadcast_to(cmp_pos.reshape(2, 1, CMP_LEN * HEAD_DIM), (2, 8, CMP_LEN * HEAD_DIM))
    cmp_kv, cmp_kv_t = _compress_call(tok, _compress_weights(cmp_w1), cmp_w1.astype(BF16), pos, cmp_w2.astype(BF16))
    gates = gates.reshape(B, N_KV_HEADS, GATE_ROWS, S)
    n_slc = S // SLC_LEN
    oc, bias = _cmp_attn_call(qt, cmp_kv, cmp_kv_t, gates, n_slc=n_slc, topk=min(SLC_TOPK, n_slc))
    ocw = _win_attn_call(qt, kk, vt, gates, oc)
    attn = _slc_attn_call(qt, bias, kk, vt, gates, ocw)
    row = lambda a: a.reshape(1, D)
    x = _out_proj_call(attn, mlp, x, g1, out_g.reshape(-1, HEAD_DIM, 1), w_o.astype(BF16),
                       row(ln1_g), row(ln1_b), alpha=alpha)
    return _ffn_call(x, sc2, sh2, g2, w1.astype(BF16), w3.astype(BF16), w2.astype(BF16),
                     row(ln2_g), row(ln2_b), alpha=alpha)


def kernel(x, c, w_ada, b_ada, w_in, cmp_pos, cmp_w1, cmp_w2, vn_g, vn_b, w_s, b_s, out_g, w_o, ln1_g, ln1_b,
           w1, w3, w2, ln2_g, ln2_b):
    depth = w_ada.shape[0]
    alpha = (2.0 * depth) ** 0.25
    mod = _ada_call(c, w_ada, b_ada)
    for l in range(depth):
        x = _hybrid_layer(x, mod[l], w_in[l], cmp_pos[l], cmp_w1[l], cmp_w2[l], vn_g[l], vn_b[l], w_s[l], b_s[l],
                          out_g[l], w_o[l], ln1_g[l], ln1_b[l], w1[l], w3[l], w2[l], ln2_g[l], ln2_b[l], alpha=alpha)
    return x
```

```python
import functools
import math

import jax
import jax.numpy as jnp
from jax import lax
from jax.experimental import pallas as pl
from jax.experimental.pallas import tpu as pltpu

F32 = jnp.float32
BF16 = jnp.bfloat16

HEAD_DIM = 64
N_KV_HEADS = 2
GQA_GROUP = 4
N_ATTN_HEADS = N_KV_HEADS * GQA_GROUP
ATTN_WIDTH = N_ATTN_HEADS * HEAD_DIM
KV_WIDTH = N_KV_HEADS * HEAD_DIM
GROUP_WIDTH = GQA_GROUP * HEAD_DIM
N_BRANCH = 3
MLP_GROUP_DIM = 64
N_MLP_GROUPS = 8
MLP_WIDTH = N_MLP_GROUPS * MLP_GROUP_DIM
CMP_LEN = 32
CMP_STRIDE = 16
CMP_HIDDEN = 256
SLC_LEN = 64
SLC_TOPK = 16
WINDOW = 512
CHUNK = 128
LN_EPS = 1e-5
NEG = -1e30
FORCED_SCORE = 1e6

LOG2E = math.log2(math.e)
Q_SCALE = HEAD_DIM ** -0.5 * LOG2E

LANES = 128
MAX_SLC_BLOCKS = LANES
TOK_TILE = 512
Q_TILE = 256
SLC_TILE = 512
SLC_UNROLL = 4
CMP_Q_TILE = 512
PICK_LANES = 256
CMP_ROW_CHUNK = 128
VMEM_LIMIT = 56 * 1024 * 1024
GATE_ROWS = 16

ROW_V = MLP_WIDTH
ROW_Q = ROW_V + MLP_WIDTH
ROW_VSLC = ROW_Q + ATTN_WIDTH
ROW_VWIN = ROW_VSLC + KV_WIDTH
ROW_GATE = ROW_VWIN + KV_WIDTH
ROWS_T = ROW_GATE + N_KV_HEADS * GATE_ROWS
COLS_A = 4 * KV_WIDTH
KK_WIDTH = N_KV_HEADS * LANES + KV_WIDTH
KK_COLBLK_WIN = N_KV_HEADS


def _nt(a, b):
    return lax.dot_general(a, b, (((1,), (1,)), ((), ())), preferred_element_type=F32)


def _tn(a, b):
    return lax.dot_general(a, b, (((0,), (0,)), ((), ())), preferred_element_type=F32)


def _nn(a, b):
    return jnp.dot(a, b, preferred_element_type=F32)


def _gelu(x):
    return x * (0.5 * (1.0 + jnp.tanh(math.sqrt(2.0 / math.pi) * (x + 0.044715 * (x * x * x)))))


def _sigmoid(x):
    return 1.0 / (1.0 + jnp.exp(-x))


def _layer_norm_rows(z, g, b):
    mu = jnp.mean(z, axis=-1, keepdims=True)
    zc = z - mu
    var = jnp.mean(zc * zc, axis=-1, keepdims=True)
    return zc * lax.rsqrt(var + LN_EPS) * g + b


def _slope(kv_head, g):
    slope = lambda h: LOG2E * 2.0 ** (-8.0 * (h + 1) / N_ATTN_HEADS)
    val = jnp.asarray(slope(g), F32)
    for k in range(1, N_KV_HEADS):
        val = jnp.where(kv_head == k, jnp.asarray(slope(k * GQA_GROUP + g), F32), val)
    return val


def _pick_head_rows(x, kv_head):
    out = x[:HEAD_DIM]
    for k in range(1, N_KV_HEADS):
        out = jnp.where(kv_head == k, x[k * HEAD_DIM:(k + 1) * HEAD_DIM], out)
    return out


def _bf16_part(x):
    return x.astype(BF16).astype(F32)


def _alibi_key_cols(pos):
    lane = lax.broadcasted_iota(jnp.int32, pos.shape, 1)
    cols = jnp.where(lane % 2 == 0, pos // LANES, pos % LANES)
    return jnp.where(lane < 4, cols, jnp.where(lane < 8, 1, 0)).astype(F32)


def _alibi_query_rows(c, t0f, rows, tq, row8=0.0):
    shape = (8, tq)
    row = lax.broadcasted_iota(jnp.int32, shape, 0)
    c = jnp.full(shape, c, F32)
    out = jnp.zeros(shape, F32)
    for n, part in enumerate((_bf16_part(c), _bf16_part(c - _bf16_part(c)))):
        shift = part * t0f
        vals = {2 * n: LANES * part, 2 * n + 1: part, 4 + 2 * n: -_bf16_part(shift), 5 + 2 * n: -(shift - _bf16_part(shift))}
        for r, val in vals.items():
            out = jnp.where(row == r, val, out)
    tail = jnp.where(lax.broadcasted_iota(jnp.int32, (rows - 8, tq), 0) == 0, row8, 0.0)
    return jnp.concatenate([out, tail], axis=0)


def _params(*sem):
    return pltpu.CompilerParams(dimension_semantics=sem, vmem_limit_bytes=VMEM_LIMIT)


def _ada_kernel(c_ref, w_ref, b_ref, o_ref):
    c = c_ref[...]
    c_act = (c * _sigmoid(c)).astype(BF16)
    o_ref[0] = _nn(c_act, w_ref[0].astype(BF16)) + b_ref[0]


def _ada_call(c, w_ada, b_ada):
    L, D, D6 = w_ada.shape
    B = c.shape[0]
    return pl.pallas_call(
        _ada_kernel,
        grid=(L, D6 // D),
        in_specs=[pl.BlockSpec((B, D), lambda l, n: (0, 0)),
                  pl.BlockSpec((1, D, D), lambda l, n: (l, 0, n)),
                  pl.BlockSpec((1, 1, D), lambda l, n: (l, 0, n))],
        out_specs=pl.BlockSpec((1, B, D), lambda l, n: (l, 0, n)),
        out_shape=jax.ShapeDtypeStruct((L, B, D6), F32),
        compiler_params=_params("arbitrary", "arbitrary"),
        name="ada_mod",
    )(c, w_ada, b_ada.reshape(L, 1, D6))


def _in_proj_kernel(x_ref, sc_ref, sh_ref, wa_ref, wt_ref, vng_ref, vnb_ref, wst_ref, bs_ref,
                    kk_ref, kvc_ref, qt_ref, vt_ref, gates_ref, mlp_ref):
    tm = x_ref.shape[0]
    h = (x_ref[...] * (1.0 + sc_ref[...]) + sh_ref[...]).astype(BF16)
    uv = _nt(wt_ref[0:ROW_Q, :], h)
    t = _nt(wt_ref[ROW_Q:, :], h)
    qt_ref[...] = (t[:ATTN_WIDTH] * Q_SCALE).astype(BF16)
    vt_ref[...] = t[ROW_VSLC - ROW_Q:ROW_GATE - ROW_Q].astype(BF16)
    gates_ref[...] = t[ROW_GATE - ROW_Q:].reshape(N_KV_HEADS, GATE_ROWS, tm)
    a = _nn(h, wa_ref[...])
    for k in range(N_KV_HEADS):
        kk_ref[:, k * LANES:k * LANES + HEAD_DIM] = a[:, k * HEAD_DIM:(k + 1) * HEAD_DIM].astype(BF16)
        kk_ref[:, k * LANES + HEAD_DIM:(k + 1) * LANES] = jnp.zeros((tm, LANES - HEAD_DIM), BF16)
    kk_ref[:, N_KV_HEADS * LANES:] = a[:, KV_WIDTH:2 * KV_WIDTH].astype(BF16)
    kvc_ref[0] = a[:, 2 * KV_WIDTH:3 * KV_WIDTH].astype(BF16)
    kvc_ref[1] = a[:, 3 * KV_WIDTH:].astype(BF16)

    u = _gelu(uv[:ROW_V])
    v = _gelu(uv[ROW_V:]).reshape(N_MLP_GROUPS, MLP_GROUP_DIM, tm)
    mu = jnp.mean(v, axis=1, keepdims=True)
    vc = v - mu
    var = jnp.mean(vc * vc, axis=1, keepdims=True)
    vn = vc * lax.rsqrt(var + LN_EPS) * vng_ref[...] + vnb_ref[...]
    n_chunks = tm // CHUNK
    s_idx = lax.broadcasted_iota(jnp.int32, (CHUNK, CHUNK), 0)
    t_idx = lax.broadcasted_iota(jnp.int32, (CHUNK, CHUNK), 1)
    for g in range(N_MLP_GROUPS):
        vg = vn[g].astype(BF16)
        stack = jnp.concatenate([vg[:, c * CHUNK:(c + 1) * CHUNK] for c in range(n_chunks)], axis=0)
        w_t = jnp.where(s_idx <= t_idx, wst_ref[g], 0.0).astype(BF16)
        sv = _nn(stack, w_t)
        bias = bs_ref[g:g + 1, :]
        for c in range(n_chunks):
            mlp_ref[g * MLP_GROUP_DIM:(g + 1) * MLP_GROUP_DIM, c * CHUNK:(c + 1) * CHUNK] = (
                u[g * MLP_GROUP_DIM:(g + 1) * MLP_GROUP_DIM, c * CHUNK:(c + 1) * CHUNK]
                * (sv[c * MLP_GROUP_DIM:(c + 1) * MLP_GROUP_DIM] + bias)).astype(BF16)


def _in_proj_call(x, sc, sh, w_a, w_t, vn_g, vn_b, w_st, b_s):
    B, S, D = x.shape
    tm = min(TOK_TILE, S)
    const2 = lambda b, i: (0, 0)
    const3 = lambda b, i: (0, 0, 0)
    cm = lambda rows: pl.BlockSpec((None, rows, tm), lambda b, i: (b, 0, i))
    return pl.pallas_call(
        _in_proj_kernel,
        grid=(B, S // tm),
        in_specs=[pl.BlockSpec((None, tm, D), lambda b, i: (b, i, 0)),
                  pl.BlockSpec((None, 1, D), lambda b, i: (b, 0, 0)),
                  pl.BlockSpec((None, 1, D), lambda b, i: (b, 0, 0)),
                  pl.BlockSpec(w_a.shape, const2),
                  pl.BlockSpec(w_t.shape, const2),
                  pl.BlockSpec(vn_g.shape, const3),
                  pl.BlockSpec(vn_b.shape, const3),
                  pl.BlockSpec(w_st.shape, const3),
                  pl.BlockSpec(b_s.shape, const2)],
        out_specs=[pl.BlockSpec((None, tm, KK_WIDTH), lambda b, i: (b, i, 0)),
                   pl.BlockSpec((2, None, tm, KV_WIDTH), lambda b, i: (0, b, i, 0)),
                   cm(ATTN_WIDTH), cm(2 * KV_WIDTH),
                   pl.BlockSpec((None, N_KV_HEADS, GATE_ROWS, tm), lambda b, i: (b, 0, 0, i)), cm(MLP_WIDTH)],
        out_shape=[jax.ShapeDtypeStruct((B, S, KK_WIDTH), BF16),
                   jax.ShapeDtypeStruct((2, B, S, KV_WIDTH), BF16),
                   jax.ShapeDtypeStruct((B, ATTN_WIDTH, S), BF16),
                   jax.ShapeDtypeStruct((B, 2 * KV_WIDTH, S), BF16),
                   jax.ShapeDtypeStruct((B, N_KV_HEADS, GATE_ROWS, S), F32),
                   jax.ShapeDtypeStruct((B, MLP_WIDTH, S), BF16)],
        compiler_params=_params("arbitrary", "arbitrary"),
        name="in_proj",
    )(x, sc, sh, w_a, w_t, vn_g, vn_b, w_st, b_s)


def _compress_kernel(tok_ref, wbig_ref, w1_ref, pos_ref, w2_ref, out_ref, out_t_ref):
    ab = _nn(tok_ref[...], wbig_ref[...])
    bias = _nn(pos_ref[...].astype(BF16), w1_ref[...])[0:1]
    ngrp = ab.shape[0]
    for h in range(N_KV_HEADS):
        base = h * 2 * CMP_HIDDEN
        first = ab[:, base:base + CMP_HIDDEN]
        second = ab[:, base + CMP_HIDDEN:base + 2 * CMP_HIDDEN]
        hid = _gelu(first + pltpu.roll(second, ngrp - 1, 0) + bias)
        c = _nn(hid.astype(BF16), w2_ref[...])
        block_end = CMP_STRIDE * lax.broadcasted_iota(jnp.int32, (ngrp, HEAD_DIM), 0) + (CMP_LEN - 1)
        out_ref[h, :, 0:HEAD_DIM] = c.astype(BF16)
        out_ref[h, :, HEAD_DIM:] = _alibi_key_cols(block_end).astype(BF16)
        out_t_ref[h] = c.T.astype(BF16)


def _compress_call(tok, wbig, w1, pos, w2):
    two, B, ngrp, flat = tok.shape
    return pl.pallas_call(
        _compress_kernel,
        grid=(two, B),
        in_specs=[pl.BlockSpec((None, None, ngrp, flat), lambda t, b: (t, b, 0, 0)),
                  pl.BlockSpec((None,) + wbig.shape[1:], lambda t, b: (t, 0, 0)),
                  pl.BlockSpec((None,) + w1.shape[1:], lambda t, b: (t, 0, 0)),
                  pl.BlockSpec((None,) + pos.shape[1:], lambda t, b: (t, 0, 0)),
                  pl.BlockSpec((None,) + w2.shape[1:], lambda t, b: (t, 0, 0))],
        out_specs=[pl.BlockSpec((None, None, N_KV_HEADS, ngrp, 2 * HEAD_DIM), lambda t, b: (t, b, 0, 0, 0)),
                   pl.BlockSpec((None, None, N_KV_HEADS, HEAD_DIM, ngrp), lambda t, b: (t, b, 0, 0, 0))],
        out_shape=[jax.ShapeDtypeStruct((two, B, N_KV_HEADS, ngrp, 2 * HEAD_DIM), BF16),
                   jax.ShapeDtypeStruct((two, B, N_KV_HEADS, HEAD_DIM, ngrp), BF16)],
        compiler_params=_params("arbitrary", "arbitrary"),
        name="compress",
    )(tok, wbig, w1, pos, w2)


def _cmp_attn_kernel(qt_ref, kc_ref, vct_ref, gates_ref, oc_ref, bias_ref, imp_ref, score_ref, left_ref,
                     *, n_slc, topk):
    k_head = pl.program_id(1)
    i = pl.program_id(2)
    tq = qt_ref.shape[1]
    ncmp = kc_ref.shape[0]
    t0 = i * tq
    any_ok = (t0 + lax.broadcasted_iota(jnp.int32, (1, tq), 1)) >= (CMP_LEN - 1)
    t0f = t0.astype(F32)
    q_aug = jnp.concatenate(
        [jnp.concatenate([qt_ref[g * HEAD_DIM:(g + 1) * HEAD_DIM, :] for g in range(GQA_GROUP)], axis=1),
         jnp.concatenate([_alibi_query_rows(_slope(k_head, g), t0f, HEAD_DIM, tq) for g in range(GQA_GROUP)],
                         axis=1).astype(BF16)], axis=0)

    def attend(nrows):
        t_col = t0 + lax.broadcasted_iota(jnp.int32, (nrows, tq), 1)
        cmp_end = CMP_STRIDE * lax.broadcasted_iota(jnp.int32, (nrows, tq), 0) + (CMP_LEN - 1)
        ok = t_col >= cmp_end
        kc = kc_ref[0:nrows, :]
        ones = jnp.where(lax.broadcasted_iota(jnp.int32, (16, nrows), 0) == 0, 1.0, 0.0).astype(BF16)
        v_aug = jnp.concatenate([vct_ref[:, 0:nrows], ones], axis=0)
        s_heads = [jnp.where(ok, _nn(kc, q_aug[:, g * tq:(g + 1) * tq]), NEG) for g in range(GQA_GROUP)]
        p_sum = None
        for g, s in enumerate(s_heads):
            e = jnp.exp2(s - jnp.max(s, axis=0, keepdims=True))
            o = _nn(v_aug, e.astype(BF16))
            inv = jnp.where(any_ok, 1.0 / o[HEAD_DIM:HEAD_DIM + 1], 0.0)
            p = e * inv
            p_sum = p if p_sum is None else p_sum + p
            gate = _sigmoid(gates_ref[g:g + 1, :])
            oc_ref[g * HEAD_DIM:(g + 1) * HEAD_DIM, :] = o[:HEAD_DIM] * (inv * gate)
        j_i = lax.broadcasted_iota(jnp.int32, (MAX_SLC_BLOCKS, nrows), 0)
        n_i = lax.broadcasted_iota(jnp.int32, (MAX_SLC_BLOCKS, nrows), 1)
        overlap = ((CMP_STRIDE * n_i <= SLC_LEN * j_i + (SLC_LEN - 1))
                   & (CMP_STRIDE * n_i + (CMP_LEN - 1) >= SLC_LEN * j_i)).astype(BF16)
        p_hi = p_sum.astype(BF16)
        r1 = p_sum - p_hi.astype(F32)
        p_mid = r1.astype(BF16)
        p_lo = (r1 - p_mid.astype(F32)).astype(BF16)
        imp_ref[...] = _nn(overlap, p_hi) + _nn(overlap, p_mid) + _nn(overlap, p_lo)

    n_chunks = ncmp // CMP_ROW_CHUNK
    last_chunk = jnp.minimum(((t0 + tq - CMP_LEN) // CMP_STRIDE) // CMP_ROW_CHUNK, n_chunks - 1)
    for c in range(n_chunks):
        pl.when(last_chunk == c)(functools.partial(attend, (c + 1) * CMP_ROW_CHUNK))
    score = imp_ref[...]

    j_row = lax.broadcasted_iota(jnp.int32, (MAX_SLC_BLOCKS, tq), 0)
    t_blk = t0 + lax.broadcasted_iota(jnp.int32, (MAX_SLC_BLOCKS, tq), 1)
    cur = t_blk // SLC_LEN
    valid = (SLC_LEN * j_row <= t_blk) & (j_row < n_slc)
    forced = (j_row == 0) | (j_row == cur) | (j_row == cur - 1)
    cand = valid & jnp.logical_not(forced)
    n_forced = 1 + (cur[0:1] >= 1).astype(jnp.int32) + (cur[0:1] >= 2).astype(jnp.int32)
    remaining = jnp.broadcast_to(topk - n_forced, (8, tq))
    score = jnp.where(cand, score, -jnp.inf)
    n_groups = tq // PICK_LANES
    for lb in range(n_groups):
        score_ref[lb] = score[:, lb * PICK_LANES:(lb + 1) * PICK_LANES]
        left_ref[lb] = remaining[:, lb * PICK_LANES:(lb + 1) * PICK_LANES]
    j_f = lax.broadcasted_iota(jnp.int32, (MAX_SLC_BLOCKS, LANES), 0).astype(F32)

    def pick_rounds(lb, rounds, nrow):
        halves = [score_ref[lb, 0:nrow, h * LANES:(h + 1) * LANES] for h in range(PICK_LANES // LANES)]
        left = [left_ref[lb, 0:1, h * LANES:(h + 1) * LANES] for h in range(PICK_LANES // LANES)]
        j_n = j_f[0:nrow]
        for r in rounds:
            for h, sc in enumerate(halves):
                m = jnp.max(sc, axis=0, keepdims=True)
                first = jnp.min(jnp.where(sc == m, j_n, float(MAX_SLC_BLOCKS)), axis=0, keepdims=True)
                first = jnp.where(left[h] > r, first, -1.0)
                halves[h] = jnp.where(j_n == first, -jnp.inf, sc)
        score_ref[lb, 0:nrow, :] = jnp.concatenate(halves, axis=1)

    base_rounds = topk - 3
    blocks_started = (t0 + tq) // SLC_LEN
    row_buckets = [r for r in (32, 64) if r < MAX_SLC_BLOCKS] + [MAX_SLC_BLOCKS]
    for lo, nrow in zip([0] + row_buckets[:-1], row_buckets):
        in_bucket = (blocks_started > lo) if nrow == MAX_SLC_BLOCKS else ((blocks_started > lo) & (blocks_started <= nrow))

        @pl.when(in_bucket)
        def _pick(nrow=nrow):
            def group(lb, carry):
                pick_rounds(lb, range(base_rounds), nrow)
                return carry

            lax.fori_loop(0, n_groups, group, 0)

    @pl.when(t0 < 2 * SLC_LEN)
    def _early_queries():
        pick_rounds(0, range(base_rounds, topk - 1), row_buckets[0])

    picked = jnp.concatenate([score_ref[lb] for lb in range(n_groups)], axis=1) == -jnp.inf
    bias_ref[...] = jnp.where((valid & forced) | (cand & picked), 0.0, NEG).astype(BF16)


def _cmp_attn_call(qt, cmp_kv, cmp_kv_t, gates, *, n_slc, topk):
    B, _, S = qt.shape
    ncmp = cmp_kv.shape[3]
    tq = min(CMP_Q_TILE, S)
    assert topk >= 3 and tq >= 2 * SLC_LEN
    kern = functools.partial(_cmp_attn_kernel, n_slc=n_slc, topk=topk)
    return pl.pallas_call(
        kern,
        grid=(B, N_KV_HEADS, S // tq),
        in_specs=[pl.BlockSpec((None, GROUP_WIDTH, tq), lambda b, k, i: (b, k, i)),
                  pl.BlockSpec((None, None, None, ncmp, 2 * HEAD_DIM), lambda b, k, i: (0, b, k, 0, 0)),
                  pl.BlockSpec((None, None, None, HEAD_DIM, ncmp), lambda b, k, i: (1, b, k, 0, 0)),
                  pl.BlockSpec((None, None, GATE_ROWS, tq), lambda b, k, i: (b, k, 0, i))],
        out_specs=[pl.BlockSpec((None, GROUP_WIDTH, tq), lambda b, k, i: (b, k, i)),
                   pl.BlockSpec((None, None, MAX_SLC_BLOCKS, tq), lambda b, k, i: (b, k, 0, i))],
        out_shape=[jax.ShapeDtypeStruct((B, ATTN_WIDTH, S), F32),
                   jax.ShapeDtypeStruct((B, N_KV_HEADS, MAX_SLC_BLOCKS, S), BF16)],
        scratch_shapes=[pltpu.VMEM((MAX_SLC_BLOCKS, tq), F32),
                        pltpu.VMEM((tq // PICK_LANES, MAX_SLC_BLOCKS, PICK_LANES), F32),
                        pltpu.VMEM((tq // PICK_LANES, 8, PICK_LANES), jnp.int32)],
        compiler_params=_params("arbitrary", "arbitrary", "arbitrary"),
        name="cmp_attn",
    )(qt, cmp_kv, cmp_kv_t, gates)


V_ROWS = HEAD_DIM + 16
SLC_HEAD_GROUPS = ((0, 1), (2, 3))
SKIP_MARGIN = 176.0


def _slc_attn_kernel(qt_ref, bias_ref, ks_ref, vt_ref, gates_ref, other_ref, attn_ref,
                     kaug_ref, vh_ref, qaug_ref, m_ref, acc_ref, sa_ref, sb_ref, ta_ref, tb_ref, ksq_ref):
    k_head = pl.program_id(1)
    i = pl.program_id(2)
    tq = qt_ref.shape[1]
    n_tiles = kaug_ref.shape[0]
    t0 = i * tq

    @pl.when(i == 0)
    def _build_keys():
        def fill(c, k_sq_max):
            r0 = pl.multiple_of(c * tq, tq)
            k_tile = ks_ref[pl.ds(r0, tq), 0:HEAD_DIM]
            kaug_ref[c, :, 0:HEAD_DIM] = k_tile
            k_f = k_tile.astype(F32)
            return jnp.maximum(k_sq_max, jnp.max(jnp.sum(k_f * k_f, axis=1, keepdims=True)))

        ksq_ref[0] = lax.fori_loop(0, n_tiles, fill, jnp.float32(0.0))
        for c in range(n_tiles):
            vh_ref[c, 0:HEAD_DIM, :] = _pick_head_rows(vt_ref[:, c * tq:(c + 1) * tq], k_head)

    @pl.when((i == 0) & (k_head == 0) & (pl.program_id(0) == 0))
    def _build_static():
        def fill(c, carry):
            r0 = pl.multiple_of(c * tq, tq)
            pos = r0 + lax.broadcasted_iota(jnp.int32, (tq, HEAD_DIM), 0)
            kaug_ref[c, :, HEAD_DIM:2 * HEAD_DIM] = _alibi_key_cols(pos).astype(BF16)
            blk = (r0 + lax.broadcasted_iota(jnp.int32, (tq, MAX_SLC_BLOCKS), 0)) // SLC_LEN
            hot = blk == lax.broadcasted_iota(jnp.int32, (tq, MAX_SLC_BLOCKS), 1)
            kaug_ref[c, :, 2 * HEAD_DIM:] = jnp.where(hot, 1.0, 0.0).astype(BF16)
            return carry

        lax.fori_loop(0, n_tiles, fill, 0)
        ones_row = lax.broadcasted_iota(jnp.int32, (V_ROWS - HEAD_DIM, tq), 0) == 0
        for c in range(n_tiles):
            vh_ref[c, HEAD_DIM:, :] = jnp.where(ones_row, 1.0, 0.0).astype(BF16)

    t0f = t0.astype(F32)
    bias = bias_ref[...]
    first_tile = []
    for g in range(GQA_GROUP):
        c_g = _slope(k_head, g)
        q_g = qt_ref[g * HEAD_DIM:(g + 1) * HEAD_DIM, :]
        aux = _alibi_query_rows(c_g, t0f, HEAD_DIM, tq)
        qaug_ref[0:HEAD_DIM, g * tq:(g + 1) * tq] = q_g
        qaug_ref[HEAD_DIM:2 * HEAD_DIM, g * tq:(g + 1) * tq] = aux.astype(BF16)
        qaug_ref[2 * HEAD_DIM:, g * tq:(g + 1) * tq] = bias
        q_f = q_g.astype(F32)
        qk = jnp.sqrt(jnp.max(jnp.sum(q_f * q_f, axis=0, keepdims=True), axis=1, keepdims=True) * ksq_ref[0])
        dist_needed = jnp.minimum((2.0 * qk[0, 0] + SKIP_MARGIN) / c_g, 1e6).astype(jnp.int32)
        first_tile.append(i - jnp.minimum(i, (dist_needed - 1) // tq + 1))
    m_ref[...] = jnp.full(m_ref.shape, NEG, F32)
    acc_ref[...] = jnp.zeros(acc_ref.shape, F32)

    s_bufs = (sa_ref, sb_ref)

    tmax_bufs = (ta_ref, tb_ref)

    def scores_head(k_tile, g, slot, causal):
        cols = slice(g * tq, (g + 1) * tq)
        s = _nn(k_tile, qaug_ref[:, cols])
        if causal:
            key = lax.broadcasted_iota(jnp.int32, (tq, tq), 0)
            qry = lax.broadcasted_iota(jnp.int32, (tq, tq), 1)
            s = jnp.where(key <= qry, s, NEG)
        s_bufs[slot][:, cols] = s
        tmax_bufs[slot][:, cols] = jnp.max(s, axis=0, keepdims=True)

    def scores(j, slot, causal, heads):
        k_tile = kaug_ref[j]
        for g in heads:
            scores_head(k_tile, g, slot, causal)

    def step(j, slot, prefetch, heads):
        v_tile = vh_ref[j]
        k_next = kaug_ref[j + 1] if prefetch is not None else None
        for g in heads:
            cols = slice(g * tq, (g + 1) * tq)
            if prefetch is not None:
                scores_head(k_next, g, 1 - slot, prefetch == 'causal')
            m_old = m_ref[:, cols]
            m_new = jnp.maximum(m_old, tmax_bufs[slot][:, cols])
            alpha = jnp.exp2(m_old - m_new)
            p = jnp.exp2(s_bufs[slot][:, cols] - m_new).astype(BF16)
            acc_ref[:, cols] = alpha * acc_ref[:, cols] + _nn(v_tile, p)
            m_ref[:, cols] = m_new

    def run(heads, lo):
        n_plain = i - lo

        @pl.when(n_plain == 0)
        def _first_is_causal():
            scores(i, 0, True, heads)

        @pl.when(n_plain > 0)
        def _first_is_plain():
            scores(lo, 0, False, heads)

        def trip(p, carry):
            for u in range(SLC_UNROLL):
                step(lo + SLC_UNROLL * p + u, u % 2, 'plain', heads)
            return carry

        n_trips = jnp.maximum(n_plain - 1, 0) // SLC_UNROLL
        lax.fori_loop(0, n_trips, trip, 0)
        done = SLC_UNROLL * n_trips
        for n_left in range(1, SLC_UNROLL + 2):
            @pl.when(n_plain - done + 1 == n_left)
            def _tail(n_left=n_left):
                for u in range(n_left):
                    prefetch = (None, 'causal')[u == n_left - 2] if u >= n_left - 2 else 'plain'
                    step(lo + done + u, u % 2, prefetch, heads)

    starts = [functools.reduce(jnp.minimum, [first_tile[g] for g in heads]) for heads in SLC_HEAD_GROUPS]
    same_start = functools.reduce(jnp.logical_and, [s == starts[0] for s in starts[1:]])

    @pl.when(same_start)
    def _one_pass():
        run(tuple(range(GQA_GROUP)), starts[0])

    @pl.when(jnp.logical_not(same_start))
    def _pass_per_group():
        for heads, lo in zip(SLC_HEAD_GROUPS, starts):
            run(heads, lo)

    o = acc_ref[0:HEAD_DIM, :] / acc_ref[HEAD_DIM:HEAD_DIM + 1, :]
    for g in range(GQA_GROUP):
        gate = _sigmoid(gates_ref[GQA_GROUP + g:GQA_GROUP + g + 1, :])
        rows = slice(g * HEAD_DIM, (g + 1) * HEAD_DIM)
        attn_ref[rows, :] = (other_ref[rows, :] + o[:, g * tq:(g + 1) * tq] * gate).astype(BF16)


def _slc_attn_call(qt, bias, kk, vt, gates, other):
    B, _, S = qt.shape
    tq = min(SLC_TILE, S)
    return pl.pallas_call(
        _slc_attn_kernel,
        grid=(B, N_KV_HEADS, S // tq),
        in_specs=[pl.BlockSpec((None, GROUP_WIDTH, tq), lambda b, k, i: (b, k, i)),
                  pl.BlockSpec((None, None, MAX_SLC_BLOCKS, tq), lambda b, k, i: (b, k, 0, i)),
                  pl.BlockSpec((None, S, LANES), lambda b, k, i: (b, 0, k)),
                  pl.BlockSpec((None, KV_WIDTH, S), lambda b, k, i: (b, 0, 0)),
                  pl.BlockSpec((None, None, GATE_ROWS, tq), lambda b, k, i: (b, k, 0, i)),
                  pl.BlockSpec((None, GROUP_WIDTH, tq), lambda b, k, i: (b, k, i))],
        out_specs=pl.BlockSpec((None, GROUP_WIDTH, tq), lambda b, k, i: (b, k, i)),
        out_shape=jax.ShapeDtypeStruct((B, ATTN_WIDTH, S), BF16),
        scratch_shapes=[pltpu.VMEM((S // tq, tq, 2 * LANES), BF16),
                        pltpu.VMEM((S // tq, V_ROWS, tq), BF16),
                        pltpu.VMEM((2 * LANES, GQA_GROUP * tq), BF16),
                        pltpu.VMEM((1, GQA_GROUP * tq), F32),
                        pltpu.VMEM((V_ROWS, GQA_GROUP * tq), F32),
                        pltpu.VMEM((tq, GQA_GROUP * tq), F32),
                        pltpu.VMEM((tq, GQA_GROUP * tq), F32),
                        pltpu.VMEM((1, GQA_GROUP * tq), F32),
                        pltpu.VMEM((1, GQA_GROUP * tq), F32),
                        pltpu.SMEM((1,), F32)],
        compiler_params=_params("arbitrary", "arbitrary", "arbitrary"),
        name="slc_attn",
    )(qt, bias, kk, vt, gates, other)


def _win_attn_kernel(qt_ref, kprev_ref, kcur_ref, vprev_ref, vcur_ref, gates_ref, other_ref, ow_ref):
    k_head = pl.program_id(1)
    i = pl.program_id(2)
    tq = qt_ref.shape[1] // 2
    key = lax.broadcasted_iota(jnp.int32, (tq, tq), 0)
    qry = lax.broadcasted_iota(jnp.int32, (tq, tq), 1)
    keeps = (qry < key, None, key <= qry)
    ones = jnp.where(lax.broadcasted_iota(jnp.int32, (16, tq), 0) == 0, 1.0, 0.0).astype(BF16)
    lane = lax.broadcasted_iota(jnp.int32, (tq, KV_WIDTH), 1)
    cols_in_tile = _alibi_key_cols(lax.broadcasted_iota(jnp.int32, (tq, KV_WIDTH), 0))
    high_lanes = jnp.where((lane == 0) | (lane == 2), 1.0, 0.0)
    flag_lane = jnp.where(lane == 8, 1.0, 0.0)
    k_augs, v_tiles = [], []
    for n, (k_ref, v_ref) in enumerate(((kprev_ref, vprev_ref), (kprev_ref, vprev_ref),
                                        (kcur_ref, vcur_ref), (kcur_ref, vcur_ref))):
        half = slice((n % 2) * tq, (n % 2 + 1) * tq)
        tile = 2 * i - 2 + n
        k_aux = (cols_in_tile + high_lanes * (tile * (tq // LANES)).astype(F32)
                 + flag_lane * jnp.where(tile < 0, 1.0, 0.0))
        k_augs.append(jnp.concatenate([k_ref[half, :], k_aux.astype(BF16)], axis=1))
        v_tiles.append(jnp.concatenate([v_ref[:, half], ones], axis=0))

    q2s = []
    for h in range(2):
        cols = slice(h * tq, (h + 1) * tq)
        t0f = ((2 * i + h) * tq).astype(F32)
        qs = jnp.concatenate([qt_ref[g * HEAD_DIM:(g + 1) * HEAD_DIM, cols] for g in range(GQA_GROUP)], axis=1)
        alibi = jnp.concatenate([_alibi_query_rows(_slope(k_head, g), t0f, KV_WIDTH, tq, row8=NEG)
                                 for g in range(GQA_GROUP)], axis=1).astype(BF16)
        q2s.append(jnp.concatenate([jnp.where(k_head == k, qs, jnp.zeros_like(qs)) for k in range(N_KV_HEADS)]
                                   + [alibi], axis=0))
    mask = lambda keep, s: s if keep is None else jnp.where(keep, s, NEG)
    scores = [[[mask(keep, _nn(k_aug, q2s[h][:, g * tq:(g + 1) * tq]))
                for k_aug, keep in zip(k_augs[h:h + 3], keeps)]
               for g in range(GQA_GROUP)] for h in range(2)]
    for h in range(2):
        for g in range(GQA_GROUP):
            m = functools.reduce(jnp.maximum, [jnp.max(s, axis=0, keepdims=True) for s in scores[h][g]])
            o2 = sum(_nn(v, jnp.exp2(s - m).astype(BF16)) for v, s in zip(v_tiles[h:h + 3], scores[h][g]))
            o = _pick_head_rows(o2[:KV_WIDTH], k_head) / o2[KV_WIDTH:KV_WIDTH + 1]
            gate = _sigmoid(gates_ref[2 * GQA_GROUP + g:2 * GQA_GROUP + g + 1, h * tq:(h + 1) * tq])
            rows, cols = slice(g * HEAD_DIM, (g + 1) * HEAD_DIM), slice(h * tq, (h + 1) * tq)
            ow_ref[rows, cols] = other_ref[rows, cols] + o * gate


def _win_attn_call(qt, kk, vt, gates, other):
    B, _, S = qt.shape
    tq = min(Q_TILE, S)
    assert WINDOW == 2 * tq and S % (2 * tq) == 0
    k_spec = lambda back: pl.BlockSpec((None, 2 * tq, KV_WIDTH),
                                       lambda b, k, i: (b, jnp.maximum(i - back, 0), KK_COLBLK_WIN))
    v_spec = lambda back: pl.BlockSpec((None, KV_WIDTH, 2 * tq), lambda b, k, i: (b, 1, jnp.maximum(i - back, 0)))
    return pl.pallas_call(
        _win_attn_kernel,
        grid=(B, N_KV_HEADS, S // (2 * tq)),
        in_specs=[pl.BlockSpec((None, GROUP_WIDTH, 2 * tq), lambda b, k, i: (b, k, i)),
                  k_spec(1), k_spec(0), v_spec(1), v_spec(0),
                  pl.BlockSpec((None, None, GATE_ROWS, 2 * tq), lambda b, k, i: (b, k, 0, i)),
                  pl.BlockSpec((None, GROUP_WIDTH, 2 * tq), lambda b, k, i: (b, k, i))],
        out_specs=pl.BlockSpec((None, GROUP_WIDTH, 2 * tq), lambda b, k, i: (b, k, i)),
        out_shape=jax.ShapeDtypeStruct((B, ATTN_WIDTH, S), F32),
        compiler_params=_params("arbitrary", "arbitrary", "arbitrary"),
        name="win_attn",
    )(qt, kk, kk, vt, vt, gates, other)


def _group_rms(a, gain):
    c, tm = a.shape
    a3 = a.reshape(c // HEAD_DIM, HEAD_DIM, tm)
    ms = jnp.mean(a3 * a3, axis=1, keepdims=True)
    return (a3 * lax.rsqrt(ms + LN_EPS) * gain).reshape(c, tm)


def _out_proj_kernel(attn_ref, mlp_ref, x_ref, g1_ref, og_ref, wo_ref, lng_ref, lnb_ref, o_ref, *, alpha):
    n_attn = ATTN_WIDTH // HEAD_DIM
    attn = _group_rms(attn_ref[...].astype(F32), og_ref[:n_attn])
    mlp = _group_rms(mlp_ref[...].astype(F32), og_ref[n_attn:])
    y_t = jnp.concatenate([attn, mlp], axis=0).astype(BF16)
    y = _tn(y_t, wo_ref[...])
    z = alpha * x_ref[...] + (1.0 + g1_ref[...]) * y
    o_ref[...] = _layer_norm_rows(z, lng_ref[...], lnb_ref[...])


def _out_proj_call(attn, mlp, x, g1, out_g, w_o, ln_g, ln_b, *, alpha):
    B, S, D = x.shape
    tm = min(TOK_TILE, S)
    cm = lambda width: pl.BlockSpec((None, width, tm), lambda b, i: (b, 0, i))
    row = pl.BlockSpec((1, D), lambda b, i: (0, 0))
    return pl.pallas_call(
        functools.partial(_out_proj_kernel, alpha=alpha),
        grid=(B, S // tm),
        in_specs=[cm(ATTN_WIDTH), cm(MLP_WIDTH),
                  pl.BlockSpec((None, tm, D), lambda b, i: (b, i, 0)),
                  pl.BlockSpec((None, 1, D), lambda b, i: (b, 0, 0)),
                  pl.BlockSpec(out_g.shape, lambda b, i: (0, 0, 0)),
                  pl.BlockSpec(w_o.shape, lambda b, i: (0, 0)),
                  row, row],
        out_specs=pl.BlockSpec((None, tm, D), lambda b, i: (b, i, 0)),
        out_shape=jax.ShapeDtypeStruct((B, S, D), F32),
        compiler_params=_params("arbitrary", "arbitrary"),
        name="out_proj",
    )(attn, mlp, x, g1, out_g, w_o, ln_g, ln_b)


def _ffn_kernel(x_ref, sc_ref, sh_ref, g2_ref, w1_ref, w3_ref, w2_ref, lng_ref, lnb_ref, o_ref, *, alpha, splits):
    x = x_ref[...]
    h = (x * (1.0 + sc_ref[...]) + sh_ref[...]).astype(BF16)
    f = None
    for lo, hi in splits:
        a = _nn(h, w1_ref[:, lo:hi])
        b = _nn(h, w3_ref[:, lo:hi])
        part = _nn((a * _sigmoid(a) * b).astype(BF16), w2_ref[lo:hi, :])
        f = part if f is None else f + part
    z = alpha * x + (1.0 + g2_ref[...]) * f
    o_ref[...] = _layer_norm_rows(z, lng_ref[...], lnb_ref[...])


def _ffn_call(x, sc, sh, g2, w1, w3, w2, ln_g, ln_b, *, alpha):
    B, S, D = x.shape
    d_ff = w1.shape[1]
    tm = min(TOK_TILE, S)
    half = (d_ff // 2 + 255) // 256 * 256
    splits = ((0, half), (half, d_ff))
    mod = pl.BlockSpec((None, 1, D), lambda b, i: (b, 0, 0))
    row = pl.BlockSpec((1, D), lambda b, i: (0, 0))
    resident = lambda shape: pl.BlockSpec(shape, lambda b, i: (0, 0), pipeline_mode=pl.Buffered(1))
    return pl.pallas_call(
        functools.partial(_ffn_kernel, alpha=alpha, splits=splits),
        grid=(B, S // tm),
        in_specs=[pl.BlockSpec((None, tm, D), lambda b, i: (b, i, 0)), mod, mod, mod,
                  resident(w1.shape), resident(w3.shape), resident(w2.shape), row, row],
        out_specs=pl.BlockSpec((None, tm, D), lambda b, i: (b, i, 0)),
        out_shape=jax.ShapeDtypeStruct((B, S, D), F32),
        compiler_params=_params("arbitrary", "arbitrary"),
        name="ffn",
    )(x, sc, sh, g2, w1, w3, w2, ln_g, ln_b)


def _in_proj_weights(w_in):
    sizes = (ATTN_WIDTH,) + (KV_WIDTH,) * 6 + (N_BRANCH * N_ATTN_HEADS, MLP_WIDTH, MLP_WIDTH)
    offs = [0]
    for s in sizes:
        offs.append(offs[-1] + s)
    col = lambda n: w_in[:, offs[n]:offs[n + 1]]
    q, kc, vc, ksl, vsl, kwn, vwn, gt, u, v = (col(n) for n in range(10))
    w_a = jnp.concatenate([ksl, kwn, kc, vc], axis=1).astype(BF16)
    D = w_in.shape[0]
    gt = gt.reshape(D, N_KV_HEADS, GQA_GROUP, N_BRANCH).transpose(1, 3, 2, 0)
    gt = gt.reshape(N_KV_HEADS, N_BRANCH * GQA_GROUP, D)
    gt = jnp.pad(gt, ((0, 0), (0, GATE_ROWS - N_BRANCH * GQA_GROUP), (0, 0)))
    gt = gt.reshape(N_KV_HEADS * GATE_ROWS, D)
    w_t = jnp.concatenate([u.T, v.T, q.T, vsl.T, vwn.T, gt], axis=0).astype(BF16)
    return w_a, w_t


def _compress_weights(cmp_w1):
    two, _, hid = cmp_w1.shape
    w = cmp_w1.reshape(two, 2, CMP_STRIDE, HEAD_DIM, hid)
    eye = jnp.eye(N_KV_HEADS, dtype=cmp_w1.dtype)
    big = jnp.einsum('thpdc,kj->tpkdjhc', w, eye)
    return big.reshape(two, CMP_STRIDE * N_KV_HEADS * HEAD_DIM, N_KV_HEADS * 2 * hid).astype(BF16)


def _hybrid_layer(x, mod, w_in, cmp_pos, cmp_w1, cmp_w2, vn_g, vn_b, w_s, b_s, out_g, w_o, ln1_g, ln1_b,
                  w1, w3, w2, ln2_g, ln2_b, *, alpha):
    B, S, D = x.shape
    assert S % Q_TILE == 0 and S % TOK_TILE == 0 and S // SLC_LEN <= MAX_SLC_BLOCKS
    sh1, sc1, g1, sh2, sc2, g2 = (mod[:, None, n * D:(n + 1) * D] for n in range(6))
    w_a, w_t = _in_proj_weights(w_in)
    kk, kvc, qt, vt, gates, mlp = _in_proj_call(
        x, sc1, sh1, w_a, w_t,
        vn_g.reshape(N_MLP_GROUPS, MLP_GROUP_DIM, 1), vn_b.reshape(N_MLP_GROUPS, MLP_GROUP_DIM, 1),
        jnp.swapaxes(w_s, 1, 2), b_s)
    ngrp = S // CMP_STRIDE
    tok = kvc.reshape(2, B, ngrp, CMP_STRIDE * KV_WIDTH)
    pos = jnp.broadcast_to(cmp_pos.reshape(2, 1, CMP_LEN * HEAD_DIM), (2, 8, CMP_LEN * HEAD_DIM))
    cmp_kv, cmp_kv_t = _compress_call(tok, _compress_weights(cmp_w1), cmp_w1.astype(BF16), pos, cmp_w2.astype(BF16))
    n_slc = S // SLC_LEN
    oc, bias = _cmp_attn_call(qt, cmp_kv, cmp_kv_t, gates, n_slc=n_slc, topk=min(SLC_TOPK, n_slc))
    ocw = _win_attn_call(qt, kk, vt, gates, oc)
    attn = _slc_attn_call(qt, bias, kk, vt, gates, ocw)
    row = lambda a: a.reshape(1, D)
    x = _out_proj_call(attn, mlp, x, g1, out_g.reshape(-1, HEAD_DIM, 1), w_o.astype(BF16),
                       row(ln1_g), row(ln1_b), alpha=alpha)
    return _ffn_call(x, sc2, sh2, g2, w1.astype(BF16), w3.astype(BF16), w2.astype(BF16),
                     row(ln2_g), row(ln2_b), alpha=alpha)


def kernel(x, c, w_ada, b_ada, w_in, cmp_pos, cmp_w1, cmp_w2, vn_g, vn_b, w_s, b_s, out_g, w_o, ln1_g, ln1_b,
           w1, w3, w2, ln2_g, ln2_b):
    depth = w_ada.shape[0]
    alpha = (2.0 * depth) ** 0.25
    mod = _ada_call(c, w_ada, b_ada)
    for l in range(depth):
        x = _hybrid_layer(x, mod[l], w_in[l], cmp_pos[l], cmp_w1[l], cmp_w2[l], vn_g[l], vn_b[l], w_s[l], b_s[l],
                          out_g[l], w_o[l], ln1_g[l], ln1_b[l], w1[l], w3[l], w2[l], ln2_g[l], ln2_b[l], alpha=alpha)
    return x
```

```python
import functools
import math

import jax
import jax.numpy as jnp
from jax import lax
from jax.experimental import pallas as pl
from jax.experimental.pallas import tpu as pltpu

F32 = jnp.float32
BF16 = jnp.bfloat16

HEAD_DIM = 64
N_KV_HEADS = 2
GQA_GROUP = 4
N_ATTN_HEADS = N_KV_HEADS * GQA_GROUP
ATTN_WIDTH = N_ATTN_HEADS * HEAD_DIM
KV_WIDTH = N_KV_HEADS * HEAD_DIM
GROUP_WIDTH = GQA_GROUP * HEAD_DIM
N_BRANCH = 3
MLP_GROUP_DIM = 64
N_MLP_GROUPS = 8
MLP_WIDTH = N_MLP_GROUPS * MLP_GROUP_DIM
CMP_LEN = 32
CMP_STRIDE = 16
CMP_HIDDEN = 256
SLC_LEN = 64
SLC_TOPK = 16
WINDOW = 512
CHUNK = 128
LN_EPS = 1e-5
NEG = -1e30
FORCED_SCORE = 1e6

LOG2E = math.log2(math.e)
Q_SCALE = HEAD_DIM ** -0.5 * LOG2E

LANES = 128
MAX_SLC_BLOCKS = LANES
TOK_TILE = 512
Q_TILE = 256
SLC_TILE = 1024
SLC_UNROLL = 4
CMP_Q_TILE = 512
PICK_LANES = 256
CMP_ROW_CHUNK = 128
VMEM_LIMIT = 56 * 1024 * 1024
GATE_ROWS = 16

ROW_V = MLP_WIDTH
ROW_Q = ROW_V + MLP_WIDTH
ROW_VSLC = ROW_Q + ATTN_WIDTH
ROW_VWIN = ROW_VSLC + KV_WIDTH
ROW_GATE = ROW_VWIN + KV_WIDTH
ROWS_T = ROW_GATE + N_KV_HEADS * GATE_ROWS
COLS_A = 4 * KV_WIDTH
KK_WIDTH = N_KV_HEADS * LANES + KV_WIDTH
KK_COLBLK_WIN = N_KV_HEADS


def _nt(a, b):
    return lax.dot_general(a, b, (((1,), (1,)), ((), ())), preferred_element_type=F32)


def _tn(a, b):
    return lax.dot_general(a, b, (((0,), (0,)), ((), ())), preferred_element_type=F32)


def _nn(a, b):
    return jnp.dot(a, b, preferred_element_type=F32)


def _gelu(x):
    return x * (0.5 * (1.0 + jnp.tanh(math.sqrt(2.0 / math.pi) * (x + 0.044715 * (x * x * x)))))


def _sigmoid(x):
    return 1.0 / (1.0 + jnp.exp(-x))


def _layer_norm_rows(z, g, b):
    mu = jnp.mean(z, axis=-1, keepdims=True)
    zc = z - mu
    var = jnp.mean(zc * zc, axis=-1, keepdims=True)
    return zc * lax.rsqrt(var + LN_EPS) * g + b


def _slope(kv_head, g):
    slope = lambda h: LOG2E * 2.0 ** (-8.0 * (h + 1) / N_ATTN_HEADS)
    val = jnp.asarray(slope(g), F32)
    for k in range(1, N_KV_HEADS):
        val = jnp.where(kv_head == k, jnp.asarray(slope(k * GQA_GROUP + g), F32), val)
    return val


def _pick_head_rows(x, kv_head):
    out = x[:HEAD_DIM]
    for k in range(1, N_KV_HEADS):
        out = jnp.where(kv_head == k, x[k * HEAD_DIM:(k + 1) * HEAD_DIM], out)
    return out


def _bf16_part(x):
    return x.astype(BF16).astype(F32)


def _alibi_key_cols(pos):
    lane = lax.broadcasted_iota(jnp.int32, pos.shape, 1)
    cols = jnp.where(lane % 2 == 0, pos // LANES, pos % LANES)
    return jnp.where(lane < 4, cols, jnp.where(lane < 8, 1, 0)).astype(F32)


def _alibi_query_rows(c, t0f, rows, tq, row8=0.0):
    shape = (8, tq)
    row = lax.broadcasted_iota(jnp.int32, shape, 0)
    c = jnp.full(shape, c, F32)
    out = jnp.zeros(shape, F32)
    for n, part in enumerate((_bf16_part(c), _bf16_part(c - _bf16_part(c)))):
        shift = part * t0f
        vals = {2 * n: LANES * part, 2 * n + 1: part, 4 + 2 * n: -_bf16_part(shift), 5 + 2 * n: -(shift - _bf16_part(shift))}
        for r, val in vals.items():
            out = jnp.where(row == r, val, out)
    tail = jnp.where(lax.broadcasted_iota(jnp.int32, (rows - 8, tq), 0) == 0, row8, 0.0)
    return jnp.concatenate([out, tail], axis=0)


def _params(*sem):
    return pltpu.CompilerParams(dimension_semantics=sem, vmem_limit_bytes=VMEM_LIMIT)


def _ada_kernel(c_ref, w_ref, b_ref, o_ref):
    c = c_ref[...]
    c_act = (c * _sigmoid(c)).astype(BF16)
    o_ref[0] = _nn(c_act, w_ref[0].astype(BF16)) + b_ref[0]


def _ada_call(c, w_ada, b_ada):
    L, D, D6 = w_ada.shape
    B = c.shape[0]
    return pl.pallas_call(
        _ada_kernel,
        grid=(L, D6 // D),
        in_specs=[pl.BlockSpec((B, D), lambda l, n: (0, 0)),
                  pl.BlockSpec((1, D, D), lambda l, n: (l, 0, n)),
                  pl.BlockSpec((1, 1, D), lambda l, n: (l, 0, n))],
        out_specs=pl.BlockSpec((1, B, D), lambda l, n: (l, 0, n)),
        out_shape=jax.ShapeDtypeStruct((L, B, D6), F32),
        compiler_params=_params("arbitrary", "arbitrary"),
        name="ada_mod",
    )(c, w_ada, b_ada.reshape(L, 1, D6))


def _in_proj_kernel(x_ref, sc_ref, sh_ref, wa_ref, wt_ref, vng_ref, vnb_ref, wst_ref, bs_ref,
                    kk_ref, kvc_ref, qt_ref, vt_ref, gates_ref, mlp_ref):
    tm = x_ref.shape[0]
    h = (x_ref[...] * (1.0 + sc_ref[...]) + sh_ref[...]).astype(BF16)
    uv = _nt(wt_ref[0:ROW_Q, :], h)
    t = _nt(wt_ref[ROW_Q:, :], h)
    qt_ref[...] = (t[:ATTN_WIDTH] * Q_SCALE).astype(BF16)
    vt_ref[...] = t[ROW_VSLC - ROW_Q:ROW_GATE - ROW_Q].astype(BF16)
    gates_ref[...] = t[ROW_GATE - ROW_Q:].reshape(N_KV_HEADS, GATE_ROWS, tm)
    a = _nn(h, wa_ref[...])
    for k in range(N_KV_HEADS):
        kk_ref[:, k * LANES:k * LANES + HEAD_DIM] = a[:, k * HEAD_DIM:(k + 1) * HEAD_DIM].astype(BF16)
        kk_ref[:, k * LANES + HEAD_DIM:(k + 1) * LANES] = jnp.zeros((tm, LANES - HEAD_DIM), BF16)
    kk_ref[:, N_KV_HEADS * LANES:] = a[:, KV_WIDTH:2 * KV_WIDTH].astype(BF16)
    kvc_ref[0] = a[:, 2 * KV_WIDTH:3 * KV_WIDTH].astype(BF16)
    kvc_ref[1] = a[:, 3 * KV_WIDTH:].astype(BF16)

    u = _gelu(uv[:ROW_V])
    v = _gelu(uv[ROW_V:]).reshape(N_MLP_GROUPS, MLP_GROUP_DIM, tm)
    mu = jnp.mean(v, axis=1, keepdims=True)
    vc = v - mu
    var = jnp.mean(vc * vc, axis=1, keepdims=True)
    vn = vc * lax.rsqrt(var + LN_EPS) * vng_ref[...] + vnb_ref[...]
    n_chunks = tm // CHUNK
    s_idx = lax.broadcasted_iota(jnp.int32, (CHUNK, CHUNK), 0)
    t_idx = lax.broadcasted_iota(jnp.int32, (CHUNK, CHUNK), 1)
    for g in range(N_MLP_GROUPS):
        vg = vn[g].astype(BF16)
        stack = jnp.concatenate([vg[:, c * CHUNK:(c + 1) * CHUNK] for c in range(n_chunks)], axis=0)
        w_t = jnp.where(s_idx <= t_idx, wst_ref[g], 0.0).astype(BF16)
        sv = _nn(stack, w_t)
        bias = bs_ref[g:g + 1, :]
        for c in range(n_chunks):
            mlp_ref[g * MLP_GROUP_DIM:(g + 1) * MLP_GROUP_DIM, c * CHUNK:(c + 1) * CHUNK] = (
                u[g * MLP_GROUP_DIM:(g + 1) * MLP_GROUP_DIM, c * CHUNK:(c + 1) * CHUNK]
                * (sv[c * MLP_GROUP_DIM:(c + 1) * MLP_GROUP_DIM] + bias)).astype(BF16)


def _in_proj_call(x, sc, sh, w_a, w_t, vn_g, vn_b, w_st, b_s):
    B, S, D = x.shape
    tm = min(TOK_TILE, S)
    const2 = lambda b, i: (0, 0)
    const3 = lambda b, i: (0, 0, 0)
    cm = lambda rows: pl.BlockSpec((None, rows, tm), lambda b, i: (b, 0, i))
    return pl.pallas_call(
        _in_proj_kernel,
        grid=(B, S // tm),
        in_specs=[pl.BlockSpec((None, tm, D), lambda b, i: (b, i, 0)),
                  pl.BlockSpec((None, 1, D), lambda b, i: (b, 0, 0)),
                  pl.BlockSpec((None, 1, D), lambda b, i: (b, 0, 0)),
                  pl.BlockSpec(w_a.shape, const2),
                  pl.BlockSpec(w_t.shape, const2),
                  pl.BlockSpec(vn_g.shape, const3),
                  pl.BlockSpec(vn_b.shape, const3),
                  pl.BlockSpec(w_st.shape, const3),
                  pl.BlockSpec(b_s.shape, const2)],
        out_specs=[pl.BlockSpec((None, tm, KK_WIDTH), lambda b, i: (b, i, 0)),
                   pl.BlockSpec((2, None, tm, KV_WIDTH), lambda b, i: (0, b, i, 0)),
                   cm(ATTN_WIDTH), cm(2 * KV_WIDTH),
                   pl.BlockSpec((None, N_KV_HEADS, GATE_ROWS, tm), lambda b, i: (b, 0, 0, i)), cm(MLP_WIDTH)],
        out_shape=[jax.ShapeDtypeStruct((B, S, KK_WIDTH), BF16),
                   jax.ShapeDtypeStruct((2, B, S, KV_WIDTH), BF16),
                   jax.ShapeDtypeStruct((B, ATTN_WIDTH, S), BF16),
                   jax.ShapeDtypeStruct((B, 2 * KV_WIDTH, S), BF16),
                   jax.ShapeDtypeStruct((B, N_KV_HEADS, GATE_ROWS, S), F32),
                   jax.ShapeDtypeStruct((B, MLP_WIDTH, S), BF16)],
        compiler_params=_params("arbitrary", "arbitrary"),
        name="in_proj",
    )(x, sc, sh, w_a, w_t, vn_g, vn_b, w_st, b_s)


def _compress_kernel(tok_ref, wbig_ref, w1_ref, pos_ref, w2_ref, out_ref, out_t_ref):
    ab = _nn(tok_ref[...], wbig_ref[...])
    bias = _nn(pos_ref[...].astype(BF16), w1_ref[...])[0:1]
    ngrp = ab.shape[0]
    for h in range(N_KV_HEADS):
        base = h * 2 * CMP_HIDDEN
        first = ab[:, base:base + CMP_HIDDEN]
        second = ab[:, base + CMP_HIDDEN:base + 2 * CMP_HIDDEN]
        hid = _gelu(first + pltpu.roll(second, ngrp - 1, 0) + bias)
        c = _nn(hid.astype(BF16), w2_ref[...])
        block_end = CMP_STRIDE * lax.broadcasted_iota(jnp.int32, (ngrp, HEAD_DIM), 0) + (CMP_LEN - 1)
        out_ref[h, :, 0:HEAD_DIM] = c.astype(BF16)
        out_ref[h, :, HEAD_DIM:] = _alibi_key_cols(block_end).astype(BF16)
        out_t_ref[h] = c.T.astype(BF16)


def _compress_call(tok, wbig, w1, pos, w2):
    two, B, ngrp, flat = tok.shape
    return pl.pallas_call(
        _compress_kernel,
        grid=(two, B),
        in_specs=[pl.BlockSpec((None, None, ngrp, flat), lambda t, b: (t, b, 0, 0)),
                  pl.BlockSpec((None,) + wbig.shape[1:], lambda t, b: (t, 0, 0)),
                  pl.BlockSpec((None,) + w1.shape[1:], lambda t, b: (t, 0, 0)),
                  pl.BlockSpec((None,) + pos.shape[1:], lambda t, b: (t, 0, 0)),
                  pl.BlockSpec((None,) + w2.shape[1:], lambda t, b: (t, 0, 0))],
        out_specs=[pl.BlockSpec((None, None, N_KV_HEADS, ngrp, 2 * HEAD_DIM), lambda t, b: (t, b, 0, 0, 0)),
                   pl.BlockSpec((None, None, N_KV_HEADS, HEAD_DIM, ngrp), lambda t, b: (t, b, 0, 0, 0))],
        out_shape=[jax.ShapeDtypeStruct((two, B, N_KV_HEADS, ngrp, 2 * HEAD_DIM), BF16),
                   jax.ShapeDtypeStruct((two, B, N_KV_HEADS, HEAD_DIM, ngrp), BF16)],
        compiler_params=_params("arbitrary", "arbitrary"),
        name="compress",
    )(tok, wbig, w1, pos, w2)


def _cmp_attn_kernel(qt_ref, kc_ref, vct_ref, gates_ref, oc_ref, bias_ref, imp_ref, score_ref, left_ref,
                     *, n_slc, topk):
    k_head = pl.program_id(1)
    i = pl.program_id(2)
    tq = qt_ref.shape[1]
    ncmp = kc_ref.shape[0]
    t0 = i * tq
    any_ok = (t0 + lax.broadcasted_iota(jnp.int32, (1, tq), 1)) >= (CMP_LEN - 1)
    t0f = t0.astype(F32)
    q_aug = jnp.concatenate(
        [jnp.concatenate([qt_ref[g * HEAD_DIM:(g + 1) * HEAD_DIM, :] for g in range(GQA_GROUP)], axis=1),
         jnp.concatenate([_alibi_query_rows(_slope(k_head, g), t0f, HEAD_DIM, tq) for g in range(GQA_GROUP)],
                         axis=1).astype(BF16)], axis=0)

    def attend(nrows):
        t_col = t0 + lax.broadcasted_iota(jnp.int32, (nrows, tq), 1)
        cmp_end = CMP_STRIDE * lax.broadcasted_iota(jnp.int32, (nrows, tq), 0) + (CMP_LEN - 1)
        ok = t_col >= cmp_end
        kc = kc_ref[0:nrows, :]
        ones = jnp.where(lax.broadcasted_iota(jnp.int32, (16, nrows), 0) == 0, 1.0, 0.0).astype(BF16)
        v_aug = jnp.concatenate([vct_ref[:, 0:nrows], ones], axis=0)
        s_heads = [jnp.where(ok, _nn(kc, q_aug[:, g * tq:(g + 1) * tq]), NEG) for g in range(GQA_GROUP)]
        p_sum = None
        for g, s in enumerate(s_heads):
            e = jnp.exp2(s - jnp.max(s, axis=0, keepdims=True))
            o = _nn(v_aug, e.astype(BF16))
            inv = jnp.where(any_ok, 1.0 / o[HEAD_DIM:HEAD_DIM + 1], 0.0)
            p = e * inv
            p_sum = p if p_sum is None else p_sum + p
            gate = _sigmoid(gates_ref[g:g + 1, :])
            oc_ref[g * HEAD_DIM:(g + 1) * HEAD_DIM, :] = o[:HEAD_DIM] * (inv * gate)
        j_i = lax.broadcasted_iota(jnp.int32, (MAX_SLC_BLOCKS, nrows), 0)
        n_i = lax.broadcasted_iota(jnp.int32, (MAX_SLC_BLOCKS, nrows), 1)
        overlap = ((CMP_STRIDE * n_i <= SLC_LEN * j_i + (SLC_LEN - 1))
                   & (CMP_STRIDE * n_i + (CMP_LEN - 1) >= SLC_LEN * j_i)).astype(BF16)
        p_hi = p_sum.astype(BF16)
        r1 = p_sum - p_hi.astype(F32)
        p_mid = r1.astype(BF16)
        p_lo = (r1 - p_mid.astype(F32)).astype(BF16)
        imp_ref[...] = _nn(overlap, p_hi) + _nn(overlap, p_mid) + _nn(overlap, p_lo)

    n_chunks = ncmp // CMP_ROW_CHUNK
    last_chunk = jnp.minimum(((t0 + tq - CMP_LEN) // CMP_STRIDE) // CMP_ROW_CHUNK, n_chunks - 1)
    for c in range(n_chunks):
        pl.when(last_chunk == c)(functools.partial(attend, (c + 1) * CMP_ROW_CHUNK))
    score = imp_ref[...]

    j_row = lax.broadcasted_iota(jnp.int32, (MAX_SLC_BLOCKS, tq), 0)
    t_blk = t0 + lax.broadcasted_iota(jnp.int32, (MAX_SLC_BLOCKS, tq), 1)
    cur = t_blk // SLC_LEN
    valid = (SLC_LEN * j_row <= t_blk) & (j_row < n_slc)
    forced = (j_row == 0) | (j_row == cur) | (j_row == cur - 1)
    cand = valid & jnp.logical_not(forced)
    n_forced = 1 + (cur[0:1] >= 1).astype(jnp.int32) + (cur[0:1] >= 2).astype(jnp.int32)
    remaining = jnp.broadcast_to(topk - n_forced, (8, tq))
    score = jnp.where(cand, score, -jnp.inf)
    n_groups = tq // PICK_LANES
    for lb in range(n_groups):
        score_ref[lb] = score[:, lb * PICK_LANES:(lb + 1) * PICK_LANES]
        left_ref[lb] = remaining[:, lb * PICK_LANES:(lb + 1) * PICK_LANES]
    j_f = lax.broadcasted_iota(jnp.int32, (MAX_SLC_BLOCKS, LANES), 0).astype(F32)

    def pick_rounds(lb, rounds, nrow):
        halves = [score_ref[lb, 0:nrow, h * LANES:(h + 1) * LANES] for h in range(PICK_LANES // LANES)]
        left = [left_ref[lb, 0:1, h * LANES:(h + 1) * LANES] for h in range(PICK_LANES // LANES)]
        j_n = j_f[0:nrow]
        for r in rounds:
            for h, sc in enumerate(halves):
                m = jnp.max(sc, axis=0, keepdims=True)
                first = jnp.min(jnp.where(sc == m, j_n, float(MAX_SLC_BLOCKS)), axis=0, keepdims=True)
                first = jnp.where(left[h] > r, first, -1.0)
                halves[h] = jnp.where(j_n == first, -jnp.inf, sc)
        score_ref[lb, 0:nrow, :] = jnp.concatenate(halves, axis=1)

    base_rounds = topk - 3
    blocks_started = (t0 + tq) // SLC_LEN
    row_buckets = [r for r in (32, 64) if r < MAX_SLC_BLOCKS] + [MAX_SLC_BLOCKS]
    for lo, nrow in zip([0] + row_buckets[:-1], row_buckets):
        in_bucket = (blocks_started > lo) if nrow == MAX_SLC_BLOCKS else ((blocks_started > lo) & (blocks_started <= nrow))

        @pl.when(in_bucket)
        def _pick(nrow=nrow):
            def group(lb, carry):
                pick_rounds(lb, range(base_rounds), nrow)
                return carry

            lax.fori_loop(0, n_groups, group, 0)

    @pl.when(t0 < 2 * SLC_LEN)
    def _early_queries():
        pick_rounds(0, range(base_rounds, topk - 1), row_buckets[0])

    picked = jnp.concatenate([score_ref[lb] for lb in range(n_groups)], axis=1) == -jnp.inf
    bias_ref[...] = jnp.where((valid & forced) | (cand & picked), 0.0, NEG).astype(BF16)


def _cmp_attn_call(qt, cmp_kv, cmp_kv_t, gates, *, n_slc, topk):
    B, _, S = qt.shape
    ncmp = cmp_kv.shape[3]
    tq = min(CMP_Q_TILE, S)
    assert topk >= 3 and tq >= 2 * SLC_LEN
    kern = functools.partial(_cmp_attn_kernel, n_slc=n_slc, topk=topk)
    return pl.pallas_call(
        kern,
        grid=(B, N_KV_HEADS, S // tq),
        in_specs=[pl.BlockSpec((None, GROUP_WIDTH, tq), lambda b, k, i: (b, k, i)),
                  pl.BlockSpec((None, None, None, ncmp, 2 * HEAD_DIM), lambda b, k, i: (0, b, k, 0, 0)),
                  pl.BlockSpec((None, None, None, HEAD_DIM, ncmp), lambda b, k, i: (1, b, k, 0, 0)),
                  pl.BlockSpec((None, None, GATE_ROWS, tq), lambda b, k, i: (b, k, 0, i))],
        out_specs=[pl.BlockSpec((None, GROUP_WIDTH, tq), lambda b, k, i: (b, k, i)),
                   pl.BlockSpec((None, None, MAX_SLC_BLOCKS, tq), lambda b, k, i: (b, k, 0, i))],
        out_shape=[jax.ShapeDtypeStruct((B, ATTN_WIDTH, S), F32),
                   jax.ShapeDtypeStruct((B, N_KV_HEADS, MAX_SLC_BLOCKS, S), BF16)],
        scratch_shapes=[pltpu.VMEM((MAX_SLC_BLOCKS, tq), F32),
                        pltpu.VMEM((tq // PICK_LANES, MAX_SLC_BLOCKS, PICK_LANES), F32),
                        pltpu.VMEM((tq // PICK_LANES, 8, PICK_LANES), jnp.int32)],
        compiler_params=_params("arbitrary", "arbitrary", "arbitrary"),
        name="cmp_attn",
    )(qt, cmp_kv, cmp_kv_t, gates)


V_ROWS = HEAD_DIM + 16
SLC_HEAD_GROUPS = ((0, 1), (2, 3))
SKIP_MARGIN = 176.0


def _slc_attn_kernel(qt_ref, bias_ref, ks_ref, vt_ref, gates_ref, other_ref, attn_ref,
                     kaug_ref, vh_ref, qaug_ref, m_ref, acc_ref, sa_ref, sb_ref, ta_ref, tb_ref, ksq_ref):
    k_head = pl.program_id(1)
    i = pl.program_id(2)
    tq = qt_ref.shape[1]
    n_tiles, tk = kaug_ref.shape[0], kaug_ref.shape[1]
    t0 = i * tq
    diag = 2 * i

    @pl.when(i == 0)
    def _build_keys():
        def fill(c, k_sq_max):
            r0 = pl.multiple_of(c * tk, tk)
            k_tile = ks_ref[pl.ds(r0, tk), 0:HEAD_DIM]
            kaug_ref[c, :, 0:HEAD_DIM] = k_tile
            k_f = k_tile.astype(F32)
            return jnp.maximum(k_sq_max, jnp.max(jnp.sum(k_f * k_f, axis=1, keepdims=True)))

        ksq_ref[0] = lax.fori_loop(0, n_tiles, fill, jnp.float32(0.0))
        for c in range(n_tiles):
            vh_ref[c, 0:HEAD_DIM, :] = _pick_head_rows(vt_ref[:, c * tk:(c + 1) * tk], k_head)

    @pl.when((i == 0) & (k_head == 0) & (pl.program_id(0) == 0))
    def _build_static():
        def fill(c, carry):
            r0 = pl.multiple_of(c * tk, tk)
            pos = r0 + lax.broadcasted_iota(jnp.int32, (tk, HEAD_DIM), 0)
            kaug_ref[c, :, HEAD_DIM:2 * HEAD_DIM] = _alibi_key_cols(pos).astype(BF16)
            blk = (r0 + lax.broadcasted_iota(jnp.int32, (tk, MAX_SLC_BLOCKS), 0)) // SLC_LEN
            hot = blk == lax.broadcasted_iota(jnp.int32, (tk, MAX_SLC_BLOCKS), 1)
            kaug_ref[c, :, 2 * HEAD_DIM:] = jnp.where(hot, 1.0, 0.0).astype(BF16)
            return carry

        lax.fori_loop(0, n_tiles, fill, 0)
        ones_row = lax.broadcasted_iota(jnp.int32, (V_ROWS - HEAD_DIM, tk), 0) == 0
        for c in range(n_tiles):
            vh_ref[c, HEAD_DIM:, :] = jnp.where(ones_row, 1.0, 0.0).astype(BF16)

    t0f = t0.astype(F32)
    bias = bias_ref[...]
    first_tile = []
    for g in range(GQA_GROUP):
        c_g = _slope(k_head, g)
        q_g = qt_ref[g * HEAD_DIM:(g + 1) * HEAD_DIM, :]
        aux = _alibi_query_rows(c_g, t0f, HEAD_DIM, tq)
        qaug_ref[0:HEAD_DIM, g * tq:(g + 1) * tq] = q_g
        qaug_ref[HEAD_DIM:2 * HEAD_DIM, g * tq:(g + 1) * tq] = aux.astype(BF16)
        qaug_ref[2 * HEAD_DIM:, g * tq:(g + 1) * tq] = bias
        q_f = q_g.astype(F32)
        qk = jnp.sqrt(jnp.max(jnp.sum(q_f * q_f, axis=0, keepdims=True), axis=1, keepdims=True) * ksq_ref[0])
        dist_needed = jnp.minimum((2.0 * qk[0, 0] + SKIP_MARGIN) / c_g, 1e6).astype(jnp.int32)
        first_tile.append(diag - jnp.minimum(diag, (dist_needed - 1) // tk + 1))
    m_ref[...] = jnp.full(m_ref.shape, NEG, F32)
    acc_ref[...] = jnp.zeros(acc_ref.shape, F32)

    s_bufs = (sa_ref, sb_ref)

    tmax_bufs = (ta_ref, tb_ref)

    def tile_cols(g, kind):
        return slice(g * tq + (tk if kind == 'second' else 0), (g + 1) * tq)

    def scores_head(k_tile, g, slot, kind):
        cols = tile_cols(g, kind)
        s = _nn(k_tile, qaug_ref[:, cols])
        if kind != 'plain':
            key = lax.broadcasted_iota(jnp.int32, (tk, tk), 0)
            qry = lax.broadcasted_iota(jnp.int32, (tk, tk), 1)
            masked = jnp.where(key <= qry, s[:, :tk], NEG)
            s = masked if kind == 'second' else jnp.concatenate([masked, s[:, tk:]], axis=1)
        s_bufs[slot][:, cols] = s
        tmax_bufs[slot][:, cols] = jnp.max(s, axis=0, keepdims=True)

    def scores(j, slot, kind, heads):
        k_tile = kaug_ref[j]
        for g in heads:
            scores_head(k_tile, g, slot, kind)

    def step(j, slot, kind, prefetch, heads):
        v_tile = vh_ref[j]
        k_next = kaug_ref[j + 1] if prefetch is not None else None
        for g in heads:
            cols = tile_cols(g, kind)
            if prefetch is not None:
                scores_head(k_next, g, 1 - slot, prefetch)
            m_old = m_ref[:, cols]
            m_new = jnp.maximum(m_old, tmax_bufs[slot][:, cols])
            alpha = jnp.exp2(m_old - m_new)
            p = jnp.exp2(s_bufs[slot][:, cols] - m_new).astype(BF16)
            acc_ref[:, cols] = alpha * acc_ref[:, cols] + _nn(v_tile, p)
            m_ref[:, cols] = m_new

    def run(heads, lo):
        n_plain = diag - lo

        @pl.when(n_plain == 0)
        def _starts_on_diagonal():
            scores(diag, 0, 'first', heads)

        @pl.when(n_plain > 0)
        def _starts_below():
            scores(lo, 0, 'plain', heads)

        def trip(p, carry):
            for u in range(SLC_UNROLL):
                step(lo + SLC_UNROLL * p + u, u % 2, 'plain', 'plain', heads)
            return carry

        n_trips = jnp.maximum(n_plain - 1, 0) // SLC_UNROLL
        lax.fori_loop(0, n_trips, trip, 0)
        done = SLC_UNROLL * n_trips
        for n_left in range(2, SLC_UNROLL + 3):
            @pl.when(n_plain - done + 2 == n_left)
            def _tail(n_left=n_left):
                kinds = ['plain'] * (n_left - 2) + ['first', 'second', None]
                for u in range(n_left):
                    step(lo + done + u, u % 2, kinds[u], kinds[u + 1], heads)

    starts = [functools.reduce(jnp.minimum, [first_tile[g] for g in heads]) for heads in SLC_HEAD_GROUPS]
    same_start = functools.reduce(jnp.logical_and, [s == starts[0] for s in starts[1:]])

    @pl.when(same_start)
    def _one_pass():
        run(tuple(range(GQA_GROUP)), starts[0])

    @pl.when(jnp.logical_not(same_start))
    def _pass_per_group():
        for heads, lo in zip(SLC_HEAD_GROUPS, starts):
            run(heads, lo)

    o = acc_ref[0:HEAD_DIM, :] / acc_ref[HEAD_DIM:HEAD_DIM + 1, :]
    for g in range(GQA_GROUP):
        gate = _sigmoid(gates_ref[GQA_GROUP + g:GQA_GROUP + g + 1, :])
        rows = slice(g * HEAD_DIM, (g + 1) * HEAD_DIM)
        attn_ref[rows, :] = (other_ref[rows, :] + o[:, g * tq:(g + 1) * tq] * gate).astype(BF16)


def _slc_attn_call(qt, bias, kk, vt, gates, other):
    B, _, S = qt.shape
    tq = min(SLC_TILE, S)
    tk = tq // 2
    return pl.pallas_call(
        _slc_attn_kernel,
        grid=(B, N_KV_HEADS, S // tq),
        in_specs=[pl.BlockSpec((None, GROUP_WIDTH, tq), lambda b, k, i: (b, k, i)),
                  pl.BlockSpec((None, None, MAX_SLC_BLOCKS, tq), lambda b, k, i: (b, k, 0, i)),
                  pl.BlockSpec((None, S, LANES), lambda b, k, i: (b, 0, k)),
                  pl.BlockSpec((None, KV_WIDTH, S), lambda b, k, i: (b, 0, 0)),
                  pl.BlockSpec((None, None, GATE_ROWS, tq), lambda b, k, i: (b, k, 0, i)),
                  pl.BlockSpec((None, GROUP_WIDTH, tq), lambda b, k, i: (b, k, i))],
        out_specs=pl.BlockSpec((None, GROUP_WIDTH, tq), lambda b, k, i: (b, k, i)),
        out_shape=jax.ShapeDtypeStruct((B, ATTN_WIDTH, S), BF16),
        scratch_shapes=[pltpu.VMEM((S // tk, tk, 2 * LANES), BF16),
                        pltpu.VMEM((S // tk, V_ROWS, tk), BF16),
                        pltpu.VMEM((2 * LANES, GQA_GROUP * tq), BF16),
                        pltpu.VMEM((1, GQA_GROUP * tq), F32),
                        pltpu.VMEM((V_ROWS, GQA_GROUP * tq), F32),
                        pltpu.VMEM((tk, GQA_GROUP * tq), F32),
                        pltpu.VMEM((tk, GQA_GROUP * tq), F32),
                        pltpu.VMEM((1, GQA_GROUP * tq), F32),
                        pltpu.VMEM((1, GQA_GROUP * tq), F32),
                        pltpu.SMEM((1,), F32)],
        compiler_params=_params("arbitrary", "arbitrary", "arbitrary"),
        name="slc_attn",
    )(qt, bias, kk, vt, gates, other)


def _win_attn_kernel(qt_ref, kprev_ref, kcur_ref, vprev_ref, vcur_ref, gates_ref, other_ref, ow_ref):
    k_head = pl.program_id(1)
    i = pl.program_id(2)
    tq = qt_ref.shape[1] // 2
    key = lax.broadcasted_iota(jnp.int32, (tq, tq), 0)
    qry = lax.broadcasted_iota(jnp.int32, (tq, tq), 1)
    keeps = (qry < key, None, key <= qry)
    ones = jnp.where(lax.broadcasted_iota(jnp.int32, (16, tq), 0) == 0, 1.0, 0.0).astype(BF16)
    lane = lax.broadcasted_iota(jnp.int32, (tq, KV_WIDTH), 1)
    cols_in_tile = _alibi_key_cols(lax.broadcasted_iota(jnp.int32, (tq, KV_WIDTH), 0))
    high_lanes = jnp.where((lane == 0) | (lane == 2), 1.0, 0.0)
    flag_lane = jnp.where(lane == 8, 1.0, 0.0)
    k_augs, v_tiles = [], []
    for n, (k_ref, v_ref) in enumerate(((kprev_ref, vprev_ref), (kprev_ref, vprev_ref),
                                        (kcur_ref, vcur_ref), (kcur_ref, vcur_ref))):
        half = slice((n % 2) * tq, (n % 2 + 1) * tq)
        tile = 2 * i - 2 + n
        k_aux = (cols_in_tile + high_lanes * (tile * (tq // LANES)).astype(F32)
                 + flag_lane * jnp.where(tile < 0, 1.0, 0.0))
        k_augs.append(jnp.concatenate([k_ref[half, :], k_aux.astype(BF16)], axis=1))
        v_tiles.append(jnp.concatenate([v_ref[:, half], ones], axis=0))

    q2s = []
    for h in range(2):
        cols = slice(h * tq, (h + 1) * tq)
        t0f = ((2 * i + h) * tq).astype(F32)
        qs = jnp.concatenate([qt_ref[g * HEAD_DIM:(g + 1) * HEAD_DIM, cols] for g in range(GQA_GROUP)], axis=1)
        alibi = jnp.concatenate([_alibi_query_rows(_slope(k_head, g), t0f, KV_WIDTH, tq, row8=NEG)
                                 for g in range(GQA_GROUP)], axis=1).astype(BF16)
        q2s.append(jnp.concatenate([jnp.where(k_head == k, qs, jnp.zeros_like(qs)) for k in range(N_KV_HEADS)]
                                   + [alibi], axis=0))
    mask = lambda keep, s: s if keep is None else jnp.where(keep, s, NEG)
    scores = [[[mask(keep, _nn(k_aug, q2s[h][:, g * tq:(g + 1) * tq]))
                for k_aug, keep in zip(k_augs[h:h + 3], keeps)]
               for g in range(GQA_GROUP)] for h in range(2)]
    for h in range(2):
        for g in range(GQA_GROUP):
            m = functools.reduce(jnp.maximum, [jnp.max(s, axis=0, keepdims=True) for s in scores[h][g]])
            o2 = sum(_nn(v, jnp.exp2(s - m).astype(BF16)) for v, s in zip(v_tiles[h:h + 3], scores[h][g]))
            o = _pick_head_rows(o2[:KV_WIDTH], k_head) / o2[KV_WIDTH:KV_WIDTH + 1]
            gate = _sigmoid(gates_ref[2 * GQA_GROUP + g:2 * GQA_GROUP + g + 1, h * tq:(h + 1) * tq])
            rows, cols = slice(g * HEAD_DIM, (g + 1) * HEAD_DIM), slice(h * tq, (h + 1) * tq)
            ow_ref[rows, cols] = other_ref[rows, cols] + o * gate


def _win_attn_call(qt, kk, vt, gates, other):
    B, _, S = qt.shape
    tq = min(Q_TILE, S)
    assert WINDOW == 2 * tq and S % (2 * tq) == 0
    k_spec = lambda back: pl.BlockSpec((None, 2 * tq, KV_WIDTH),
                                       lambda b, k, i: (b, jnp.maximum(i - back, 0), KK_COLBLK_WIN))
    v_spec = lambda back: pl.BlockSpec((None, KV_WIDTH, 2 * tq), lambda b, k, i: (b, 1, jnp.maximum(i - back, 0)))
    return pl.pallas_call(
        _win_attn_kernel,
        grid=(B, N_KV_HEADS, S // (2 * tq)),
        in_specs=[pl.BlockSpec((None, GROUP_WIDTH, 2 * tq), lambda b, k, i: (b, k, i)),
                  k_spec(1), k_spec(0), v_spec(1), v_spec(0),
                  pl.BlockSpec((None, None, GATE_ROWS, 2 * tq), lambda b, k, i: (b, k, 0, i)),
                  pl.BlockSpec((None, GROUP_WIDTH, 2 * tq), lambda b, k, i: (b, k, i))],
        out_specs=pl.BlockSpec((None, GROUP_WIDTH, 2 * tq), lambda b, k, i: (b, k, i)),
        out_shape=jax.ShapeDtypeStruct((B, ATTN_WIDTH, S), F32),
        compiler_params=_params("arbitrary", "arbitrary", "arbitrary"),
        name="win_attn",
    )(qt, kk, kk, vt, vt, gates, other)


def _group_rms(a, gain):
    c, tm = a.shape
    a3 = a.reshape(c // HEAD_DIM, HEAD_DIM, tm)
    ms = jnp.mean(a3 * a3, axis=1, keepdims=True)
    return (a3 * lax.rsqrt(ms + LN_EPS) * gain).reshape(c, tm)


def _out_proj_kernel(attn_ref, mlp_ref, x_ref, g1_ref, og_ref, wo_ref, lng_ref, lnb_ref, o_ref, *, alpha):
    n_attn = ATTN_WIDTH // HEAD_DIM
    attn = _group_rms(attn_ref[...].astype(F32), og_ref[:n_attn])
    mlp = _group_rms(mlp_ref[...].astype(F32), og_ref[n_attn:])
    y_t = jnp.concatenate([attn, mlp], axis=0).astype(BF16)
    y = _tn(y_t, wo_ref[...])
    z = alpha * x_ref[...] + (1.0 + g1_ref[...]) * y
    o_ref[...] = _layer_norm_rows(z, lng_ref[...], lnb_ref[...])


def _out_proj_call(attn, mlp, x, g1, out_g, w_o, ln_g, ln_b, *, alpha):
    B, S, D = x.shape
    tm = min(TOK_TILE, S)
    cm = lambda width: pl.BlockSpec((None, width, tm), lambda b, i: (b, 0, i))
    row = pl.BlockSpec((1, D), lambda b, i: (0, 0))
    return pl.pallas_call(
        functools.partial(_out_proj_kernel, alpha=alpha),
        grid=(B, S // tm),
        in_specs=[cm(ATTN_WIDTH), cm(MLP_WIDTH),
                  pl.BlockSpec((None, tm, D), lambda b, i: (b, i, 0)),
                  pl.BlockSpec((None, 1, D), lambda b, i: (b, 0, 0)),
                  pl.BlockSpec(out_g.shape, lambda b, i: (0, 0, 0)),
                  pl.BlockSpec(w_o.shape, lambda b, i: (0, 0)),
                  row, row],
        out_specs=pl.BlockSpec((None, tm, D), lambda b, i: (b, i, 0)),
        out_shape=jax.ShapeDtypeStruct((B, S, D), F32),
        compiler_params=_params("arbitrary", "arbitrary"),
        name="out_proj",
    )(attn, mlp, x, g1, out_g, w_o, ln_g, ln_b)


def _ffn_kernel(x_ref, sc_ref, sh_ref, g2_ref, w1_ref, w3_ref, w2_ref, lng_ref, lnb_ref, o_ref, *, alpha, splits):
    x = x_ref[...]
    h = (x * (1.0 + sc_ref[...]) + sh_ref[...]).astype(BF16)
    f = None
    for lo, hi in splits:
        a = _nn(h, w1_ref[:, lo:hi])
        b = _nn(h, w3_ref[:, lo:hi])
        part = _nn((a * _sigmoid(a) * b).astype(BF16), w2_ref[lo:hi, :])
        f = part if f is None else f + part
    z = alpha * x + (1.0 + g2_ref[...]) * f
    o_ref[...] = _layer_norm_rows(z, lng_ref[...], lnb_ref[...])


def _ffn_call(x, sc, sh, g2, w1, w3, w2, ln_g, ln_b, *, alpha):
    B, S, D = x.shape
    d_ff = w1.shape[1]
    tm = min(TOK_TILE, S)
    half = (d_ff // 2 + 255) // 256 * 256
    splits = ((0, half), (half, d_ff))
    mod = pl.BlockSpec((None, 1, D), lambda b, i: (b, 0, 0))
    row = pl.BlockSpec((1, D), lambda b, i: (0, 0))
    resident = lambda shape: pl.BlockSpec(shape, lambda b, i: (0, 0), pipeline_mode=pl.Buffered(1))
    return pl.pallas_call(
        functools.partial(_ffn_kernel, alpha=alpha, splits=splits),
        grid=(B, S // tm),
        in_specs=[pl.BlockSpec((None, tm, D), lambda b, i: (b, i, 0)), mod, mod, mod,
                  resident(w1.shape), resident(w3.shape), resident(w2.shape), row, row],
        out_specs=pl.BlockSpec((None, tm, D), lambda b, i: (b, i, 0)),
        out_shape=jax.ShapeDtypeStruct((B, S, D), F32),
        compiler_params=_params("arbitrary", "arbitrary"),
        name="ffn",
    )(x, sc, sh, g2, w1, w3, w2, ln_g, ln_b)


def _in_proj_weights(w_in):
    sizes = (ATTN_WIDTH,) + (KV_WIDTH,) * 6 + (N_BRANCH * N_ATTN_HEADS, MLP_WIDTH, MLP_WIDTH)
    offs = [0]
    for s in sizes:
        offs.append(offs[-1] + s)
    col = lambda n: w_in[:, offs[n]:offs[n + 1]]
    q, kc, vc, ksl, vsl, kwn, vwn, gt, u, v = (col(n) for n in range(10))
    w_a = jnp.concatenate([ksl, kwn, kc, vc], axis=1).astype(BF16)
    D = w_in.shape[0]
    gt = gt.reshape(D, N_KV_HEADS, GQA_GROUP, N_BRANCH).transpose(1, 3, 2, 0)
    gt = gt.reshape(N_KV_HEADS, N_BRANCH * GQA_GROUP, D)
    gt = jnp.pad(gt, ((0, 0), (0, GATE_ROWS - N_BRANCH * GQA_GROUP), (0, 0)))
    gt = gt.reshape(N_KV_HEADS * GATE_ROWS, D)
    w_t = jnp.concatenate([u.T, v.T, q.T, vsl.T, vwn.T, gt], axis=0).astype(BF16)
    return w_a, w_t


def _compress_weights(cmp_w1):
    two, _, hid = cmp_w1.shape
    w = cmp_w1.reshape(two, 2, CMP_STRIDE, HEAD_DIM, hid)
    eye = jnp.eye(N_KV_HEADS, dtype=cmp_w1.dtype)
    big = jnp.einsum('thpdc,kj->tpkdjhc', w, eye)
    return big.reshape(two, CMP_STRIDE * N_KV_HEADS * HEAD_DIM, N_KV_HEADS * 2 * hid).astype(BF16)


def _hybrid_layer(x, mod, w_in, cmp_pos, cmp_w1, cmp_w2, vn_g, vn_b, w_s, b_s, out_g, w_o, ln1_g, ln1_b,
                  w1, w3, w2, ln2_g, ln2_b, *, alpha):
    B, S, D = x.shape
    assert S % Q_TILE == 0 and S % TOK_TILE == 0 and S // SLC_LEN <= MAX_SLC_BLOCKS
    sh1, sc1, g1, sh2, sc2, g2 = (mod[:, None, n * D:(n + 1) * D] for n in range(6))
    w_a, w_t = _in_proj_weights(w_in)
    kk, kvc, qt, vt, gates, mlp = _in_proj_call(
        x, sc1, sh1, w_a, w_t,
        vn_g.reshape(N_MLP_GROUPS, MLP_GROUP_DIM, 1), vn_b.reshape(N_MLP_GROUPS, MLP_GROUP_DIM, 1),
        jnp.swapaxes(w_s, 1, 2), b_s)
    ngrp = S // CMP_STRIDE
    tok = kvc.reshape(2, B, ngrp, CMP_STRIDE * KV_WIDTH)
    pos = jnp.broadcast_to(cmp_pos.reshape(2, 1, CMP_LEN * HEAD_DIM), (2, 8, CMP_LEN * HEAD_DIM))
    cmp_kv, cmp_kv_t = _compress_call(tok, _compress_weights(cmp_w1), cmp_w1.astype(BF16), pos, cmp_w2.astype(BF16))
    n_slc = S // SLC_LEN
    oc, bias = _cmp_attn_call(qt, cmp_kv, cmp_kv_t, gates, n_slc=n_slc, topk=min(SLC_TOPK, n_slc))
    ocw = _win_attn_call(qt, kk, vt, gates, oc)
    attn = _slc_attn_call(qt, bias, kk, vt, gates, ocw)
    row = lambda a: a.reshape(1, D)
    x = _out_proj_call(attn, mlp, x, g1, out_g.reshape(-1, HEAD_DIM, 1), w_o.astype(BF16),
                       row(ln1_g), row(ln1_b), alpha=alpha)
    return _ffn_call(x, sc2, sh2, g2, w1.astype(BF16), w3.astype(BF16), w2.astype(BF16),
                     row(ln2_g), row(ln2_b), alpha=alpha)


def kernel(x, c, w_ada, b_ada, w_in, cmp_pos, cmp_w1, cmp_w2, vn_g, vn_b, w_s, b_s, out_g, w_o, ln1_g, ln1_b,
           w1, w3, w2, ln2_g, ln2_b):
    depth = w_ada.shape[0]
    alpha = (2.0 * depth) ** 0.25
    mod = _ada_call(c, w_ada, b_ada)
    for l in range(depth):
        x = _hybrid_layer(x, mod[l], w_in[l], cmp_pos[l], cmp_w1[l], cmp_w2[l], vn_g[l], vn_b[l], w_s[l], b_s[l],
                          out_g[l], w_o[l], ln1_g[l], ln1_b[l], w1[l], w3[l], w2[l], ln2_g[l], ln2_b[l], alpha=alpha)
    return x
```

```python
import functools
import math

import jax
import jax.numpy as jnp
from jax import lax
from jax.experimental import pallas as pl
from jax.experimental.pallas import tpu as pltpu

F32 = jnp.float32
BF16 = jnp.bfloat16

HEAD_DIM = 64
N_KV_HEADS = 2
GQA_GROUP = 4
N_ATTN_HEADS = N_KV_HEADS * GQA_GROUP
ATTN_WIDTH = N_ATTN_HEADS * HEAD_DIM
KV_WIDTH = N_KV_HEADS * HEAD_DIM
GROUP_WIDTH = GQA_GROUP * HEAD_DIM
N_BRANCH = 3
MLP_GROUP_DIM = 64
N_MLP_GROUPS = 8
MLP_WIDTH = N_MLP_GROUPS * MLP_GROUP_DIM
CMP_LEN = 32
CMP_STRIDE = 16
CMP_HIDDEN = 256
SLC_LEN = 64
SLC_TOPK = 16
WINDOW = 512
CHUNK = 128
LN_EPS = 1e-5
NEG = -1e30
FORCED_SCORE = 1e6

LOG2E = math.log2(math.e)
Q_SCALE = HEAD_DIM ** -0.5 * LOG2E

LANES = 128
MAX_SLC_BLOCKS = LANES
TOK_TILE = 512
Q_TILE = 256
SLC_TILE = 512
SLC_UNROLL = 2
CMP_Q_TILE = 512
PICK_LANES = 256
CMP_ROW_CHUNK = 128
VMEM_LIMIT = 56 * 1024 * 1024
GATE_ROWS = 16

ROW_V = MLP_WIDTH
ROW_Q = ROW_V + MLP_WIDTH
ROW_VSLC = ROW_Q + ATTN_WIDTH
ROW_VWIN = ROW_VSLC + KV_WIDTH
ROW_GATE = ROW_VWIN + KV_WIDTH
ROWS_T = ROW_GATE + N_KV_HEADS * GATE_ROWS
COLS_A = 4 * KV_WIDTH
KK_WIDTH = N_KV_HEADS * LANES + KV_WIDTH
KK_COLBLK_WIN = N_KV_HEADS


def _nt(a, b):
    return lax.dot_general(a, b, (((1,), (1,)), ((), ())), preferred_element_type=F32)


def _tn(a, b):
    return lax.dot_general(a, b, (((0,), (0,)), ((), ())), preferred_element_type=F32)


def _nn(a, b):
    return jnp.dot(a, b, preferred_element_type=F32)


def _gelu(x):
    return x * (0.5 * (1.0 + jnp.tanh(math.sqrt(2.0 / math.pi) * (x + 0.044715 * (x * x * x)))))


def _sigmoid(x):
    return 1.0 / (1.0 + jnp.exp(-x))


def _layer_norm_rows(z, g, b):
    mu = jnp.mean(z, axis=-1, keepdims=True)
    zc = z - mu
    var = jnp.mean(zc * zc, axis=-1, keepdims=True)
    return zc * lax.rsqrt(var + LN_EPS) * g + b


def _slope(kv_head, g):
    slope = lambda h: LOG2E * 2.0 ** (-8.0 * (h + 1) / N_ATTN_HEADS)
    val = jnp.asarray(slope(g), F32)
    for k in range(1, N_KV_HEADS):
        val = jnp.where(kv_head == k, jnp.asarray(slope(k * GQA_GROUP + g), F32), val)
    return val


def _pick_head_rows(x, kv_head):
    out = x[:HEAD_DIM]
    for k in range(1, N_KV_HEADS):
        out = jnp.where(kv_head == k, x[k * HEAD_DIM:(k + 1) * HEAD_DIM], out)
    return out


def _bf16_part(x):
    return x.astype(BF16).astype(F32)


def _alibi_key_cols(pos):
    lane = lax.broadcasted_iota(jnp.int32, pos.shape, 1)
    cols = jnp.where(lane % 2 == 0, pos // LANES, pos % LANES)
    return jnp.where(lane < 4, cols, jnp.where(lane < 8, 1, 0)).astype(F32)


def _alibi_query_rows(c, t0f, rows, tq, row8=0.0):
    shape = (8, tq)
    row = lax.broadcasted_iota(jnp.int32, shape, 0)
    c = jnp.full(shape, c, F32)
    out = jnp.zeros(shape, F32)
    for n, part in enumerate((_bf16_part(c), _bf16_part(c - _bf16_part(c)))):
        shift = part * t0f
        vals = {2 * n: LANES * part, 2 * n + 1: part, 4 + 2 * n: -_bf16_part(shift), 5 + 2 * n: -(shift - _bf16_part(shift))}
        for r, val in vals.items():
            out = jnp.where(row == r, val, out)
    tail = jnp.where(lax.broadcasted_iota(jnp.int32, (rows - 8, tq), 0) == 0, row8, 0.0)
    return jnp.concatenate([out, tail], axis=0)


def _params(*sem):
    return pltpu.CompilerParams(dimension_semantics=sem, vmem_limit_bytes=VMEM_LIMIT)


def _ada_kernel(c_ref, w_ref, b_ref, o_ref):
    c = c_ref[...]
    c_act = (c * _sigmoid(c)).astype(BF16)
    o_ref[0] = _nn(c_act, w_ref[0].astype(BF16)) + b_ref[0]


def _ada_call(c, w_ada, b_ada):
    L, D, D6 = w_ada.shape
    B = c.shape[0]
    return pl.pallas_call(
        _ada_kernel,
        grid=(L, D6 // D),
        in_specs=[pl.BlockSpec((B, D), lambda l, n: (0, 0)),
                  pl.BlockSpec((1, D, D), lambda l, n: (l, 0, n)),
                  pl.BlockSpec((1, 1, D), lambda l, n: (l, 0, n))],
        out_specs=pl.BlockSpec((1, B, D), lambda l, n: (l, 0, n)),
        out_shape=jax.ShapeDtypeStruct((L, B, D6), F32),
        compiler_params=_params("arbitrary", "arbitrary"),
        name="ada_mod",
    )(c, w_ada, b_ada.reshape(L, 1, D6))


def _in_proj_kernel(x_ref, sc_ref, sh_ref, wa_ref, wt_ref, vng_ref, vnb_ref, wst_ref, bs_ref,
                    kk_ref, kvc_ref, qt_ref, vt_ref, gates_ref, mlp_ref):
    tm = x_ref.shape[0]
    h = (x_ref[...] * (1.0 + sc_ref[...]) + sh_ref[...]).astype(BF16)
    uv = _nt(wt_ref[0:ROW_Q, :], h)
    t = _nt(wt_ref[ROW_Q:, :], h)
    qt_ref[...] = (t[:ATTN_WIDTH] * Q_SCALE).astype(BF16)
    vt_ref[...] = t[ROW_VSLC - ROW_Q:ROW_GATE - ROW_Q].astype(BF16)
    gates_ref[...] = t[ROW_GATE - ROW_Q:].reshape(N_KV_HEADS, GATE_ROWS, tm)
    a = _nn(h, wa_ref[...])
    for k in range(N_KV_HEADS):
        kk_ref[:, k * LANES:k * LANES + HEAD_DIM] = a[:, k * HEAD_DIM:(k + 1) * HEAD_DIM].astype(BF16)
        kk_ref[:, k * LANES + HEAD_DIM:(k + 1) * LANES] = jnp.zeros((tm, LANES - HEAD_DIM), BF16)
    kk_ref[:, N_KV_HEADS * LANES:] = a[:, KV_WIDTH:2 * KV_WIDTH].astype(BF16)
    kvc_ref[0] = a[:, 2 * KV_WIDTH:3 * KV_WIDTH].astype(BF16)
    kvc_ref[1] = a[:, 3 * KV_WIDTH:].astype(BF16)

    u = _gelu(uv[:ROW_V])
    v = _gelu(uv[ROW_V:]).reshape(N_MLP_GROUPS, MLP_GROUP_DIM, tm)
    mu = jnp.mean(v, axis=1, keepdims=True)
    vc = v - mu
    var = jnp.mean(vc * vc, axis=1, keepdims=True)
    vn = vc * lax.rsqrt(var + LN_EPS) * vng_ref[...] + vnb_ref[...]
    n_chunks = tm // CHUNK
    s_idx = lax.broadcasted_iota(jnp.int32, (CHUNK, CHUNK), 0)
    t_idx = lax.broadcasted_iota(jnp.int32, (CHUNK, CHUNK), 1)
    for g in range(N_MLP_GROUPS):
        vg = vn[g].astype(BF16)
        stack = jnp.concatenate([vg[:, c * CHUNK:(c + 1) * CHUNK] for c in range(n_chunks)], axis=0)
        w_t = jnp.where(s_idx <= t_idx, wst_ref[g], 0.0).astype(BF16)
        sv = _nn(stack, w_t)
        bias = bs_ref[g:g + 1, :]
        for c in range(n_chunks):
            mlp_ref[g * MLP_GROUP_DIM:(g + 1) * MLP_GROUP_DIM, c * CHUNK:(c + 1) * CHUNK] = (
                u[g * MLP_GROUP_DIM:(g + 1) * MLP_GROUP_DIM, c * CHUNK:(c + 1) * CHUNK]
                * (sv[c * MLP_GROUP_DIM:(c + 1) * MLP_GROUP_DIM] + bias)).astype(BF16)


def _in_proj_call(x, sc, sh, w_a, w_t, vn_g, vn_b, w_st, b_s):
    B, S, D = x.shape
    tm = min(TOK_TILE, S)
    const2 = lambda b, i: (0, 0)
    const3 = lambda b, i: (0, 0, 0)
    cm = lambda rows: pl.BlockSpec((None, rows, tm), lambda b, i: (b, 0, i))
    return pl.pallas_call(
        _in_proj_kernel,
        grid=(B, S // tm),
        in_specs=[pl.BlockSpec((None, tm, D), lambda b, i: (b, i, 0)),
                  pl.BlockSpec((None, 1, D), lambda b, i: (b, 0, 0)),
                  pl.BlockSpec((None, 1, D), lambda b, i: (b, 0, 0)),
                  pl.BlockSpec(w_a.shape, const2),
                  pl.BlockSpec(w_t.shape, const2),
                  pl.BlockSpec(vn_g.shape, const3),
                  pl.BlockSpec(vn_b.shape, const3),
                  pl.BlockSpec(w_st.shape, const3),
                  pl.BlockSpec(b_s.shape, const2)],
        out_specs=[pl.BlockSpec((None, tm, KK_WIDTH), lambda b, i: (b, i, 0)),
                   pl.BlockSpec((2, None, tm, KV_WIDTH), lambda b, i: (0, b, i, 0)),
                   cm(ATTN_WIDTH), cm(2 * KV_WIDTH),
                   pl.BlockSpec((None, N_KV_HEADS, GATE_ROWS, tm), lambda b, i: (b, 0, 0, i)), cm(MLP_WIDTH)],
        out_shape=[jax.ShapeDtypeStruct((B, S, KK_WIDTH), BF16),
                   jax.ShapeDtypeStruct((2, B, S, KV_WIDTH), BF16),
                   jax.ShapeDtypeStruct((B, ATTN_WIDTH, S), BF16),
                   jax.ShapeDtypeStruct((B, 2 * KV_WIDTH, S), BF16),
                   jax.ShapeDtypeStruct((B, N_KV_HEADS, GATE_ROWS, S), F32),
                   jax.ShapeDtypeStruct((B, MLP_WIDTH, S), BF16)],
        compiler_params=_params("arbitrary", "arbitrary"),
        name="in_proj",
    )(x, sc, sh, w_a, w_t, vn_g, vn_b, w_st, b_s)


def _compress_kernel(tok_ref, wbig_ref, w1_ref, pos_ref, w2_ref, out_ref, out_t_ref):
    ab = _nn(tok_ref[...], wbig_ref[...])
    bias = _nn(pos_ref[...].astype(BF16), w1_ref[...])[0:1]
    ngrp = ab.shape[0]
    for h in range(N_KV_HEADS):
        base = h * 2 * CMP_HIDDEN
        first = ab[:, base:base + CMP_HIDDEN]
        second = ab[:, base + CMP_HIDDEN:base + 2 * CMP_HIDDEN]
        hid = _gelu(first + pltpu.roll(second, ngrp - 1, 0) + bias)
        c = _nn(hid.astype(BF16), w2_ref[...])
        block_end = CMP_STRIDE * lax.broadcasted_iota(jnp.int32, (ngrp, HEAD_DIM), 0) + (CMP_LEN - 1)
        out_ref[h, :, 0:HEAD_DIM] = c.astype(BF16)
        out_ref[h, :, HEAD_DIM:] = _alibi_key_cols(block_end).astype(BF16)
        out_t_ref[h] = c.T.astype(BF16)


def _compress_call(tok, wbig, w1, pos, w2):
    two, B, ngrp, flat = tok.shape
    return pl.pallas_call(
        _compress_kernel,
        grid=(two, B),
        in_specs=[pl.BlockSpec((None, None, ngrp, flat), lambda t, b: (t, b, 0, 0)),
                  pl.BlockSpec((None,) + wbig.shape[1:], lambda t, b: (t, 0, 0)),
                  pl.BlockSpec((None,) + w1.shape[1:], lambda t, b: (t, 0, 0)),
                  pl.BlockSpec((None,) + pos.shape[1:], lambda t, b: (t, 0, 0)),
                  pl.BlockSpec((None,) + w2.shape[1:], lambda t, b: (t, 0, 0))],
        out_specs=[pl.BlockSpec((None, None, N_KV_HEADS, ngrp, 2 * HEAD_DIM), lambda t, b: (t, b, 0, 0, 0)),
                   pl.BlockSpec((None, None, N_KV_HEADS, HEAD_DIM, ngrp), lambda t, b: (t, b, 0, 0, 0))],
        out_shape=[jax.ShapeDtypeStruct((two, B, N_KV_HEADS, ngrp, 2 * HEAD_DIM), BF16),
                   jax.ShapeDtypeStruct((two, B, N_KV_HEADS, HEAD_DIM, ngrp), BF16)],
        compiler_params=_params("arbitrary", "arbitrary"),
        name="compress",
    )(tok, wbig, w1, pos, w2)


def _cmp_attn_kernel(qt_ref, kc_ref, vct_ref, gates_ref, oc_ref, bias_ref, imp_ref, score_ref, left_ref,
                     *, n_slc, topk):
    k_head = pl.program_id(1)
    i = pl.program_id(2)
    tq = qt_ref.shape[1]
    ncmp = kc_ref.shape[0]
    t0 = i * tq
    any_ok = (t0 + lax.broadcasted_iota(jnp.int32, (1, tq), 1)) >= (CMP_LEN - 1)
    t0f = t0.astype(F32)
    q_aug = jnp.concatenate(
        [jnp.concatenate([qt_ref[g * HEAD_DIM:(g + 1) * HEAD_DIM, :] for g in range(GQA_GROUP)], axis=1),
         jnp.concatenate([_alibi_query_rows(_slope(k_head, g), t0f, HEAD_DIM, tq) for g in range(GQA_GROUP)],
                         axis=1).astype(BF16)], axis=0)

    def attend(nrows):
        t_col = t0 + lax.broadcasted_iota(jnp.int32, (nrows, tq), 1)
        cmp_end = CMP_STRIDE * lax.broadcasted_iota(jnp.int32, (nrows, tq), 0) + (CMP_LEN - 1)
        ok = t_col >= cmp_end
        kc = kc_ref[0:nrows, :]
        ones = jnp.where(lax.broadcasted_iota(jnp.int32, (16, nrows), 0) == 0, 1.0, 0.0).astype(BF16)
        v_aug = jnp.concatenate([vct_ref[:, 0:nrows], ones], axis=0)
        s_heads = [jnp.where(ok, _nn(kc, q_aug[:, g * tq:(g + 1) * tq]), NEG) for g in range(GQA_GROUP)]
        p_sum = None
        for g, s in enumerate(s_heads):
            e = jnp.exp2(s - jnp.max(s, axis=0, keepdims=True))
            o = _nn(v_aug, e.astype(BF16))
            inv = jnp.where(any_ok, 1.0 / o[HEAD_DIM:HEAD_DIM + 1], 0.0)
            p = e * inv
            p_sum = p if p_sum is None else p_sum + p
            gate = _sigmoid(gates_ref[g:g + 1, :])
            oc_ref[g * HEAD_DIM:(g + 1) * HEAD_DIM, :] = o[:HEAD_DIM] * (inv * gate)
        j_i = lax.broadcasted_iota(jnp.int32, (MAX_SLC_BLOCKS, nrows), 0)
        n_i = lax.broadcasted_iota(jnp.int32, (MAX_SLC_BLOCKS, nrows), 1)
        overlap = ((CMP_STRIDE * n_i <= SLC_LEN * j_i + (SLC_LEN - 1))
                   & (CMP_STRIDE * n_i + (CMP_LEN - 1) >= SLC_LEN * j_i)).astype(BF16)
        p_hi = p_sum.astype(BF16)
        r1 = p_sum - p_hi.astype(F32)
        p_mid = r1.astype(BF16)
        p_lo = (r1 - p_mid.astype(F32)).astype(BF16)
        imp_ref[...] = _nn(overlap, p_hi) + _nn(overlap, p_mid) + _nn(overlap, p_lo)

    n_chunks = ncmp // CMP_ROW_CHUNK
    last_chunk = jnp.minimum(((t0 + tq - CMP_LEN) // CMP_STRIDE) // CMP_ROW_CHUNK, n_chunks - 1)
    for c in range(n_chunks):
        pl.when(last_chunk == c)(functools.partial(attend, (c + 1) * CMP_ROW_CHUNK))
    score = imp_ref[...]

    j_row = lax.broadcasted_iota(jnp.int32, (MAX_SLC_BLOCKS, tq), 0)
    t_blk = t0 + lax.broadcasted_iota(jnp.int32, (MAX_SLC_BLOCKS, tq), 1)
    cur = t_blk // SLC_LEN
    valid = (SLC_LEN * j_row <= t_blk) & (j_row < n_slc)
    forced = (j_row == 0) | (j_row == cur) | (j_row == cur - 1)
    cand = valid & jnp.logical_not(forced)
    n_forced = 1 + (cur[0:1] >= 1).astype(jnp.int32) + (cur[0:1] >= 2).astype(jnp.int32)
    remaining = jnp.broadcast_to(topk - n_forced, (8, tq))
    score = jnp.where(cand, score, -jnp.inf)
    n_groups = tq // PICK_LANES
    for lb in range(n_groups):
        score_ref[lb] = score[:, lb * PICK_LANES:(lb + 1) * PICK_LANES]
        left_ref[lb] = remaining[:, lb * PICK_LANES:(lb + 1) * PICK_LANES]
    j_f = lax.broadcasted_iota(jnp.int32, (MAX_SLC_BLOCKS, LANES), 0).astype(F32)

    def pick_rounds(lb, rounds, nrow):
        halves = [score_ref[lb, 0:nrow, h * LANES:(h + 1) * LANES] for h in range(PICK_LANES // LANES)]
        left = [left_ref[lb, 0:1, h * LANES:(h + 1) * LANES] for h in range(PICK_LANES // LANES)]
        j_n = j_f[0:nrow]
        for r in rounds:
            for h, sc in enumerate(halves):
                m = jnp.max(sc, axis=0, keepdims=True)
                first = jnp.min(jnp.where(sc == m, j_n, float(MAX_SLC_BLOCKS)), axis=0, keepdims=True)
                first = jnp.where(left[h] > r, first, -1.0)
                halves[h] = jnp.where(j_n == first, -jnp.inf, sc)
        score_ref[lb, 0:nrow, :] = jnp.concatenate(halves, axis=1)

    base_rounds = topk - 3
    blocks_started = (t0 + tq) // SLC_LEN
    row_buckets = [r for r in (32, 64) if r < MAX_SLC_BLOCKS] + [MAX_SLC_BLOCKS]
    for lo, nrow in zip([0] + row_buckets[:-1], row_buckets):
        in_bucket = (blocks_started > lo) if nrow == MAX_SLC_BLOCKS else ((blocks_started > lo) & (blocks_started <= nrow))

        @pl.when(in_bucket)
        def _pick(nrow=nrow):
            def group(lb, carry):
                pick_rounds(lb, range(base_rounds), nrow)
                return carry

            lax.fori_loop(0, n_groups, group, 0)

    @pl.when(t0 < 2 * SLC_LEN)
    def _early_queries():
        pick_rounds(0, range(base_rounds, topk - 1), row_buckets[0])

    picked = jnp.concatenate([score_ref[lb] for lb in range(n_groups)], axis=1) == -jnp.inf
    bias_ref[...] = jnp.where((valid & forced) | (cand & picked), 0.0, NEG).astype(BF16)


def _cmp_attn_call(qt, cmp_kv, cmp_kv_t, gates, *, n_slc, topk):
    B, _, S = qt.shape
    ncmp = cmp_kv.shape[3]
    tq = min(CMP_Q_TILE, S)
    assert topk >= 3 and tq >= 2 * SLC_LEN
    kern = functools.partial(_cmp_attn_kernel, n_slc=n_slc, topk=topk)
    return pl.pallas_call(
        kern,
        grid=(B, N_KV_HEADS, S // tq),
        in_specs=[pl.BlockSpec((None, GROUP_WIDTH, tq), lambda b, k, i: (b, k, i)),
                  pl.BlockSpec((None, None, None, ncmp, 2 * HEAD_DIM), lambda b, k, i: (0, b, k, 0, 0)),
                  pl.BlockSpec((None, None, None, HEAD_DIM, ncmp), lambda b, k, i: (1, b, k, 0, 0)),
                  pl.BlockSpec((None, None, GATE_ROWS, tq), lambda b, k, i: (b, k, 0, i))],
        out_specs=[pl.BlockSpec((None, GROUP_WIDTH, tq), lambda b, k, i: (b, k, i)),
                   pl.BlockSpec((None, None, MAX_SLC_BLOCKS, tq), lambda b, k, i: (b, k, 0, i))],
        out_shape=[jax.ShapeDtypeStruct((B, ATTN_WIDTH, S), F32),
                   jax.ShapeDtypeStruct((B, N_KV_HEADS, MAX_SLC_BLOCKS, S), BF16)],
        scratch_shapes=[pltpu.VMEM((MAX_SLC_BLOCKS, tq), F32),
                        pltpu.VMEM((tq // PICK_LANES, MAX_SLC_BLOCKS, PICK_LANES), F32),
                        pltpu.VMEM((tq // PICK_LANES, 8, PICK_LANES), jnp.int32)],
        compiler_params=_params("arbitrary", "arbitrary", "arbitrary"),
        name="cmp_attn",
    )(qt, cmp_kv, cmp_kv_t, gates)


V_ROWS = HEAD_DIM + 16
SLC_HEAD_GROUPS = ((0, 1), (2, 3))
SKIP_MARGIN = 176.0


def _slc_attn_kernel(qt_ref, bias_ref, ks_ref, vt_ref, gates_ref, other_ref, attn_ref,
                     kaug_ref, vh_ref, qaug_ref, m_ref, acc_ref, sa_ref, sb_ref, ta_ref, tb_ref, ksq_ref):
    k_head = pl.program_id(1)
    i = pl.program_id(2)
    tq = qt_ref.shape[1]
    n_tiles = kaug_ref.shape[0]
    t0 = i * tq

    @pl.when(i == 0)
    def _build_keys():
        def fill(c, k_sq_max):
            r0 = pl.multiple_of(c * tq, tq)
            k_tile = ks_ref[pl.ds(r0, tq), 0:HEAD_DIM]
            kaug_ref[c, :, 0:HEAD_DIM] = k_tile
            k_f = k_tile.astype(F32)
            return jnp.maximum(k_sq_max, jnp.max(jnp.sum(k_f * k_f, axis=1, keepdims=True)))

        ksq_ref[0] = lax.fori_loop(0, n_tiles, fill, jnp.float32(0.0))
        for c in range(n_tiles):
            vh_ref[c, 0:HEAD_DIM, :] = _pick_head_rows(vt_ref[:, c * tq:(c + 1) * tq], k_head)

    @pl.when((i == 0) & (k_head == 0) & (pl.program_id(0) == 0))
    def _build_static():
        def fill(c, carry):
            r0 = pl.multiple_of(c * tq, tq)
            pos = r0 + lax.broadcasted_iota(jnp.int32, (tq, HEAD_DIM), 0)
            kaug_ref[c, :, HEAD_DIM:2 * HEAD_DIM] = _alibi_key_cols(pos).astype(BF16)
            blk = (r0 + lax.broadcasted_iota(jnp.int32, (tq, MAX_SLC_BLOCKS), 0)) // SLC_LEN
            hot = blk == lax.broadcasted_iota(jnp.int32, (tq, MAX_SLC_BLOCKS), 1)
            kaug_ref[c, :, 2 * HEAD_DIM:] = jnp.where(hot, 1.0, 0.0).astype(BF16)
            return carry

        lax.fori_loop(0, n_tiles, fill, 0)
        ones_row = lax.broadcasted_iota(jnp.int32, (V_ROWS - HEAD_DIM, tq), 0) == 0
        for c in range(n_tiles):
            vh_ref[c, HEAD_DIM:, :] = jnp.where(ones_row, 1.0, 0.0).astype(BF16)

    t0f = t0.astype(F32)
    bias = bias_ref[...]
    first_tile = []
    for g in range(GQA_GROUP):
        c_g = _slope(k_head, g)
        q_g = qt_ref[g * HEAD_DIM:(g + 1) * HEAD_DIM, :]
        aux = _alibi_query_rows(c_g, t0f, HEAD_DIM, tq)
        qaug_ref[0:HEAD_DIM, g * tq:(g + 1) * tq] = q_g
        qaug_ref[HEAD_DIM:2 * HEAD_DIM, g * tq:(g + 1) * tq] = aux.astype(BF16)
        qaug_ref[2 * HEAD_DIM:, g * tq:(g + 1) * tq] = bias
        q_f = q_g.astype(F32)
        qk = jnp.sqrt(jnp.max(jnp.sum(q_f * q_f, axis=0, keepdims=True), axis=1, keepdims=True) * ksq_ref[0])
        dist_needed = jnp.minimum((2.0 * qk[0, 0] + SKIP_MARGIN) / c_g, 1e6).astype(jnp.int32)
        first_tile.append(i - jnp.minimum(i, (dist_needed - 1) // tq + 1))
    m_ref[...] = jnp.full(m_ref.shape, NEG, F32)
    acc_ref[...] = jnp.zeros(acc_ref.shape, F32)

    s_bufs = (sa_ref, sb_ref)

    tmax_bufs = (ta_ref, tb_ref)

    def scores_head(k_tile, g, slot, causal):
        cols = slice(g * tq, (g + 1) * tq)
        s = _nn(k_tile, qaug_ref[:, cols])
        if causal:
            key = lax.broadcasted_iota(jnp.int32, (tq, tq), 0)
            qry = lax.broadcasted_iota(jnp.int32, (tq, tq), 1)
            s = jnp.where(key <= qry, s, NEG)
        s_bufs[slot][:, cols] = s
        tmax_bufs[slot][:, cols] = jnp.max(s, axis=0, keepdims=True)

    def scores(j, slot, causal, heads):
        k_tile = kaug_ref[j]
        for g in heads:
            scores_head(k_tile, g, slot, causal)

    def step(j, slot, prefetch, heads):
        v_tile = vh_ref[j]
        k_next = kaug_ref[j + 1] if prefetch is not None else None
        for g in heads:
            cols = slice(g * tq, (g + 1) * tq)
            if prefetch is not None:
                scores_head(k_next, g, 1 - slot, prefetch == 'causal')
            m_old = m_ref[:, cols]
            m_new = jnp.maximum(m_old, tmax_bufs[slot][:, cols])
            alpha = jnp.exp2(m_old - m_new)
            p = jnp.exp2(s_bufs[slot][:, cols] - m_new).astype(BF16)
            acc_ref[:, cols] = alpha * acc_ref[:, cols] + _nn(v_tile, p)
            m_ref[:, cols] = m_new

    def run(heads, lo):
        n_plain = i - lo

        @pl.when(n_plain == 0)
        def _first_is_causal():
            scores(i, 0, True, heads)

        @pl.when(n_plain > 0)
        def _first_is_plain():
            scores(lo, 0, False, heads)

        def trip(p, carry):
            for u in range(SLC_UNROLL):
                step(lo + SLC_UNROLL * p + u, u % 2, 'plain', heads)
            return carry

        n_trips = jnp.maximum(n_plain - 1, 0) // SLC_UNROLL
        lax.fori_loop(0, n_trips, trip, 0)
        done = SLC_UNROLL * n_trips
        for n_left in range(1, SLC_UNROLL + 2):
            @pl.when(n_plain - done + 1 == n_left)
            def _tail(n_left=n_left):
                for u in range(n_left):
                    prefetch = (None, 'causal')[u == n_left - 2] if u >= n_left - 2 else 'plain'
                    step(lo + done + u, u % 2, prefetch, heads)

    starts = [functools.reduce(jnp.minimum, [first_tile[g] for g in heads]) for heads in SLC_HEAD_GROUPS]
    same_start = functools.reduce(jnp.logical_and, [s == starts[0] for s in starts[1:]])

    @pl.when(same_start)
    def _one_pass():
        run(tuple(range(GQA_GROUP)), starts[0])

    @pl.when(jnp.logical_not(same_start))
    def _pass_per_group():
        for heads, lo in zip(SLC_HEAD_GROUPS, starts):
            run(heads, lo)

    o = acc_ref[0:HEAD_DIM, :] / acc_ref[HEAD_DIM:HEAD_DIM + 1, :]
    for g in range(GQA_GROUP):
        gate = _sigmoid(gates_ref[GQA_GROUP + g:GQA_GROUP + g + 1, :])
        rows = slice(g * HEAD_DIM, (g + 1) * HEAD_DIM)
        attn_ref[rows, :] = (other_ref[rows, :] + o[:, g * tq:(g + 1) * tq] * gate).astype(BF16)


def _slc_attn_call(qt, bias, kk, vt, gates, other):
    B, _, S = qt.shape
    tq = min(SLC_TILE, S)
    return pl.pallas_call(
        _slc_attn_kernel,
        grid=(B, N_KV_HEADS, S // tq),
        in_specs=[pl.BlockSpec((None, GROUP_WIDTH, tq), lambda b, k, i: (b, k, i)),
                  pl.BlockSpec((None, None, MAX_SLC_BLOCKS, tq), lambda b, k, i: (b, k, 0, i)),
                  pl.BlockSpec((None, S, LANES), lambda b, k, i: (b, 0, k)),
                  pl.BlockSpec((None, KV_WIDTH, S), lambda b, k, i: (b, 0, 0)),
                  pl.BlockSpec((None, None, GATE_ROWS, tq), lambda b, k, i: (b, k, 0, i)),
                  pl.BlockSpec((None, GROUP_WIDTH, tq), lambda b, k, i: (b, k, i))],
        out_specs=pl.BlockSpec((None, GROUP_WIDTH, tq), lambda b, k, i: (b, k, i)),
        out_shape=jax.ShapeDtypeStruct((B, ATTN_WIDTH, S), BF16),
        scratch_shapes=[pltpu.VMEM((S // tq, tq, 2 * LANES), BF16),
                        pltpu.VMEM((S // tq, V_ROWS, tq), BF16),
                        pltpu.VMEM((2 * LANES, GQA_GROUP * tq), BF16),
                        pltpu.VMEM((1, GQA_GROUP * tq), F32),
                        pltpu.VMEM((V_ROWS, GQA_GROUP * tq), F32),
                        pltpu.VMEM((tq, GQA_GROUP * tq), F32),
                        pltpu.VMEM((tq, GQA_GROUP * tq), F32),
                        pltpu.VMEM((1, GQA_GROUP * tq), F32),
                        pltpu.VMEM((1, GQA_GROUP * tq), F32),
                        pltpu.SMEM((1,), F32)],
        compiler_params=_params("arbitrary", "arbitrary", "arbitrary"),
        name="slc_attn",
    )(qt, bias, kk, vt, gates, other)


def _win_attn_kernel(qt_ref, kprev_ref, kcur_ref, vprev_ref, vcur_ref, gates_ref, other_ref, ow_ref):
    k_head = pl.program_id(1)
    i = pl.program_id(2)
    tq = qt_ref.shape[1] // 2
    key = lax.broadcasted_iota(jnp.int32, (tq, tq), 0)
    qry = lax.broadcasted_iota(jnp.int32, (tq, tq), 1)
    keeps = (qry < key, None, key <= qry)
    ones = jnp.where(lax.broadcasted_iota(jnp.int32, (16, tq), 0) == 0, 1.0, 0.0).astype(BF16)
    lane = lax.broadcasted_iota(jnp.int32, (tq, KV_WIDTH), 1)
    cols_in_tile = _alibi_key_cols(lax.broadcasted_iota(jnp.int32, (tq, KV_WIDTH), 0))
    high_lanes = jnp.where((lane == 0) | (lane == 2), 1.0, 0.0)
    flag_lane = jnp.where(lane == 8, 1.0, 0.0)
    k_augs, v_tiles = [], []
    for n, (k_ref, v_ref) in enumerate(((kprev_ref, vprev_ref), (kprev_ref, vprev_ref),
                                        (kcur_ref, vcur_ref), (kcur_ref, vcur_ref))):
        half = slice((n % 2) * tq, (n % 2 + 1) * tq)
        tile = 2 * i - 2 + n
        k_aux = (cols_in_tile + high_lanes * (tile * (tq // LANES)).astype(F32)
                 + flag_lane * jnp.where(tile < 0, 1.0, 0.0))
        k_augs.append(jnp.concatenate([k_ref[half, :], k_aux.astype(BF16)], axis=1))
        v_tiles.append(jnp.concatenate([v_ref[:, half], ones], axis=0))

    q2s = []
    for h in range(2):
        cols = slice(h * tq, (h + 1) * tq)
        t0f = ((2 * i + h) * tq).astype(F32)
        qs = jnp.concatenate([qt_ref[g * HEAD_DIM:(g + 1) * HEAD_DIM, cols] for g in range(GQA_GROUP)], axis=1)
        alibi = jnp.concatenate([_alibi_query_rows(_slope(k_head, g), t0f, KV_WIDTH, tq, row8=NEG)
                                 for g in range(GQA_GROUP)], axis=1).astype(BF16)
        q2s.append(jnp.concatenate([jnp.where(k_head == k, qs, jnp.zeros_like(qs)) for k in range(N_KV_HEADS)]
                                   + [alibi], axis=0))
    mask = lambda keep, s: s if keep is None else jnp.where(keep, s, NEG)
    scores = [[[mask(keep, _nn(k_aug, q2s[h][:, g * tq:(g + 1) * tq]))
                for k_aug, keep in zip(k_augs[h:h + 3], keeps)]
               for g in range(GQA_GROUP)] for h in range(2)]
    for h in range(2):
        for g in range(GQA_GROUP):
            m = functools.reduce(jnp.maximum, [jnp.max(s, axis=0, keepdims=True) for s in scores[h][g]])
            o2 = sum(_nn(v, jnp.exp2(s - m).astype(BF16)) for v, s in zip(v_tiles[h:h + 3], scores[h][g]))
            o = _pick_head_rows(o2[:KV_WIDTH], k_head) / o2[KV_WIDTH:KV_WIDTH + 1]
            gate = _sigmoid(gates_ref[2 * GQA_GROUP + g:2 * GQA_GROUP + g + 1, h * tq:(h + 1) * tq])
            rows, cols = slice(g * HEAD_DIM, (g + 1) * HEAD_DIM), slice(h * tq, (h + 1) * tq)
            ow_ref[rows, cols] = other_ref[rows, cols] + o * gate


def _win_attn_call(qt, kk, vt, gates, other):
    B, _, S = qt.shape
    tq = min(Q_TILE, S)
    assert WINDOW == 2 * tq and S % (2 * tq) == 0
    k_spec = lambda back: pl.BlockSpec((None, 2 * tq, KV_WIDTH),
                                       lambda b, k, i: (b, jnp.maximum(i - back, 0), KK_COLBLK_WIN))
    v_spec = lambda back: pl.BlockSpec((None, KV_WIDTH, 2 * tq), lambda b, k, i: (b, 1, jnp.maximum(i - back, 0)))
    return pl.pallas_call(
        _win_attn_kernel,
        grid=(B, N_KV_HEADS, S // (2 * tq)),
        in_specs=[pl.BlockSpec((None, GROUP_WIDTH, 2 * tq), lambda b, k, i: (b, k, i)),
                  k_spec(1), k_spec(0), v_spec(1), v_spec(0),
                  pl.BlockSpec((None, None, GATE_ROWS, 2 * tq), lambda b, k, i: (b, k, 0, i)),
                  pl.BlockSpec((None, GROUP_WIDTH, 2 * tq), lambda b, k, i: (b, k, i))],
        out_specs=pl.BlockSpec((None, GROUP_WIDTH, 2 * tq), lambda b, k, i: (b, k, i)),
        out_shape=jax.ShapeDtypeStruct((B, ATTN_WIDTH, S), F32),
        compiler_params=_params("arbitrary", "arbitrary", "arbitrary"),
        name="win_attn",
    )(qt, kk, kk, vt, vt, gates, other)


def _group_rms(a, gain):
    c, tm = a.shape
    a3 = a.reshape(c // HEAD_DIM, HEAD_DIM, tm)
    ms = jnp.mean(a3 * a3, axis=1, keepdims=True)
    return (a3 * lax.rsqrt(ms + LN_EPS) * gain).reshape(c, tm)


def _out_proj_kernel(attn_ref, mlp_ref, x_ref, g1_ref, og_ref, wo_ref, lng_ref, lnb_ref, o_ref, *, alpha):
    n_attn = ATTN_WIDTH // HEAD_DIM
    attn = _group_rms(attn_ref[...].astype(F32), og_ref[:n_attn])
    mlp = _group_rms(mlp_ref[...].astype(F32), og_ref[n_attn:])
    y_t = jnp.concatenate([attn, mlp], axis=0).astype(BF16)
    y = _tn(y_t, wo_ref[...])
    z = alpha * x_ref[...] + (1.0 + g1_ref[...]) * y
    o_ref[...] = _layer_norm_rows(z, lng_ref[...], lnb_ref[...])


def _out_proj_call(attn, mlp, x, g1, out_g, w_o, ln_g, ln_b, *, alpha):
    B, S, D = x.shape
    tm = min(TOK_TILE, S)
    cm = lambda width: pl.BlockSpec((None, width, tm), lambda b, i: (b, 0, i))
    row = pl.BlockSpec((1, D), lambda b, i: (0, 0))
    return pl.pallas_call(
        functools.partial(_out_proj_kernel, alpha=alpha),
        grid=(B, S // tm),
        in_specs=[cm(ATTN_WIDTH), cm(MLP_WIDTH),
                  pl.BlockSpec((None, tm, D), lambda b, i: (b, i, 0)),
                  pl.BlockSpec((None, 1, D), lambda b, i: (b, 0, 0)),
                  pl.BlockSpec(out_g.shape, lambda b, i: (0, 0, 0)),
                  pl.BlockSpec(w_o.shape, lambda b, i: (0, 0)),
                  row, row],
        out_specs=pl.BlockSpec((None, tm, D), lambda b, i: (b, i, 0)),
        out_shape=jax.ShapeDtypeStruct((B, S, D), F32),
        compiler_params=_params("arbitrary", "arbitrary"),
        name="out_proj",
    )(attn, mlp, x, g1, out_g, w_o, ln_g, ln_b)


def _ffn_kernel(x_ref, sc_ref, sh_ref, g2_ref, w1_ref, w3_ref, w2_ref, lng_ref, lnb_ref, o_ref, *, alpha, splits):
    x = x_ref[...]
    h = (x * (1.0 + sc_ref[...]) + sh_ref[...]).astype(BF16)
    f = None
    for lo, hi in splits:
        a = _nn(h, w1_ref[:, lo:hi])
        b = _nn(h, w3_ref[:, lo:hi])
        part = _nn((a * _sigmoid(a) * b).astype(BF16), w2_ref[lo:hi, :])
        f = part if f is None else f + part
    z = alpha * x + (1.0 + g2_ref[...]) * f
    o_ref[...] = _layer_norm_rows(z, lng_ref[...], lnb_ref[...])


def _ffn_call(x, sc, sh, g2, w1, w3, w2, ln_g, ln_b, *, alpha):
    B, S, D = x.shape
    d_ff = w1.shape[1]
    tm = min(TOK_TILE, S)
    half = (d_ff // 2 + 255) // 256 * 256
    splits = ((0, half), (half, d_ff))
    mod = pl.BlockSpec((None, 1, D), lambda b, i: (b, 0, 0))
    row = pl.BlockSpec((1, D), lambda b, i: (0, 0))
    resident = lambda shape: pl.BlockSpec(shape, lambda b, i: (0, 0), pipeline_mode=pl.Buffered(1))
    return pl.pallas_call(
        functools.partial(_ffn_kernel, alpha=alpha, splits=splits),
        grid=(B, S // tm),
        in_specs=[pl.BlockSpec((None, tm, D), lambda b, i: (b, i, 0)), mod, mod, mod,
                  resident(w1.shape), resident(w3.shape), resident(w2.shape), row, row],
        out_specs=pl.BlockSpec((None, tm, D), lambda b, i: (b, i, 0)),
        out_shape=jax.ShapeDtypeStruct((B, S, D), F32),
        compiler_params=_params("arbitrary", "arbitrary"),
        name="ffn",
    )(x, sc, sh, g2, w1, w3, w2, ln_g, ln_b)


def _in_proj_weights(w_in):
    sizes = (ATTN_WIDTH,) + (KV_WIDTH,) * 6 + (N_BRANCH * N_ATTN_HEADS, MLP_WIDTH, MLP_WIDTH)
    offs = [0]
    for s in sizes:
        offs.append(offs[-1] + s)
    col = lambda n: w_in[:, offs[n]:offs[n + 1]]
    q, kc, vc, ksl, vsl, kwn, vwn, gt, u, v = (col(n) for n in range(10))
    w_a = jnp.concatenate([ksl, kwn, kc, vc], axis=1).astype(BF16)
    D = w_in.shape[0]
    gt = gt.reshape(D, N_KV_HEADS, GQA_GROUP, N_BRANCH).transpose(1, 3, 2, 0)
    gt = gt.reshape(N_KV_HEADS, N_BRANCH * GQA_GROUP, D)
    gt = jnp.pad(gt, ((0, 0), (0, GATE_ROWS - N_BRANCH * GQA_GROUP), (0, 0)))
    gt = gt.reshape(N_KV_HEADS * GATE_ROWS, D)
    w_t = jnp.concatenate([u.T, v.T, q.T, vsl.T, vwn.T, gt], axis=0).astype(BF16)
    return w_a, w_t


def _compress_weights(cmp_w1):
    two, _, hid = cmp_w1.shape
    w = cmp_w1.reshape(two, 2, CMP_STRIDE, HEAD_DIM, hid)
    eye = jnp.eye(N_KV_HEADS, dtype=cmp_w1.dtype)
    big = jnp.einsum('thpdc,kj->tpkdjhc', w, eye)
    return big.reshape(two, CMP_STRIDE * N_KV_HEADS * HEAD_DIM, N_KV_HEADS * 2 * hid).astype(BF16)


def _hybrid_layer(x, mod, w_in, cmp_pos, cmp_w1, cmp_w2, vn_g, vn_b, w_s, b_s, out_g, w_o, ln1_g, ln1_b,
                  w1, w3, w2, ln2_g, ln2_b, *, alpha):
    B, S, D = x.shape
    assert S % Q_TILE == 0 and S % TOK_TILE == 0 and S // SLC_LEN <= MAX_SLC_BLOCKS
    sh1, sc1, g1, sh2, sc2, g2 = (mod[:, None, n * D:(n + 1) * D] for n in range(6))
    w_a, w_t = _in_proj_weights(w_in)
    kk, kvc, qt, vt, gates, mlp = _in_proj_call(
        x, sc1, sh1, w_a, w_t,
        vn_g.reshape(N_MLP_GROUPS, MLP_GROUP_DIM, 1), vn_b.reshape(N_MLP_GROUPS, MLP_GROUP_DIM, 1),
        jnp.swapaxes(w_s, 1, 2), b_s)
    ngrp = S // CMP_STRIDE
    tok = kvc.reshape(2, B, ngrp, CMP_STRIDE * KV_WIDTH)
    pos = jnp.broadcast_to(cmp_pos.reshape(2, 1, CMP_LEN * HEAD_DIM), (2, 8, CMP_LEN * HEAD_DIM))
    cmp_kv, cmp_kv_t = _compress_call(tok, _compress_weights(cmp_w1), cmp_w1.astype(BF16), pos, cmp_w2.astype(BF16))
    n_slc = S // SLC_LEN
    oc, bias = _cmp_attn_call(qt, cmp_kv, cmp_kv_t, gates, n_slc=n_slc, topk=min(SLC_TOPK, n_slc))
    ocw = _win_attn_call(qt, kk, vt, gates, oc)
    attn = _slc_attn_call(qt, bias, kk, vt, gates, ocw)
    row = lambda a: a.reshape(1, D)
    x = _out_proj_call(attn, mlp, x, g1, out_g.reshape(-1, HEAD_DIM, 1), w_o.astype(BF16),
                       row(ln1_g), row(ln1_b), alpha=alpha)
    return _ffn_call(x, sc2, sh2, g2, w1.astype(BF16), w3.astype(BF16), w2.astype(BF16),
                     row(ln2_g), row(ln2_b), alpha=alpha)


def kernel(x, c, w_ada, b_ada, w_in, cmp_pos, cmp_w1, cmp_w2, vn_g, vn_b, w_s, b_s, out_g, w_o, ln1_g, ln1_b,
           w1, w3, w2, ln2_g, ln2_b):
    depth = w_ada.shape[0]
    alpha = (2.0 * depth) ** 0.25
    mod = _ada_call(c, w_ada, b_ada)
    for l in range(depth):
        x = _hybrid_layer(x, mod[l], w_in[l], cmp_pos[l], cmp_w1[l], cmp_w2[l], vn_g[l], vn_b[l], w_s[l], b_s[l],
                          out_g[l], w_o[l], ln1_g[l], ln1_b[l], w1[l], w3[l], w2[l], ln2_g[l], ln2_b[l], alpha=alpha)
    return x
```

```python
import functools
import math

import jax
import jax.numpy as jnp
from jax import lax
from jax.experimental import pallas as pl
from jax.experimental.pallas import tpu as pltpu

F32 = jnp.float32
BF16 = jnp.bfloat16

HEAD_DIM = 64
N_KV_HEADS = 2
GQA_GROUP = 4
N_ATTN_HEADS = N_KV_HEADS * GQA_GROUP
ATTN_WIDTH = N_ATTN_HEADS * HEAD_DIM
KV_WIDTH = N_KV_HEADS * HEAD_DIM
GROUP_WIDTH = GQA_GROUP * HEAD_DIM
N_BRANCH = 3
MLP_GROUP_DIM = 64
N_MLP_GROUPS = 8
MLP_WIDTH = N_MLP_GROUPS * MLP_GROUP_DIM
CMP_LEN = 32
CMP_STRIDE = 16
CMP_HIDDEN = 256
SLC_LEN = 64
SLC_TOPK = 16
WINDOW = 512
CHUNK = 128
LN_EPS = 1e-5
NEG = -1e30
FORCED_SCORE = 1e6

LOG2E = math.log2(math.e)
Q_SCALE = HEAD_DIM ** -0.5 * LOG2E

LANES = 128
MAX_SLC_BLOCKS = LANES
TOK_TILE = 512
Q_TILE = 256
SLC_TILE = 512
SLC_UNROLL = 2
CMP_Q_TILE = 512
PICK_LANES = 256
CMP_ROW_CHUNK = 128
VMEM_LIMIT = 56 * 1024 * 1024
GATE_ROWS = 16

ROW_V = MLP_WIDTH
ROW_Q = ROW_V + MLP_WIDTH
ROW_VSLC = ROW_Q + ATTN_WIDTH
ROW_VWIN = ROW_VSLC + KV_WIDTH
ROW_GATE = ROW_VWIN + KV_WIDTH
ROWS_T = ROW_GATE + N_KV_HEADS * GATE_ROWS
COLS_A = 4 * KV_WIDTH
KK_WIDTH = N_KV_HEADS * LANES + KV_WIDTH
KK_COLBLK_WIN = N_KV_HEADS


def _nt(a, b):
    return lax.dot_general(a, b, (((1,), (1,)), ((), ())), preferred_element_type=F32)


def _tn(a, b):
    return lax.dot_general(a, b, (((0,), (0,)), ((), ())), preferred_element_type=F32)


def _nn(a, b):
    return jnp.dot(a, b, preferred_element_type=F32)


def _gelu(x):
    return x * (0.5 * (1.0 + jnp.tanh(math.sqrt(2.0 / math.pi) * (x + 0.044715 * (x * x * x)))))


def _sigmoid(x):
    return 1.0 / (1.0 + jnp.exp(-x))


def _layer_norm_rows(z, g, b):
    mu = jnp.mean(z, axis=-1, keepdims=True)
    zc = z - mu
    var = jnp.mean(zc * zc, axis=-1, keepdims=True)
    return zc * lax.rsqrt(var + LN_EPS) * g + b


def _slope(kv_head, g):
    slope = lambda h: LOG2E * 2.0 ** (-8.0 * (h + 1) / N_ATTN_HEADS)
    val = jnp.asarray(slope(g), F32)
    for k in range(1, N_KV_HEADS):
        val = jnp.where(kv_head == k, jnp.asarray(slope(k * GQA_GROUP + g), F32), val)
    return val


def _pick_head_rows(x, kv_head):
    out = x[:HEAD_DIM]
    for k in range(1, N_KV_HEADS):
        out = jnp.where(kv_head == k, x[k * HEAD_DIM:(k + 1) * HEAD_DIM], out)
    return out


def _bf16_part(x):
    return x.astype(BF16).astype(F32)


def _alibi_key_cols(pos):
    lane = lax.broadcasted_iota(jnp.int32, pos.shape, 1)
    cols = jnp.where(lane % 2 == 0, pos // LANES, pos % LANES)
    return jnp.where(lane < 4, cols, jnp.where(lane < 8, 1, 0)).astype(F32)


def _alibi_query_rows(c, t0f, rows, tq, row8=0.0):
    shape = (8, tq)
    row = lax.broadcasted_iota(jnp.int32, shape, 0)
    c = jnp.full(shape, c, F32)
    out = jnp.zeros(shape, F32)
    for n, part in enumerate((_bf16_part(c), _bf16_part(c - _bf16_part(c)))):
        shift = part * t0f
        vals = {2 * n: LANES * part, 2 * n + 1: part, 4 + 2 * n: -_bf16_part(shift), 5 + 2 * n: -(shift - _bf16_part(shift))}
        for r, val in vals.items():
            out = jnp.where(row == r, val, out)
    tail = jnp.where(lax.broadcasted_iota(jnp.int32, (rows - 8, tq), 0) == 0, row8, 0.0)
    return jnp.concatenate([out, tail], axis=0)


def _params(*sem):
    return pltpu.CompilerParams(dimension_semantics=sem, vmem_limit_bytes=VMEM_LIMIT)


def _ada_kernel(c_ref, w_ref, b_ref, o_ref):
    c = c_ref[...]
    c_act = (c * _sigmoid(c)).astype(BF16)
    o_ref[0] = _nn(c_act, w_ref[0].astype(BF16)) + b_ref[0]


def _ada_call(c, w_ada, b_ada):
    L, D, D6 = w_ada.shape
    B = c.shape[0]
    return pl.pallas_call(
        _ada_kernel,
        grid=(L, D6 // D),
        in_specs=[pl.BlockSpec((B, D), lambda l, n: (0, 0)),
                  pl.BlockSpec((1, D, D), lambda l, n: (l, 0, n)),
                  pl.BlockSpec((1, 1, D), lambda l, n: (l, 0, n))],
        out_specs=pl.BlockSpec((1, B, D), lambda l, n: (l, 0, n)),
        out_shape=jax.ShapeDtypeStruct((L, B, D6), F32),
        compiler_params=_params("arbitrary", "arbitrary"),
        name="ada_mod",
    )(c, w_ada, b_ada.reshape(L, 1, D6))


def _in_proj_kernel(x_ref, sc_ref, sh_ref, wa_ref, wt_ref, vng_ref, vnb_ref, wst_ref, bs_ref,
                    kk_ref, kvc_ref, qt_ref, vt_ref, gates_ref, mlp_ref):
    tm = x_ref.shape[0]
    h = (x_ref[...] * (1.0 + sc_ref[...]) + sh_ref[...]).astype(BF16)
    uv = _nt(wt_ref[0:ROW_Q, :], h)
    t = _nt(wt_ref[ROW_Q:, :], h)
    qt_ref[...] = (t[:ATTN_WIDTH] * Q_SCALE).astype(BF16)
    vt_ref[...] = t[ROW_VSLC - ROW_Q:ROW_GATE - ROW_Q].astype(BF16)
    gates_ref[...] = t[ROW_GATE - ROW_Q:].reshape(N_KV_HEADS, GATE_ROWS, tm)
    a = _nn(h, wa_ref[...])
    for k in range(N_KV_HEADS):
        kk_ref[:, k * LANES:k * LANES + HEAD_DIM] = a[:, k * HEAD_DIM:(k + 1) * HEAD_DIM].astype(BF16)
        kk_ref[:, k * LANES + HEAD_DIM:(k + 1) * LANES] = jnp.zeros((tm, LANES - HEAD_DIM), BF16)
    kk_ref[:, N_KV_HEADS * LANES:] = a[:, KV_WIDTH:2 * KV_WIDTH].astype(BF16)
    kvc_ref[0] = a[:, 2 * KV_WIDTH:3 * KV_WIDTH].astype(BF16)
    kvc_ref[1] = a[:, 3 * KV_WIDTH:].astype(BF16)

    u = _gelu(uv[:ROW_V])
    v = _gelu(uv[ROW_V:]).reshape(N_MLP_GROUPS, MLP_GROUP_DIM, tm)
    mu = jnp.mean(v, axis=1, keepdims=True)
    vc = v - mu
    var = jnp.mean(vc * vc, axis=1, keepdims=True)
    vn = vc * lax.rsqrt(var + LN_EPS) * vng_ref[...] + vnb_ref[...]
    n_chunks = tm // CHUNK
    s_idx = lax.broadcasted_iota(jnp.int32, (CHUNK, CHUNK), 0)
    t_idx = lax.broadcasted_iota(jnp.int32, (CHUNK, CHUNK), 1)
    for g in range(N_MLP_GROUPS):
        vg = vn[g].astype(BF16)
        stack = jnp.concatenate([vg[:, c * CHUNK:(c + 1) * CHUNK] for c in range(n_chunks)], axis=0)
        w_t = jnp.where(s_idx <= t_idx, wst_ref[g], 0.0).astype(BF16)
        sv = _nn(stack, w_t)
        bias = bs_ref[g:g + 1, :]
        for c in range(n_chunks):
            mlp_ref[g * MLP_GROUP_DIM:(g + 1) * MLP_GROUP_DIM, c * CHUNK:(c + 1) * CHUNK] = (
                u[g * MLP_GROUP_DIM:(g + 1) * MLP_GROUP_DIM, c * CHUNK:(c + 1) * CHUNK]
                * (sv[c * MLP_GROUP_DIM:(c + 1) * MLP_GROUP_DIM] + bias)).astype(BF16)


def _in_proj_call(x, sc, sh, w_a, w_t, vn_g, vn_b, w_st, b_s):
    B, S, D = x.shape
    tm = min(2 * TOK_TILE, S)
    const2 = lambda b, i: (0, 0)
    const3 = lambda b, i: (0, 0, 0)
    cm = lambda rows: pl.BlockSpec((None, rows, tm), lambda b, i: (b, 0, i))
    return pl.pallas_call(
        _in_proj_kernel,
        grid=(B, S // tm),
        in_specs=[pl.BlockSpec((None, tm, D), lambda b, i: (b, i, 0)),
                  pl.BlockSpec((None, 1, D), lambda b, i: (b, 0, 0)),
                  pl.BlockSpec((None, 1, D), lambda b, i: (b, 0, 0)),
                  pl.BlockSpec(w_a.shape, const2),
                  pl.BlockSpec(w_t.shape, const2),
                  pl.BlockSpec(vn_g.shape, const3),
                  pl.BlockSpec(vn_b.shape, const3),
                  pl.BlockSpec(w_st.shape, const3),
                  pl.BlockSpec(b_s.shape, const2)],
        out_specs=[pl.BlockSpec((None, tm, KK_WIDTH), lambda b, i: (b, i, 0)),
                   pl.BlockSpec((2, None, tm, KV_WIDTH), lambda b, i: (0, b, i, 0)),
                   cm(ATTN_WIDTH), cm(2 * KV_WIDTH),
                   pl.BlockSpec((None, N_KV_HEADS, GATE_ROWS, tm), lambda b, i: (b, 0, 0, i)), cm(MLP_WIDTH)],
        out_shape=[jax.ShapeDtypeStruct((B, S, KK_WIDTH), BF16),
                   jax.ShapeDtypeStruct((2, B, S, KV_WIDTH), BF16),
                   jax.ShapeDtypeStruct((B, ATTN_WIDTH, S), BF16),
                   jax.ShapeDtypeStruct((B, 2 * KV_WIDTH, S), BF16),
                   jax.ShapeDtypeStruct((B, N_KV_HEADS, GATE_ROWS, S), F32),
                   jax.ShapeDtypeStruct((B, MLP_WIDTH, S), BF16)],
        compiler_params=_params("arbitrary", "arbitrary"),
        name="in_proj",
    )(x, sc, sh, w_a, w_t, vn_g, vn_b, w_st, b_s)


def _compress_kernel(tok_ref, wbig_ref, w1_ref, pos_ref, w2_ref, out_ref, out_t_ref):
    ab = _nn(tok_ref[...], wbig_ref[...])
    bias = _nn(pos_ref[...].astype(BF16), w1_ref[...])[0:1]
    ngrp = ab.shape[0]
    for h in range(N_KV_HEADS):
        base = h * 2 * CMP_HIDDEN
        first = ab[:, base:base + CMP_HIDDEN]
        second = ab[:, base + CMP_HIDDEN:base + 2 * CMP_HIDDEN]
        hid = _gelu(first + pltpu.roll(second, ngrp - 1, 0) + bias)
        c = _nn(hid.astype(BF16), w2_ref[...])
        block_end = CMP_STRIDE * lax.broadcasted_iota(jnp.int32, (ngrp, HEAD_DIM), 0) + (CMP_LEN - 1)
        out_ref[h, :, 0:HEAD_DIM] = c.astype(BF16)
        out_ref[h, :, HEAD_DIM:] = _alibi_key_cols(block_end).astype(BF16)
        out_t_ref[h] = c.T.astype(BF16)


def _compress_call(tok, wbig, w1, pos, w2):
    two, B, ngrp, flat = tok.shape
    return pl.pallas_call(
        _compress_kernel,
        grid=(two, B),
        in_specs=[pl.BlockSpec((None, None, ngrp, flat), lambda t, b: (t, b, 0, 0)),
                  pl.BlockSpec((None,) + wbig.shape[1:], lambda t, b: (t, 0, 0)),
                  pl.BlockSpec((None,) + w1.shape[1:], lambda t, b: (t, 0, 0)),
                  pl.BlockSpec((None,) + pos.shape[1:], lambda t, b: (t, 0, 0)),
                  pl.BlockSpec((None,) + w2.shape[1:], lambda t, b: (t, 0, 0))],
        out_specs=[pl.BlockSpec((None, None, N_KV_HEADS, ngrp, 2 * HEAD_DIM), lambda t, b: (t, b, 0, 0, 0)),
                   pl.BlockSpec((None, None, N_KV_HEADS, HEAD_DIM, ngrp), lambda t, b: (t, b, 0, 0, 0))],
        out_shape=[jax.ShapeDtypeStruct((two, B, N_KV_HEADS, ngrp, 2 * HEAD_DIM), BF16),
                   jax.ShapeDtypeStruct((two, B, N_KV_HEADS, HEAD_DIM, ngrp), BF16)],
        compiler_params=_params("arbitrary", "arbitrary"),
        name="compress",
    )(tok, wbig, w1, pos, w2)


def _cmp_attn_kernel(qt_ref, kc_ref, vct_ref, gates_ref, oc_ref, bias_ref, imp_ref, score_ref, left_ref,
                     *, n_slc, topk):
    k_head = pl.program_id(1)
    i = pl.program_id(2)
    tq = qt_ref.shape[1]
    ncmp = kc_ref.shape[0]
    t0 = i * tq
    any_ok = (t0 + lax.broadcasted_iota(jnp.int32, (1, tq), 1)) >= (CMP_LEN - 1)
    t0f = t0.astype(F32)
    q_aug = jnp.concatenate(
        [jnp.concatenate([qt_ref[g * HEAD_DIM:(g + 1) * HEAD_DIM, :] for g in range(GQA_GROUP)], axis=1),
         jnp.concatenate([_alibi_query_rows(_slope(k_head, g), t0f, HEAD_DIM, tq) for g in range(GQA_GROUP)],
                         axis=1).astype(BF16)], axis=0)

    def attend(nrows):
        t_col = t0 + lax.broadcasted_iota(jnp.int32, (nrows, tq), 1)
        cmp_end = CMP_STRIDE * lax.broadcasted_iota(jnp.int32, (nrows, tq), 0) + (CMP_LEN - 1)
        ok = t_col >= cmp_end
        kc = kc_ref[0:nrows, :]
        ones = jnp.where(lax.broadcasted_iota(jnp.int32, (16, nrows), 0) == 0, 1.0, 0.0).astype(BF16)
        v_aug = jnp.concatenate([vct_ref[:, 0:nrows], ones], axis=0)
        s_heads = [jnp.where(ok, _nn(kc, q_aug[:, g * tq:(g + 1) * tq]), NEG) for g in range(GQA_GROUP)]
        p_sum = None
        for g, s in enumerate(s_heads):
            e = jnp.exp2(s - jnp.max(s, axis=0, keepdims=True))
            o = _nn(v_aug, e.astype(BF16))
            inv = jnp.where(any_ok, 1.0 / o[HEAD_DIM:HEAD_DIM + 1], 0.0)
            p = e * inv
            p_sum = p if p_sum is None else p_sum + p
            gate = _sigmoid(gates_ref[g:g + 1, :])
            oc_ref[g * HEAD_DIM:(g + 1) * HEAD_DIM, :] = o[:HEAD_DIM] * (inv * gate)
        j_i = lax.broadcasted_iota(jnp.int32, (MAX_SLC_BLOCKS, nrows), 0)
        n_i = lax.broadcasted_iota(jnp.int32, (MAX_SLC_BLOCKS, nrows), 1)
        overlap = ((CMP_STRIDE * n_i <= SLC_LEN * j_i + (SLC_LEN - 1))
                   & (CMP_STRIDE * n_i + (CMP_LEN - 1) >= SLC_LEN * j_i)).astype(BF16)
        p_hi = p_sum.astype(BF16)
        r1 = p_sum - p_hi.astype(F32)
        p_mid = r1.astype(BF16)
        p_lo = (r1 - p_mid.astype(F32)).astype(BF16)
        imp_ref[...] = _nn(overlap, p_hi) + _nn(overlap, p_mid) + _nn(overlap, p_lo)

    n_chunks = ncmp // CMP_ROW_CHUNK
    last_chunk = jnp.minimum(((t0 + tq - CMP_LEN) // CMP_STRIDE) // CMP_ROW_CHUNK, n_chunks - 1)
    for c in range(n_chunks):
        pl.when(last_chunk == c)(functools.partial(attend, (c + 1) * CMP_ROW_CHUNK))
    score = imp_ref[...]

    j_row = lax.broadcasted_iota(jnp.int32, (MAX_SLC_BLOCKS, tq), 0)
    t_blk = t0 + lax.broadcasted_iota(jnp.int32, (MAX_SLC_BLOCKS, tq), 1)
    cur = t_blk // SLC_LEN
    valid = (SLC_LEN * j_row <= t_blk) & (j_row < n_slc)
    forced = (j_row == 0) | (j_row == cur) | (j_row == cur - 1)
    cand = valid & jnp.logical_not(forced)
    n_forced = 1 + (cur[0:1] >= 1).astype(jnp.int32) + (cur[0:1] >= 2).astype(jnp.int32)
    remaining = jnp.broadcast_to(topk - n_forced, (8, tq))
    score = jnp.where(cand, score, -jnp.inf)
    n_groups = tq // PICK_LANES
    for lb in range(n_groups):
        score_ref[lb] = score[:, lb * PICK_LANES:(lb + 1) * PICK_LANES]
        left_ref[lb] = remaining[:, lb * PICK_LANES:(lb + 1) * PICK_LANES]
    j_f = lax.broadcasted_iota(jnp.int32, (MAX_SLC_BLOCKS, LANES), 0).astype(F32)

    def pick_rounds(lb, rounds, nrow):
        halves = [score_ref[lb, 0:nrow, h * LANES:(h + 1) * LANES] for h in range(PICK_LANES // LANES)]
        left = [left_ref[lb, 0:1, h * LANES:(h + 1) * LANES] for h in range(PICK_LANES // LANES)]
        j_n = j_f[0:nrow]
        for r in rounds:
            for h, sc in enumerate(halves):
                m = jnp.max(sc, axis=0, keepdims=True)
                first = jnp.min(jnp.where(sc == m, j_n, float(MAX_SLC_BLOCKS)), axis=0, keepdims=True)
                first = jnp.where(left[h] > r, first, -1.0)
                halves[h] = jnp.where(j_n == first, -jnp.inf, sc)
        score_ref[lb, 0:nrow, :] = jnp.concatenate(halves, axis=1)

    base_rounds = topk - 3
    blocks_started = (t0 + tq) // SLC_LEN
    row_buckets = [r for r in (32, 64) if r < MAX_SLC_BLOCKS] + [MAX_SLC_BLOCKS]
    for lo, nrow in zip([0] + row_buckets[:-1], row_buckets):
        in_bucket = (blocks_started > lo) if nrow == MAX_SLC_BLOCKS else ((blocks_started > lo) & (blocks_started <= nrow))

        @pl.when(in_bucket)
        def _pick(nrow=nrow):
            def group(lb, carry):
                pick_rounds(lb, range(base_rounds), nrow)
                return carry

            lax.fori_loop(0, n_groups, group, 0)

    @pl.when(t0 < 2 * SLC_LEN)
    def _early_queries():
        pick_rounds(0, range(base_rounds, topk - 1), row_buckets[0])

    picked = jnp.concatenate([score_ref[lb] for lb in range(n_groups)], axis=1) == -jnp.inf
    bias_ref[...] = jnp.where((valid & forced) | (cand & picked), 0.0, NEG).astype(BF16)


def _cmp_attn_call(qt, cmp_kv, cmp_kv_t, gates, *, n_slc, topk):
    B, _, S = qt.shape
    ncmp = cmp_kv.shape[3]
    tq = min(CMP_Q_TILE, S)
    assert topk >= 3 and tq >= 2 * SLC_LEN
    kern = functools.partial(_cmp_attn_kernel, n_slc=n_slc, topk=topk)
    return pl.pallas_call(
        kern,
        grid=(B, N_KV_HEADS, S // tq),
        in_specs=[pl.BlockSpec((None, GROUP_WIDTH, tq), lambda b, k, i: (b, k, i)),
                  pl.BlockSpec((None, None, None, ncmp, 2 * HEAD_DIM), lambda b, k, i: (0, b, k, 0, 0)),
                  pl.BlockSpec((None, None, None, HEAD_DIM, ncmp), lambda b, k, i: (1, b, k, 0, 0)),
                  pl.BlockSpec((None, None, GATE_ROWS, tq), lambda b, k, i: (b, k, 0, i))],
        out_specs=[pl.BlockSpec((None, GROUP_WIDTH, tq), lambda b, k, i: (b, k, i)),
                   pl.BlockSpec((None, None, MAX_SLC_BLOCKS, tq), lambda b, k, i: (b, k, 0, i))],
        out_shape=[jax.ShapeDtypeStruct((B, ATTN_WIDTH, S), F32),
                   jax.ShapeDtypeStruct((B, N_KV_HEADS, MAX_SLC_BLOCKS, S), BF16)],
        scratch_shapes=[pltpu.VMEM((MAX_SLC_BLOCKS, tq), F32),
                        pltpu.VMEM((tq // PICK_LANES, MAX_SLC_BLOCKS, PICK_LANES), F32),
                        pltpu.VMEM((tq // PICK_LANES, 8, PICK_LANES), jnp.int32)],
        compiler_params=_params("arbitrary", "arbitrary", "arbitrary"),
        name="cmp_attn",
    )(qt, cmp_kv, cmp_kv_t, gates)


V_ROWS = HEAD_DIM + 16
SLC_HEAD_GROUPS = ((0, 1), (2, 3))
SKIP_MARGIN = 176.0


def _slc_attn_kernel(qt_ref, bias_ref, ks_ref, vt_ref, gates_ref, other_ref, attn_ref,
                     kaug_ref, vh_ref, qaug_ref, m_ref, acc_ref, sa_ref, sb_ref, ta_ref, tb_ref, ksq_ref):
    k_head = pl.program_id(1)
    i = pl.program_id(2)
    tq = qt_ref.shape[1]
    n_tiles = kaug_ref.shape[0]
    t0 = i * tq

    @pl.when(i == 0)
    def _build_keys():
        def fill(c, k_sq_max):
            r0 = pl.multiple_of(c * tq, tq)
            k_tile = ks_ref[pl.ds(r0, tq), 0:HEAD_DIM]
            kaug_ref[c, :, 0:HEAD_DIM] = k_tile
            k_f = k_tile.astype(F32)
            return jnp.maximum(k_sq_max, jnp.max(jnp.sum(k_f * k_f, axis=1, keepdims=True)))

        ksq_ref[0] = lax.fori_loop(0, n_tiles, fill, jnp.float32(0.0))
        for c in range(n_tiles):
            vh_ref[c, 0:HEAD_DIM, :] = _pick_head_rows(vt_ref[:, c * tq:(c + 1) * tq], k_head)

    @pl.when((i == 0) & (k_head == 0) & (pl.program_id(0) == 0))
    def _build_static():
        def fill(c, carry):
            r0 = pl.multiple_of(c * tq, tq)
            pos = r0 + lax.broadcasted_iota(jnp.int32, (tq, HEAD_DIM), 0)
            kaug_ref[c, :, HEAD_DIM:2 * HEAD_DIM] = _alibi_key_cols(pos).astype(BF16)
            blk = (r0 + lax.broadcasted_iota(jnp.int32, (tq, MAX_SLC_BLOCKS), 0)) // SLC_LEN
            hot = blk == lax.broadcasted_iota(jnp.int32, (tq, MAX_SLC_BLOCKS), 1)
            kaug_ref[c, :, 2 * HEAD_DIM:] = jnp.where(hot, 1.0, 0.0).astype(BF16)
            return carry

        lax.fori_loop(0, n_tiles, fill, 0)
        ones_row = lax.broadcasted_iota(jnp.int32, (V_ROWS - HEAD_DIM, tq), 0) == 0
        for c in range(n_tiles):
            vh_ref[c, HEAD_DIM:, :] = jnp.where(ones_row, 1.0, 0.0).astype(BF16)

    t0f = t0.astype(F32)
    bias = bias_ref[...]
    first_tile = []
    for g in range(GQA_GROUP):
        c_g = _slope(k_head, g)
        q_g = qt_ref[g * HEAD_DIM:(g + 1) * HEAD_DIM, :]
        aux = _alibi_query_rows(c_g, t0f, HEAD_DIM, tq)
        qaug_ref[0:HEAD_DIM, g * tq:(g + 1) * tq] = q_g
        qaug_ref[HEAD_DIM:2 * HEAD_DIM, g * tq:(g + 1) * tq] = aux.astype(BF16)
        qaug_ref[2 * HEAD_DIM:, g * tq:(g + 1) * tq] = bias
        q_f = q_g.astype(F32)
        qk = jnp.sqrt(jnp.max(jnp.sum(q_f * q_f, axis=0, keepdims=True), axis=1, keepdims=True) * ksq_ref[0])
        dist_needed = jnp.minimum((2.0 * qk[0, 0] + SKIP_MARGIN) / c_g, 1e6).astype(jnp.int32)
        first_tile.append(i - jnp.minimum(i, (dist_needed - 1) // tq + 1))
    m_ref[...] = jnp.full(m_ref.shape, NEG, F32)
    acc_ref[...] = jnp.zeros(acc_ref.shape, F32)

    s_bufs = (sa_ref, sb_ref)

    tmax_bufs = (ta_ref, tb_ref)

    def scores_head(k_tile, g, slot, causal):
        cols = slice(g * tq, (g + 1) * tq)
        s = _nn(k_tile, qaug_ref[:, cols])
        if causal:
            key = lax.broadcasted_iota(jnp.int32, (tq, tq), 0)
            qry = lax.broadcasted_iota(jnp.int32, (tq, tq), 1)
            s = jnp.where(key <= qry, s, NEG)
        s_bufs[slot][:, cols] = s
        tmax_bufs[slot][:, cols] = jnp.max(s, axis=0, keepdims=True)

    def scores(j, slot, causal, heads):
        k_tile = kaug_ref[j]
        for g in heads:
            scores_head(k_tile, g, slot, causal)

    def step(j, slot, prefetch, heads):
        v_tile = vh_ref[j]
        k_next = kaug_ref[j + 1] if prefetch is not None else None
        for g in heads:
            cols = slice(g * tq, (g + 1) * tq)
            if prefetch is not None:
                scores_head(k_next, g, 1 - slot, prefetch == 'causal')
            m_old = m_ref[:, cols]
            m_new = jnp.maximum(m_old, tmax_bufs[slot][:, cols])
            alpha = jnp.exp2(m_old - m_new)
            p = jnp.exp2(s_bufs[slot][:, cols] - m_new).astype(BF16)
            acc_ref[:, cols] = alpha * acc_ref[:, cols] + _nn(v_tile, p)
            m_ref[:, cols] = m_new

    def run(heads, lo):
        n_plain = i - lo

        @pl.when(n_plain == 0)
        def _first_is_causal():
            scores(i, 0, True, heads)

        @pl.when(n_plain > 0)
        def _first_is_plain():
            scores(lo, 0, False, heads)

        def trip(p, carry):
            for u in range(SLC_UNROLL):
                step(lo + SLC_UNROLL * p + u, u % 2, 'plain', heads)
            return carry

        n_trips = jnp.maximum(n_plain - 1, 0) // SLC_UNROLL
        lax.fori_loop(0, n_trips, trip, 0)
        done = SLC_UNROLL * n_trips
        for n_left in range(1, SLC_UNROLL + 2):
            @pl.when(n_plain - done + 1 == n_left)
            def _tail(n_left=n_left):
                for u in range(n_left):
                    prefetch = (None, 'causal')[u == n_left - 2] if u >= n_left - 2 else 'plain'
                    step(lo + done + u, u % 2, prefetch, heads)

    starts = [functools.reduce(jnp.minimum, [first_tile[g] for g in heads]) for heads in SLC_HEAD_GROUPS]
    same_start = functools.reduce(jnp.logical_and, [s == starts[0] for s in starts[1:]])

    @pl.when(same_start)
    def _one_pass():
        run(tuple(range(GQA_GROUP)), starts[0])

    @pl.when(jnp.logical_not(same_start))
    def _pass_per_group():
        for heads, lo in zip(SLC_HEAD_GROUPS, starts):
            run(heads, lo)

    o = acc_ref[0:HEAD_DIM, :] / acc_ref[HEAD_DIM:HEAD_DIM + 1, :]
    for g in range(GQA_GROUP):
        gate = _sigmoid(gates_ref[GQA_GROUP + g:GQA_GROUP + g + 1, :])
        rows = slice(g * HEAD_DIM, (g + 1) * HEAD_DIM)
        attn_ref[rows, :] = (other_ref[rows, :] + o[:, g * tq:(g + 1) * tq] * gate).astype(BF16)


def _slc_attn_call(qt, bias, kk, vt, gates, other):
    B, _, S = qt.shape
    tq = min(SLC_TILE, S)
    return pl.pallas_call(
        _slc_attn_kernel,
        grid=(B, N_KV_HEADS, S // tq),
        in_specs=[pl.BlockSpec((None, GROUP_WIDTH, tq), lambda b, k, i: (b, k, i)),
                  pl.BlockSpec((None, None, MAX_SLC_BLOCKS, tq), lambda b, k, i: (b, k, 0, i)),
                  pl.BlockSpec((None, S, LANES), lambda b, k, i: (b, 0, k)),
                  pl.BlockSpec((None, KV_WIDTH, S), lambda b, k, i: (b, 0, 0)),
                  pl.BlockSpec((None, None, GATE_ROWS, tq), lambda b, k, i: (b, k, 0, i)),
                  pl.BlockSpec((None, GROUP_WIDTH, tq), lambda b, k, i: (b, k, i))],
        out_specs=pl.BlockSpec((None, GROUP_WIDTH, tq), lambda b, k, i: (b, k, i)),
        out_shape=jax.ShapeDtypeStruct((B, ATTN_WIDTH, S), BF16),
        scratch_shapes=[pltpu.VMEM((S // tq, tq, 2 * LANES), BF16),
                        pltpu.VMEM((S // tq, V_ROWS, tq), BF16),
                        pltpu.VMEM((2 * LANES, GQA_GROUP * tq), BF16),
                        pltpu.VMEM((1, GQA_GROUP * tq), F32),
                        pltpu.VMEM((V_ROWS, GQA_GROUP * tq), F32),
                        pltpu.VMEM((tq, GQA_GROUP * tq), F32),
                        pltpu.VMEM((tq, GQA_GROUP * tq), F32),
                        pltpu.VMEM((1, GQA_GROUP * tq), F32),
                        pltpu.VMEM((1, GQA_GROUP * tq), F32),
                        pltpu.SMEM((1,), F32)],
        compiler_params=_params("arbitrary", "arbitrary", "arbitrary"),
        name="slc_attn",
    )(qt, bias, kk, vt, gates, other)


def _win_attn_kernel(qt_ref, kprev_ref, kcur_ref, vprev_ref, vcur_ref, gates_ref, other_ref, ow_ref):
    k_head = pl.program_id(1)
    i = pl.program_id(2)
    tq = qt_ref.shape[1] // 2
    key = lax.broadcasted_iota(jnp.int32, (tq, tq), 0)
    qry = lax.broadcasted_iota(jnp.int32, (tq, tq), 1)
    keeps = (qry < key, None, key <= qry)
    ones = jnp.where(lax.broadcasted_iota(jnp.int32, (16, tq), 0) == 0, 1.0, 0.0).astype(BF16)
    lane = lax.broadcasted_iota(jnp.int32, (tq, KV_WIDTH), 1)
    cols_in_tile = _alibi_key_cols(lax.broadcasted_iota(jnp.int32, (tq, KV_WIDTH), 0))
    high_lanes = jnp.where((lane == 0) | (lane == 2), 1.0, 0.0)
    flag_lane = jnp.where(lane == 8, 1.0, 0.0)
    k_augs, v_tiles = [], []
    for n, (k_ref, v_ref) in enumerate(((kprev_ref, vprev_ref), (kprev_ref, vprev_ref),
                                        (kcur_ref, vcur_ref), (kcur_ref, vcur_ref))):
        half = slice((n % 2) * tq, (n % 2 + 1) * tq)
        tile = 2 * i - 2 + n
        k_aux = (cols_in_tile + high_lanes * (tile * (tq // LANES)).astype(F32)
                 + flag_lane * jnp.where(tile < 0, 1.0, 0.0))
        k_augs.append(jnp.concatenate([k_ref[half, :], k_aux.astype(BF16)], axis=1))
        v_tiles.append(jnp.concatenate([v_ref[:, half], ones], axis=0))

    q2s = []
    for h in range(2):
        cols = slice(h * tq, (h + 1) * tq)
        t0f = ((2 * i + h) * tq).astype(F32)
        qs = jnp.concatenate([qt_ref[g * HEAD_DIM:(g + 1) * HEAD_DIM, cols] for g in range(GQA_GROUP)], axis=1)
        alibi = jnp.concatenate([_alibi_query_rows(_slope(k_head, g), t0f, KV_WIDTH, tq, row8=NEG)
                                 for g in range(GQA_GROUP)], axis=1).astype(BF16)
        q2s.append(jnp.concatenate([jnp.where(k_head == k, qs, jnp.zeros_like(qs)) for k in range(N_KV_HEADS)]
                                   + [alibi], axis=0))
    mask = lambda keep, s: s if keep is None else jnp.where(keep, s, NEG)
    scores = [[[mask(keep, _nn(k_aug, q2s[h][:, g * tq:(g + 1) * tq]))
                for k_aug, keep in zip(k_augs[h:h + 3], keeps)]
               for g in range(GQA_GROUP)] for h in range(2)]
    for h in range(2):
        for g in range(GQA_GROUP):
            m = functools.reduce(jnp.maximum, [jnp.max(s, axis=0, keepdims=True) for s in scores[h][g]])
            o2 = sum(_nn(v, jnp.exp2(s - m).astype(BF16)) for v, s in zip(v_tiles[h:h + 3], scores[h][g]))
            o = _pick_head_rows(o2[:KV_WIDTH], k_head) / o2[KV_WIDTH:KV_WIDTH + 1]
            gate = _sigmoid(gates_ref[2 * GQA_GROUP + g:2 * GQA_GROUP + g + 1, h * tq:(h + 1) * tq])
            rows, cols = slice(g * HEAD_DIM, (g + 1) * HEAD_DIM), slice(h * tq, (h + 1) * tq)
            ow_ref[rows, cols] = other_ref[rows, cols] + o * gate


def _win_attn_call(qt, kk, vt, gates, other):
    B, _, S = qt.shape
    tq = min(Q_TILE, S)
    assert WINDOW == 2 * tq and S % (2 * tq) == 0
    k_spec = lambda back: pl.BlockSpec((None, 2 * tq, KV_WIDTH),
                                       lambda b, k, i: (b, jnp.maximum(i - back, 0), KK_COLBLK_WIN))
    v_spec = lambda back: pl.BlockSpec((None, KV_WIDTH, 2 * tq), lambda b, k, i: (b, 1, jnp.maximum(i - back, 0)))
    return pl.pallas_call(
        _win_attn_kernel,
        grid=(B, N_KV_HEADS, S // (2 * tq)),
        in_specs=[pl.BlockSpec((None, GROUP_WIDTH, 2 * tq), lambda b, k, i: (b, k, i)),
                  k_spec(1), k_spec(0), v_spec(1), v_spec(0),
                  pl.BlockSpec((None, None, GATE_ROWS, 2 * tq), lambda b, k, i: (b, k, 0, i)),
                  pl.BlockSpec((None, GROUP_WIDTH, 2 * tq), lambda b, k, i: (b, k, i))],
        out_specs=pl.BlockSpec((None, GROUP_WIDTH, 2 * tq), lambda b, k, i: (b, k, i)),
        out_shape=jax.ShapeDtypeStruct((B, ATTN_WIDTH, S), F32),
        compiler_params=_params("arbitrary", "arbitrary", "arbitrary"),
        name="win_attn",
    )(qt, kk, kk, vt, vt, gates, other)


def _group_rms(a, gain):
    c, tm = a.shape
    a3 = a.reshape(c // HEAD_DIM, HEAD_DIM, tm)
    ms = jnp.mean(a3 * a3, axis=1, keepdims=True)
    return (a3 * lax.rsqrt(ms + LN_EPS) * gain).reshape(c, tm)


def _out_proj_kernel(attn_ref, mlp_ref, x_ref, g1_ref, og_ref, wo_ref, lng_ref, lnb_ref, o_ref, *, alpha):
    n_attn = ATTN_WIDTH // HEAD_DIM
    attn = _group_rms(attn_ref[...].astype(F32), og_ref[:n_attn])
    mlp = _group_rms(mlp_ref[...].astype(F32), og_ref[n_attn:])
    y_t = jnp.concatenate([attn, mlp], axis=0).astype(BF16)
    y = _tn(y_t, wo_ref[...])
    z = alpha * x_ref[...] + (1.0 + g1_ref[...]) * y
    o_ref[...] = _layer_norm_rows(z, lng_ref[...], lnb_ref[...])


def _out_proj_call(attn, mlp, x, g1, out_g, w_o, ln_g, ln_b, *, alpha):
    B, S, D = x.shape
    tm = min(2 * TOK_TILE, S)
    cm = lambda width: pl.BlockSpec((None, width, tm), lambda b, i: (b, 0, i))
    row = pl.BlockSpec((1, D), lambda b, i: (0, 0))
    return pl.pallas_call(
        functools.partial(_out_proj_kernel, alpha=alpha),
        grid=(B, S // tm),
        in_specs=[cm(ATTN_WIDTH), cm(MLP_WIDTH),
                  pl.BlockSpec((None, tm, D), lambda b, i: (b, i, 0)),
                  pl.BlockSpec((None, 1, D), lambda b, i: (b, 0, 0)),
                  pl.BlockSpec(out_g.shape, lambda b, i: (0, 0, 0)),
                  pl.BlockSpec(w_o.shape, lambda b, i: (0, 0)),
                  row, row],
        out_specs=pl.BlockSpec((None, tm, D), lambda b, i: (b, i, 0)),
        out_shape=jax.ShapeDtypeStruct((B, S, D), F32),
        compiler_params=_params("arbitrary", "arbitrary"),
        name="out_proj",
    )(attn, mlp, x, g1, out_g, w_o, ln_g, ln_b)


def _ffn_kernel(x_ref, sc_ref, sh_ref, g2_ref, w1_ref, w3_ref, w2_ref, lng_ref, lnb_ref, o_ref, *, alpha, splits):
    x = x_ref[...]
    h = (x * (1.0 + sc_ref[...]) + sh_ref[...]).astype(BF16)
    f = None
    for lo, hi in splits:
        a = _nn(h, w1_ref[:, lo:hi])
        b = _nn(h, w3_ref[:, lo:hi])
        part = _nn((a * _sigmoid(a) * b).astype(BF16), w2_ref[lo:hi, :])
        f = part if f is None else f + part
    z = alpha * x + (1.0 + g2_ref[...]) * f
    o_ref[...] = _layer_norm_rows(z, lng_ref[...], lnb_ref[...])


def _ffn_call(x, sc, sh, g2, w1, w3, w2, ln_g, ln_b, *, alpha):
    B, S, D = x.shape
    d_ff = w1.shape[1]
    tm = min(TOK_TILE, S)
    half = (d_ff // 2 + 255) // 256 * 256
    splits = ((0, half), (half, d_ff))
    mod = pl.BlockSpec((None, 1, D), lambda b, i: (b, 0, 0))
    row = pl.BlockSpec((1, D), lambda b, i: (0, 0))
    resident = lambda shape: pl.BlockSpec(shape, lambda b, i: (0, 0), pipeline_mode=pl.Buffered(1))
    return pl.pallas_call(
        functools.partial(_ffn_kernel, alpha=alpha, splits=splits),
        grid=(B, S // tm),
        in_specs=[pl.BlockSpec((None, tm, D), lambda b, i: (b, i, 0)), mod, mod, mod,
                  resident(w1.shape), resident(w3.shape), resident(w2.shape), row, row],
        out_specs=pl.BlockSpec((None, tm, D), lambda b, i: (b, i, 0)),
        out_shape=jax.ShapeDtypeStruct((B, S, D), F32),
        compiler_params=_params("arbitrary", "arbitrary"),
        name="ffn",
    )(x, sc, sh, g2, w1, w3, w2, ln_g, ln_b)


def _in_proj_weights(w_in):
    sizes = (ATTN_WIDTH,) + (KV_WIDTH,) * 6 + (N_BRANCH * N_ATTN_HEADS, MLP_WIDTH, MLP_WIDTH)
    offs = [0]
    for s in sizes:
        offs.append(offs[-1] + s)
    col = lambda n: w_in[:, offs[n]:offs[n + 1]]
    q, kc, vc, ksl, vsl, kwn, vwn, gt, u, v = (col(n) for n in range(10))
    w_a = jnp.concatenate([ksl, kwn, kc, vc], axis=1).astype(BF16)
    D = w_in.shape[0]
    gt = gt.reshape(D, N_KV_HEADS, GQA_GROUP, N_BRANCH).transpose(1, 3, 2, 0)
    gt = gt.reshape(N_KV_HEADS, N_BRANCH * GQA_GROUP, D)
    gt = jnp.pad(gt, ((0, 0), (0, GATE_ROWS - N_BRANCH * GQA_GROUP), (0, 0)))
    gt = gt.reshape(N_KV_HEADS * GATE_ROWS, D)
    w_t = jnp.concatenate([u.T, v.T, q.T, vsl.T, vwn.T, gt], axis=0).astype(BF16)
    return w_a, w_t


def _compress_weights(cmp_w1):
    two, _, hid = cmp_w1.shape
    w = cmp_w1.reshape(two, 2, CMP_STRIDE, HEAD_DIM, hid)
    eye = jnp.eye(N_KV_HEADS, dtype=cmp_w1.dtype)
    big = jnp.einsum('thpdc,kj->tpkdjhc', w, eye)
    return big.reshape(two, CMP_STRIDE * N_KV_HEADS * HEAD_DIM, N_KV_HEADS * 2 * hid).astype(BF16)


def _hybrid_layer(x, mod, w_in, cmp_pos, cmp_w1, cmp_w2, vn_g, vn_b, w_s, b_s, out_g, w_o, ln1_g, ln1_b,
                  w1, w3, w2, ln2_g, ln2_b, *, alpha):
    B, S, D = x.shape
    assert S % Q_TILE == 0 and S % TOK_TILE == 0 and S // SLC_LEN <= MAX_SLC_BLOCKS
    sh1, sc1, g1, sh2, sc2, g2 = (mod[:, None, n * D:(n + 1) * D] for n in range(6))
    w_a, w_t = _in_proj_weights(w_in)
    kk, kvc, qt, vt, gates, mlp = _in_proj_call(
        x, sc1, sh1, w_a, w_t,
        vn_g.reshape(N_MLP_GROUPS, MLP_GROUP_DIM, 1), vn_b.reshape(N_MLP_GROUPS, MLP_GROUP_DIM, 1),
        jnp.swapaxes(w_s, 1, 2), b_s)
    ngrp = S // CMP_STRIDE
    tok = kvc.reshape(2, B, ngrp, CMP_STRIDE * KV_WIDTH)
    pos = jnp.broadcast_to(cmp_pos.reshape(2, 1, CMP_LEN * HEAD_DIM), (2, 8, CMP_LEN * HEAD_DIM))
    cmp_kv, cmp_kv_t = _compress_call(tok, _compress_weights(cmp_w1), cmp_w1.astype(BF16), pos, cmp_w2.astype(BF16))
    n_slc = S // SLC_LEN
    oc, bias = _cmp_attn_call(qt, cmp_kv, cmp_kv_t, gates, n_slc=n_slc, topk=min(SLC_TOPK, n_slc))
    ocw = _win_attn_call(qt, kk, vt, gates, oc)
    attn = _slc_attn_call(qt, bias, kk, vt, gates, ocw)
    row = lambda a: a.reshape(1, D)
    x = _out_proj_call(attn, mlp, x, g1, out_g.reshape(-1, HEAD_DIM, 1), w_o.astype(BF16),
                       row(ln1_g), row(ln1_b), alpha=alpha)
    return _ffn_call(x, sc2, sh2, g2, w1.astype(BF16), w3.astype(BF16), w2.astype(BF16),
                     row(ln2_g), row(ln2_b), alpha=alpha)


def kernel(x, c, w_ada, b_ada, w_in, cmp_pos, cmp_w1, cmp_w2, vn_g, vn_b, w_s, b_s, out_g, w_o, ln1_g, ln1_b,
           w1, w3, w2, ln2_g, ln2_b):
    depth = w_ada.shape[0]
    alpha = (2.0 * depth) ** 0.25
    mod = _ada_call(c, w_ada, b_ada)
    for l in range(depth):
        x = _hybrid_layer(x, mod[l], w_in[l], cmp_pos[l], cmp_w1[l], cmp_w2[l], vn_g[l], vn_b[l], w_s[l], b_s[l],
                          out_g[l], w_o[l], ln1_g[l], ln1_b[l], w1[l], w3[l], w2[l], ln2_g[l], ln2_b[l], alpha=alpha)
    return x
```

```python
import functools
import math

import jax
import jax.numpy as jnp
from jax import lax
from jax.experimental import pallas as pl
from jax.experimental.pallas import tpu as pltpu

F32 = jnp.float32
BF16 = jnp.bfloat16

HEAD_DIM = 64
N_KV_HEADS = 2
GQA_GROUP = 4
N_ATTN_HEADS = N_KV_HEADS * GQA_GROUP
ATTN_WIDTH = N_ATTN_HEADS * HEAD_DIM
KV_WIDTH = N_KV_HEADS * HEAD_DIM
GROUP_WIDTH = GQA_GROUP * HEAD_DIM
N_BRANCH = 3
MLP_GROUP_DIM = 64
N_MLP_GROUPS = 8
MLP_WIDTH = N_MLP_GROUPS * MLP_GROUP_DIM
CMP_LEN = 32
CMP_STRIDE = 16
CMP_HIDDEN = 256
SLC_LEN = 64
SLC_TOPK = 16
WINDOW = 512
CHUNK = 128
LN_EPS = 1e-5
NEG = -1e30
FORCED_SCORE = 1e6

LOG2E = math.log2(math.e)
Q_SCALE = HEAD_DIM ** -0.5 * LOG2E

LANES = 128
MAX_SLC_BLOCKS = LANES
TOK_TILE = 512
Q_TILE = 256
SLC_TILE = 512
SLC_UNROLL = 2
CMP_Q_TILE = 512
PICK_LANES = 256
CMP_ROW_CHUNK = 128
VMEM_LIMIT = 56 * 1024 * 1024
GATE_ROWS = 16

ROW_V = MLP_WIDTH
ROW_Q = ROW_V + MLP_WIDTH
ROW_VSLC = ROW_Q + ATTN_WIDTH
ROW_VWIN = ROW_VSLC + KV_WIDTH
ROW_GATE = ROW_VWIN + KV_WIDTH
ROWS_T = ROW_GATE + N_KV_HEADS * GATE_ROWS
COLS_A = 4 * KV_WIDTH
KK_WIDTH = N_KV_HEADS * LANES + KV_WIDTH
KK_COLBLK_WIN = N_KV_HEADS


def _nt(a, b):
    return lax.dot_general(a, b, (((1,), (1,)), ((), ())), preferred_element_type=F32)


def _tn(a, b):
    return lax.dot_general(a, b, (((0,), (0,)), ((), ())), preferred_element_type=F32)


def _nn(a, b):
    return jnp.dot(a, b, preferred_element_type=F32)


def _gelu(x):
    return x * (0.5 * (1.0 + jnp.tanh(math.sqrt(2.0 / math.pi) * (x + 0.044715 * (x * x * x)))))


def _sigmoid(x):
    return 1.0 / (1.0 + jnp.exp(-x))


def _layer_norm_rows(z, g, b):
    mu = jnp.mean(z, axis=-1, keepdims=True)
    zc = z - mu
    var = jnp.mean(zc * zc, axis=-1, keepdims=True)
    return zc * lax.rsqrt(var + LN_EPS) * g + b


def _slope(kv_head, g):
    slope = lambda h: LOG2E * 2.0 ** (-8.0 * (h + 1) / N_ATTN_HEADS)
    val = jnp.asarray(slope(g), F32)
    for k in range(1, N_KV_HEADS):
        val = jnp.where(kv_head == k, jnp.asarray(slope(k * GQA_GROUP + g), F32), val)
    return val


def _pick_head_rows(x, kv_head):
    out = x[:HEAD_DIM]
    for k in range(1, N_KV_HEADS):
        out = jnp.where(kv_head == k, x[k * HEAD_DIM:(k + 1) * HEAD_DIM], out)
    return out


def _bf16_part(x):
    return x.astype(BF16).astype(F32)


def _alibi_key_cols(pos):
    lane = lax.broadcasted_iota(jnp.int32, pos.shape, 1)
    cols = jnp.where(lane % 2 == 0, pos // LANES, pos % LANES)
    return jnp.where(lane < 4, cols, jnp.where(lane < 8, 1, 0)).astype(F32)


def _alibi_query_rows(c, t0f, rows, tq, row8=0.0):
    shape = (8, tq)
    row = lax.broadcasted_iota(jnp.int32, shape, 0)
    c = jnp.full(shape, c, F32)
    out = jnp.zeros(shape, F32)
    for n, part in enumerate((_bf16_part(c), _bf16_part(c - _bf16_part(c)))):
        shift = part * t0f
        vals = {2 * n: LANES * part, 2 * n + 1: part, 4 + 2 * n: -_bf16_part(shift), 5 + 2 * n: -(shift - _bf16_part(shift))}
        for r, val in vals.items():
            out = jnp.where(row == r, val, out)
    tail = jnp.where(lax.broadcasted_iota(jnp.int32, (rows - 8, tq), 0) == 0, row8, 0.0)
    return jnp.concatenate([out, tail], axis=0)


def _params(*sem):
    return pltpu.CompilerParams(dimension_semantics=sem, vmem_limit_bytes=VMEM_LIMIT)


def _ada_kernel(c_ref, w_ref, b_ref, o_ref):
    c = c_ref[...]
    c_act = (c * _sigmoid(c)).astype(BF16)
    o_ref[0] = _nn(c_act, w_ref[0].astype(BF16)) + b_ref[0]


def _ada_call(c, w_ada, b_ada):
    L, D, D6 = w_ada.shape
    B = c.shape[0]
    return pl.pallas_call(
        _ada_kernel,
        grid=(L, D6 // D),
        in_specs=[pl.BlockSpec((B, D), lambda l, n: (0, 0)),
                  pl.BlockSpec((1, D, D), lambda l, n: (l, 0, n)),
                  pl.BlockSpec((1, 1, D), lambda l, n: (l, 0, n))],
        out_specs=pl.BlockSpec((1, B, D), lambda l, n: (l, 0, n)),
        out_shape=jax.ShapeDtypeStruct((L, B, D6), F32),
        compiler_params=_params("arbitrary", "arbitrary"),
        name="ada_mod",
    )(c, w_ada, b_ada.reshape(L, 1, D6))


def _in_proj_kernel(x_ref, sc_ref, sh_ref, wa_ref, wt_ref, vng_ref, vnb_ref, wst_ref, bs_ref,
                    kk_ref, kvc_ref, qt_ref, vt_ref, gates_ref, mlp_ref):
    tm = x_ref.shape[0]
    h = (x_ref[...] * (1.0 + sc_ref[...]) + sh_ref[...]).astype(BF16)
    uv = _nt(wt_ref[0:ROW_Q, :], h)
    t = _nt(wt_ref[ROW_Q:, :], h)
    qt_ref[...] = (t[:ATTN_WIDTH] * Q_SCALE).astype(BF16)
    vt_ref[...] = t[ROW_VSLC - ROW_Q:ROW_GATE - ROW_Q].astype(BF16)
    gates_ref[...] = t[ROW_GATE - ROW_Q:].reshape(N_KV_HEADS, GATE_ROWS, tm)
    a = _nn(h, wa_ref[...])
    for k in range(N_KV_HEADS):
        kk_ref[:, k * LANES:k * LANES + HEAD_DIM] = a[:, k * HEAD_DIM:(k + 1) * HEAD_DIM].astype(BF16)
        kk_ref[:, k * LANES + HEAD_DIM:(k + 1) * LANES] = jnp.zeros((tm, LANES - HEAD_DIM), BF16)
    kk_ref[:, N_KV_HEADS * LANES:] = a[:, KV_WIDTH:2 * KV_WIDTH].astype(BF16)
    kvc_ref[0] = a[:, 2 * KV_WIDTH:3 * KV_WIDTH].astype(BF16)
    kvc_ref[1] = a[:, 3 * KV_WIDTH:].astype(BF16)

    u = _gelu(uv[:ROW_V])
    v = _gelu(uv[ROW_V:]).reshape(N_MLP_GROUPS, MLP_GROUP_DIM, tm)
    mu = jnp.mean(v, axis=1, keepdims=True)
    vc = v - mu
    var = jnp.mean(vc * vc, axis=1, keepdims=True)
    vn = vc * lax.rsqrt(var + LN_EPS) * vng_ref[...] + vnb_ref[...]
    n_chunks = tm // CHUNK
    s_idx = lax.broadcasted_iota(jnp.int32, (CHUNK, CHUNK), 0)
    t_idx = lax.broadcasted_iota(jnp.int32, (CHUNK, CHUNK), 1)
    for g in range(N_MLP_GROUPS):
        vg = vn[g].astype(BF16)
        stack = jnp.concatenate([vg[:, c * CHUNK:(c + 1) * CHUNK] for c in range(n_chunks)], axis=0)
        w_t = jnp.where(s_idx <= t_idx, wst_ref[g], 0.0).astype(BF16)
        sv = _nn(stack, w_t)
        bias = bs_ref[g:g + 1, :]
        for c in range(n_chunks):
            mlp_ref[g * MLP_GROUP_DIM:(g + 1) * MLP_GROUP_DIM, c * CHUNK:(c + 1) * CHUNK] = (
                u[g * MLP_GROUP_DIM:(g + 1) * MLP_GROUP_DIM, c * CHUNK:(c + 1) * CHUNK]
                * (sv[c * MLP_GROUP_DIM:(c + 1) * MLP_GROUP_DIM] + bias)).astype(BF16)


def _in_proj_call(x, sc, sh, w_a, w_t, vn_g, vn_b, w_st, b_s):
    B, S, D = x.shape
    tm = min(2 * TOK_TILE, S)
    const2 = lambda b, i: (0, 0)
    const3 = lambda b, i: (0, 0, 0)
    cm = lambda rows: pl.BlockSpec((None, rows, tm), lambda b, i: (b, 0, i))
    return pl.pallas_call(
        _in_proj_kernel,
        grid=(B, S // tm),
        in_specs=[pl.BlockSpec((None, tm, D), lambda b, i: (b, i, 0)),
                  pl.BlockSpec((None, 1, D), lambda b, i: (b, 0, 0)),
                  pl.BlockSpec((None, 1, D), lambda b, i: (b, 0, 0)),
                  pl.BlockSpec(w_a.shape, const2),
                  pl.BlockSpec(w_t.shape, const2),
                  pl.BlockSpec(vn_g.shape, const3),
                  pl.BlockSpec(vn_b.shape, const3),
                  pl.BlockSpec(w_st.shape, const3),
                  pl.BlockSpec(b_s.shape, const2)],
        out_specs=[pl.BlockSpec((None, tm, KK_WIDTH), lambda b, i: (b, i, 0)),
                   pl.BlockSpec((2, None, tm, KV_WIDTH), lambda b, i: (0, b, i, 0)),
                   cm(ATTN_WIDTH), cm(2 * KV_WIDTH),
                   pl.BlockSpec((None, N_KV_HEADS, GATE_ROWS, tm), lambda b, i: (b, 0, 0, i)), cm(MLP_WIDTH)],
        out_shape=[jax.ShapeDtypeStruct((B, S, KK_WIDTH), BF16),
                   jax.ShapeDtypeStruct((2, B, S, KV_WIDTH), BF16),
                   jax.ShapeDtypeStruct((B, ATTN_WIDTH, S), BF16),
                   jax.ShapeDtypeStruct((B, 2 * KV_WIDTH, S), BF16),
                   jax.ShapeDtypeStruct((B, N_KV_HEADS, GATE_ROWS, S), F32),
                   jax.ShapeDtypeStruct((B, MLP_WIDTH, S), BF16)],
        compiler_params=_params("arbitrary", "arbitrary"),
        name="in_proj",
    )(x, sc, sh, w_a, w_t, vn_g, vn_b, w_st, b_s)


def _compress_kernel(tok_ref, wbig_ref, w1_ref, pos_ref, w2_ref, out_ref, out_t_ref):
    ab = _nn(tok_ref[...], wbig_ref[...])
    bias = _nn(pos_ref[...].astype(BF16), w1_ref[...])[0:1]
    ngrp = ab.shape[0]
    for h in range(N_KV_HEADS):
        base = h * 2 * CMP_HIDDEN
        first = ab[:, base:base + CMP_HIDDEN]
        second = ab[:, base + CMP_HIDDEN:base + 2 * CMP_HIDDEN]
        hid = _gelu(first + pltpu.roll(second, ngrp - 1, 0) + bias)
        c = _nn(hid.astype(BF16), w2_ref[...])
        block_end = CMP_STRIDE * lax.broadcasted_iota(jnp.int32, (ngrp, HEAD_DIM), 0) + (CMP_LEN - 1)
        out_ref[h, :, 0:HEAD_DIM] = c.astype(BF16)
        out_ref[h, :, HEAD_DIM:] = _alibi_key_cols(block_end).astype(BF16)
        out_t_ref[h] = c.T.astype(BF16)


def _compress_call(tok, wbig, w1, pos, w2):
    two, B, ngrp, flat = tok.shape
    return pl.pallas_call(
        _compress_kernel,
        grid=(two, B),
        in_specs=[pl.BlockSpec((None, None, ngrp, flat), lambda t, b: (t, b, 0, 0)),
                  pl.BlockSpec((None,) + wbig.shape[1:], lambda t, b: (t, 0, 0)),
                  pl.BlockSpec((None,) + w1.shape[1:], lambda t, b: (t, 0, 0)),
                  pl.BlockSpec((None,) + pos.shape[1:], lambda t, b: (t, 0, 0)),
                  pl.BlockSpec((None,) + w2.shape[1:], lambda t, b: (t, 0, 0))],
        out_specs=[pl.BlockSpec((None, None, N_KV_HEADS, ngrp, 2 * HEAD_DIM), lambda t, b: (t, b, 0, 0, 0)),
                   pl.BlockSpec((None, None, N_KV_HEADS, HEAD_DIM, ngrp), lambda t, b: (t, b, 0, 0, 0))],
        out_shape=[jax.ShapeDtypeStruct((two, B, N_KV_HEADS, ngrp, 2 * HEAD_DIM), BF16),
                   jax.ShapeDtypeStruct((two, B, N_KV_HEADS, HEAD_DIM, ngrp), BF16)],
        compiler_params=_params("arbitrary", "arbitrary"),
        name="compress",
    )(tok, wbig, w1, pos, w2)


def _cmp_attn_kernel(qt_ref, kc_ref, vct_ref, gates_ref, oc_ref, bias_ref, imp_ref, score_ref, left_ref,
                     *, n_slc, topk):
    k_head = pl.program_id(1)
    i = pl.program_id(2)
    tq = qt_ref.shape[1]
    ncmp = kc_ref.shape[0]
    t0 = i * tq
    any_ok = (t0 + lax.broadcasted_iota(jnp.int32, (1, tq), 1)) >= (CMP_LEN - 1)
    t0f = t0.astype(F32)
    q_aug = jnp.concatenate(
        [jnp.concatenate([qt_ref[g * HEAD_DIM:(g + 1) * HEAD_DIM, :] for g in range(GQA_GROUP)], axis=1),
         jnp.concatenate([_alibi_query_rows(_slope(k_head, g), t0f, HEAD_DIM, tq) for g in range(GQA_GROUP)],
                         axis=1).astype(BF16)], axis=0)

    def attend(nrows):
        t_col = t0 + lax.broadcasted_iota(jnp.int32, (nrows, tq), 1)
        cmp_end = CMP_STRIDE * lax.broadcasted_iota(jnp.int32, (nrows, tq), 0) + (CMP_LEN - 1)
        ok = t_col >= cmp_end
        kc = kc_ref[0:nrows, :]
        ones = jnp.where(lax.broadcasted_iota(jnp.int32, (16, nrows), 0) == 0, 1.0, 0.0).astype(BF16)
        v_aug = jnp.concatenate([vct_ref[:, 0:nrows], ones], axis=0)
        s_heads = [jnp.where(ok, _nn(kc, q_aug[:, g * tq:(g + 1) * tq]), NEG) for g in range(GQA_GROUP)]
        p_sum = None
        for g, s in enumerate(s_heads):
            e = jnp.exp2(s - jnp.max(s, axis=0, keepdims=True))
            o = _nn(v_aug, e.astype(BF16))
            inv = jnp.where(any_ok, 1.0 / o[HEAD_DIM:HEAD_DIM + 1], 0.0)
            p = e * inv
            p_sum = p if p_sum is None else p_sum + p
            gate = _sigmoid(gates_ref[g:g + 1, :])
            oc_ref[g * HEAD_DIM:(g + 1) * HEAD_DIM, :] = o[:HEAD_DIM] * (inv * gate)
        j_i = lax.broadcasted_iota(jnp.int32, (MAX_SLC_BLOCKS, nrows), 0)
        n_i = lax.broadcasted_iota(jnp.int32, (MAX_SLC_BLOCKS, nrows), 1)
        overlap = ((CMP_STRIDE * n_i <= SLC_LEN * j_i + (SLC_LEN - 1))
                   & (CMP_STRIDE * n_i + (CMP_LEN - 1) >= SLC_LEN * j_i)).astype(BF16)
        p_hi = p_sum.astype(BF16)
        r1 = p_sum - p_hi.astype(F32)
        p_mid = r1.astype(BF16)
        p_lo = (r1 - p_mid.astype(F32)).astype(BF16)
        imp_ref[...] = _nn(overlap, p_hi) + _nn(overlap, p_mid) + _nn(overlap, p_lo)

    n_chunks = ncmp // CMP_ROW_CHUNK
    last_chunk = jnp.minimum(((t0 + tq - CMP_LEN) // CMP_STRIDE) // CMP_ROW_CHUNK, n_chunks - 1)
    for c in range(n_chunks):
        pl.when(last_chunk == c)(functools.partial(attend, (c + 1) * CMP_ROW_CHUNK))
    score = imp_ref[...]

    j_row = lax.broadcasted_iota(jnp.int32, (MAX_SLC_BLOCKS, tq), 0)
    t_blk = t0 + lax.broadcasted_iota(jnp.int32, (MAX_SLC_BLOCKS, tq), 1)
    cur = t_blk // SLC_LEN
    valid = (SLC_LEN * j_row <= t_blk) & (j_row < n_slc)
    forced = (j_row == 0) | (j_row == cur) | (j_row == cur - 1)
    cand = valid & jnp.logical_not(forced)
    n_forced = 1 + (cur[0:1] >= 1).astype(jnp.int32) + (cur[0:1] >= 2).astype(jnp.int32)
    remaining = jnp.broadcast_to(topk - n_forced, (8, tq))
    score = jnp.where(cand, score, -jnp.inf)
    n_groups = tq // PICK_LANES
    for lb in range(n_groups):
        score_ref[lb] = score[:, lb * PICK_LANES:(lb + 1) * PICK_LANES]
        left_ref[lb] = remaining[:, lb * PICK_LANES:(lb + 1) * PICK_LANES]
    j_f = lax.broadcasted_iota(jnp.int32, (MAX_SLC_BLOCKS, LANES), 0).astype(F32)

    def pick_rounds(lb, rounds, nrow):
        halves = [score_ref[lb, 0:nrow, h * LANES:(h + 1) * LANES] for h in range(PICK_LANES // LANES)]
        left = [left_ref[lb, 0:1, h * LANES:(h + 1) * LANES] for h in range(PICK_LANES // LANES)]
        j_n = j_f[0:nrow]
        for r in rounds:
            for h, sc in enumerate(halves):
                m = jnp.max(sc, axis=0, keepdims=True)
                first = jnp.min(jnp.where(sc == m, j_n, float(MAX_SLC_BLOCKS)), axis=0, keepdims=True)
                first = jnp.where(left[h] > r, first, -1.0)
                halves[h] = jnp.where(j_n == first, -jnp.inf, sc)
        score_ref[lb, 0:nrow, :] = jnp.concatenate(halves, axis=1)

    base_rounds = topk - 3
    blocks_started = (t0 + tq) // SLC_LEN
    row_buckets = [r for r in (32, 64) if r < MAX_SLC_BLOCKS] + [MAX_SLC_BLOCKS]
    for lo, nrow in zip([0] + row_buckets[:-1], row_buckets):
        in_bucket = (blocks_started > lo) if nrow == MAX_SLC_BLOCKS else ((blocks_started > lo) & (blocks_started <= nrow))

        @pl.when(in_bucket)
        def _pick(nrow=nrow):
            def group(lb, carry):
                pick_rounds(lb, range(base_rounds), nrow)
                return carry

            lax.fori_loop(0, n_groups, group, 0)

    @pl.when(t0 < 2 * SLC_LEN)
    def _early_queries():
        pick_rounds(0, range(base_rounds, topk - 1), row_buckets[0])

    picked = jnp.concatenate([score_ref[lb] for lb in range(n_groups)], axis=1) == -jnp.inf
    bias_ref[...] = jnp.where((valid & forced) | (cand & picked), 0.0, NEG).astype(BF16)


def _cmp_attn_call(qt, cmp_kv, cmp_kv_t, gates, *, n_slc, topk):
    B, _, S = qt.shape
    ncmp = cmp_kv.shape[3]
    tq = min(CMP_Q_TILE, S)
    assert topk >= 3 and tq >= 2 * SLC_LEN
    kern = functools.partial(_cmp_attn_kernel, n_slc=n_slc, topk=topk)
    return pl.pallas_call(
        kern,
        grid=(B, N_KV_HEADS, S // tq),
        in_specs=[pl.BlockSpec((None, GROUP_WIDTH, tq), lambda b, k, i: (b, k, i)),
                  pl.BlockSpec((None, None, None, ncmp, 2 * HEAD_DIM), lambda b, k, i: (0, b, k, 0, 0)),
                  pl.BlockSpec((None, None, None, HEAD_DIM, ncmp), lambda b, k, i: (1, b, k, 0, 0)),
                  pl.BlockSpec((None, None, GATE_ROWS, tq), lambda b, k, i: (b, k, 0, i))],
        out_specs=[pl.BlockSpec((None, GROUP_WIDTH, tq), lambda b, k, i: (b, k, i)),
                   pl.BlockSpec((None, None, MAX_SLC_BLOCKS, tq), lambda b, k, i: (b, k, 0, i))],
        out_shape=[jax.ShapeDtypeStruct((B, ATTN_WIDTH, S), F32),
                   jax.ShapeDtypeStruct((B, N_KV_HEADS, MAX_SLC_BLOCKS, S), BF16)],
        scratch_shapes=[pltpu.VMEM((MAX_SLC_BLOCKS, tq), F32),
                        pltpu.VMEM((tq // PICK_LANES, MAX_SLC_BLOCKS, PICK_LANES), F32),
                        pltpu.VMEM((tq // PICK_LANES, 8, PICK_LANES), jnp.int32)],
        compiler_params=_params("arbitrary", "arbitrary", "arbitrary"),
        name="cmp_attn",
    )(qt, cmp_kv, cmp_kv_t, gates)


V_ROWS = HEAD_DIM + 16
SLC_HEAD_GROUPS = ((0, 1), (2, 3))
SKIP_MARGIN = 176.0


def _slc_attn_kernel(qt_ref, bias_ref, ks_ref, vt_ref, gates_ref, other_ref, attn_ref,
                     kaug_ref, vh_ref, qaug_ref, m_ref, acc_ref, sa_ref, sb_ref, ta_ref, tb_ref, ksq_ref):
    k_head = pl.program_id(1)
    i = pl.program_id(2)
    tq = qt_ref.shape[1]
    n_tiles = kaug_ref.shape[0]
    t0 = i * tq

    @pl.when(i == 0)
    def _build_keys():
        def fill(c, k_sq_max):
            r0 = pl.multiple_of(c * tq, tq)
            k_tile = ks_ref[pl.ds(r0, tq), 0:HEAD_DIM]
            kaug_ref[c, :, 0:HEAD_DIM] = k_tile
            k_f = k_tile.astype(F32)
            return jnp.maximum(k_sq_max, jnp.max(jnp.sum(k_f * k_f, axis=1, keepdims=True)))

        ksq_ref[0] = lax.fori_loop(0, n_tiles, fill, jnp.float32(0.0))
        for c in range(n_tiles):
            vh_ref[c, 0:HEAD_DIM, :] = _pick_head_rows(vt_ref[:, c * tq:(c + 1) * tq], k_head)

    @pl.when((i == 0) & (k_head == 0) & (pl.program_id(0) == 0))
    def _build_static():
        def fill(c, carry):
            r0 = pl.multiple_of(c * tq, tq)
            pos = r0 + lax.broadcasted_iota(jnp.int32, (tq, HEAD_DIM), 0)
            kaug_ref[c, :, HEAD_DIM:2 * HEAD_DIM] = _alibi_key_cols(pos).astype(BF16)
            blk = (r0 + lax.broadcasted_iota(jnp.int32, (tq, MAX_SLC_BLOCKS), 0)) // SLC_LEN
            hot = blk == lax.broadcasted_iota(jnp.int32, (tq, MAX_SLC_BLOCKS), 1)
            kaug_ref[c, :, 2 * HEAD_DIM:] = jnp.where(hot, 1.0, 0.0).astype(BF16)
            return carry

        lax.fori_loop(0, n_tiles, fill, 0)
        ones_row = lax.broadcasted_iota(jnp.int32, (V_ROWS - HEAD_DIM, tq), 0) == 0
        for c in range(n_tiles):
            vh_ref[c, HEAD_DIM:, :] = jnp.where(ones_row, 1.0, 0.0).astype(BF16)

    t0f = t0.astype(F32)
    bias = bias_ref[...]
    first_tile = []
    for g in range(GQA_GROUP):
        c_g = _slope(k_head, g)
        q_g = qt_ref[g * HEAD_DIM:(g + 1) * HEAD_DIM, :]
        aux = _alibi_query_rows(c_g, t0f, HEAD_DIM, tq)
        qaug_ref[0:HEAD_DIM, g * tq:(g + 1) * tq] = q_g
        qaug_ref[HEAD_DIM:2 * HEAD_DIM, g * tq:(g + 1) * tq] = aux.astype(BF16)
        qaug_ref[2 * HEAD_DIM:, g * tq:(g + 1) * tq] = bias
        q_f = q_g.astype(F32)
        qk = jnp.sqrt(jnp.max(jnp.sum(q_f * q_f, axis=0, keepdims=True), axis=1, keepdims=True) * ksq_ref[0])
        dist_needed = jnp.minimum((2.0 * qk[0, 0] + SKIP_MARGIN) / c_g, 1e6).astype(jnp.int32)
        first_tile.append(i - jnp.minimum(i, (dist_needed - 1) // tq + 1))
    m_ref[...] = jnp.full(m_ref.shape, NEG, F32)
    acc_ref[...] = jnp.zeros(acc_ref.shape, F32)

    s_bufs = (sa_ref, sb_ref)

    tmax_bufs = (ta_ref, tb_ref)

    def scores_head(k_tile, g, slot, causal):
        cols = slice(g * tq, (g + 1) * tq)
        s = _nn(k_tile, qaug_ref[:, cols])
        if causal is not False:
            key = lax.broadcasted_iota(jnp.int32, (tq, tq), 0)
            qry = lax.broadcasted_iota(jnp.int32, (tq, tq), 1)
            keep = (key <= qry) if causal is True else ((key <= qry) | jnp.logical_not(causal))
            s = jnp.where(keep, s, NEG)
        s_bufs[slot][:, cols] = s
        tmax_bufs[slot][:, cols] = jnp.max(s, axis=0, keepdims=True)

    def scores(j, slot, causal, heads):
        k_tile = kaug_ref[j]
        for g in heads:
            scores_head(k_tile, g, slot, causal)

    def step(j, slot, prefetch, heads):
        v_tile = vh_ref[j]
        k_next = kaug_ref[j + 1] if prefetch is not None else None
        for g in heads:
            cols = slice(g * tq, (g + 1) * tq)
            if prefetch is not None:
                scores_head(k_next, g, 1 - slot, prefetch == 'causal')
            m_old = m_ref[:, cols]
            m_new = jnp.maximum(m_old, tmax_bufs[slot][:, cols])
            alpha = jnp.exp2(m_old - m_new)
            p = jnp.exp2(s_bufs[slot][:, cols] - m_new).astype(BF16)
            acc_ref[:, cols] = alpha * acc_ref[:, cols] + _nn(v_tile, p)
            m_ref[:, cols] = m_new

    def run(heads, lo):
        n_plain = i - lo

        scores(lo, 0, n_plain == 0, heads)

        def trip(p, carry):
            for u in range(SLC_UNROLL):
                step(lo + SLC_UNROLL * p + u, u % 2, 'plain', heads)
            return carry

        n_trips = jnp.maximum(n_plain - 1, 0) // SLC_UNROLL
        lax.fori_loop(0, n_trips, trip, 0)
        done = SLC_UNROLL * n_trips
        for n_left in range(1, SLC_UNROLL + 2):
            @pl.when(n_plain - done + 1 == n_left)
            def _tail(n_left=n_left):
                for u in range(n_left):
                    prefetch = (None, 'causal')[u == n_left - 2] if u >= n_left - 2 else 'plain'
                    step(lo + done + u, u % 2, prefetch, heads)

    starts = [functools.reduce(jnp.minimum, [first_tile[g] for g in heads]) for heads in SLC_HEAD_GROUPS]
    same_start = functools.reduce(jnp.logical_and, [s == starts[0] for s in starts[1:]])

    @pl.when(same_start)
    def _one_pass():
        run(tuple(range(GQA_GROUP)), starts[0])

    @pl.when(jnp.logical_not(same_start))
    def _pass_per_group():
        for heads, lo in zip(SLC_HEAD_GROUPS, starts):
            run(heads, lo)

    o = acc_ref[0:HEAD_DIM, :] / acc_ref[HEAD_DIM:HEAD_DIM + 1, :]
    for g in range(GQA_GROUP):
        gate = _sigmoid(gates_ref[GQA_GROUP + g:GQA_GROUP + g + 1, :])
        rows = slice(g * HEAD_DIM, (g + 1) * HEAD_DIM)
        attn_ref[rows, :] = (other_ref[rows, :] + o[:, g * tq:(g + 1) * tq] * gate).astype(BF16)


def _slc_attn_call(qt, bias, kk, vt, gates, other):
    B, _, S = qt.shape
    tq = min(SLC_TILE, S)
    return pl.pallas_call(
        _slc_attn_kernel,
        grid=(B, N_KV_HEADS, S // tq),
        in_specs=[pl.BlockSpec((None, GROUP_WIDTH, tq), lambda b, k, i: (b, k, i)),
                  pl.BlockSpec((None, None, MAX_SLC_BLOCKS, tq), lambda b, k, i: (b, k, 0, i)),
                  pl.BlockSpec((None, S, LANES), lambda b, k, i: (b, 0, k)),
                  pl.BlockSpec((None, KV_WIDTH, S), lambda b, k, i: (b, 0, 0)),
                  pl.BlockSpec((None, None, GATE_ROWS, tq), lambda b, k, i: (b, k, 0, i)),
                  pl.BlockSpec((None, GROUP_WIDTH, tq), lambda b, k, i: (b, k, i))],
        out_specs=pl.BlockSpec((None, GROUP_WIDTH, tq), lambda b, k, i: (b, k, i)),
        out_shape=jax.ShapeDtypeStruct((B, ATTN_WIDTH, S), BF16),
        scratch_shapes=[pltpu.VMEM((S // tq, tq, 2 * LANES), BF16),
                        pltpu.VMEM((S // tq, V_ROWS, tq), BF16),
                        pltpu.VMEM((2 * LANES, GQA_GROUP * tq), BF16),
                        pltpu.VMEM((1, GQA_GROUP * tq), F32),
                        pltpu.VMEM((V_ROWS, GQA_GROUP * tq), F32),
                        pltpu.VMEM((tq, GQA_GROUP * tq), F32),
                        pltpu.VMEM((tq, GQA_GROUP * tq), F32),
                        pltpu.VMEM((1, GQA_GROUP * tq), F32),
                        pltpu.VMEM((1, GQA_GROUP * tq), F32),
                        pltpu.SMEM((1,), F32)],
        compiler_params=_params("arbitrary", "arbitrary", "arbitrary"),
        name="slc_attn",
    )(qt, bias, kk, vt, gates, other)


def _win_attn_kernel(qt_ref, kprev_ref, kcur_ref, vprev_ref, vcur_ref, gates_ref, other_ref, ow_ref):
    k_head = pl.program_id(1)
    i = pl.program_id(2)
    tq = qt_ref.shape[1] // 2
    key = lax.broadcasted_iota(jnp.int32, (tq, tq), 0)
    qry = lax.broadcasted_iota(jnp.int32, (tq, tq), 1)
    keeps = (qry < key, None, key <= qry)
    ones = jnp.where(lax.broadcasted_iota(jnp.int32, (16, tq), 0) == 0, 1.0, 0.0).astype(BF16)
    lane = lax.broadcasted_iota(jnp.int32, (tq, KV_WIDTH), 1)
    cols_in_tile = _alibi_key_cols(lax.broadcasted_iota(jnp.int32, (tq, KV_WIDTH), 0))
    high_lanes = jnp.where((lane == 0) | (lane == 2), 1.0, 0.0)
    flag_lane = jnp.where(lane == 8, 1.0, 0.0)
    k_augs, v_tiles = [], []
    for n, (k_ref, v_ref) in enumerate(((kprev_ref, vprev_ref), (kprev_ref, vprev_ref),
                                        (kcur_ref, vcur_ref), (kcur_ref, vcur_ref))):
        half = slice((n % 2) * tq, (n % 2 + 1) * tq)
        tile = 2 * i - 2 + n
        k_aux = (cols_in_tile + high_lanes * (tile * (tq // LANES)).astype(F32)
                 + flag_lane * jnp.where(tile < 0, 1.0, 0.0))
        k_augs.append(jnp.concatenate([k_ref[half, :], k_aux.astype(BF16)], axis=1))
        v_tiles.append(jnp.concatenate([v_ref[:, half], ones], axis=0))

    q2s = []
    for h in range(2):
        cols = slice(h * tq, (h + 1) * tq)
        t0f = ((2 * i + h) * tq).astype(F32)
        qs = jnp.concatenate([qt_ref[g * HEAD_DIM:(g + 1) * HEAD_DIM, cols] for g in range(GQA_GROUP)], axis=1)
        alibi = jnp.concatenate([_alibi_query_rows(_slope(k_head, g), t0f, KV_WIDTH, tq, row8=NEG)
                                 for g in range(GQA_GROUP)], axis=1).astype(BF16)
        q2s.append(jnp.concatenate([jnp.where(k_head == k, qs, jnp.zeros_like(qs)) for k in range(N_KV_HEADS)]
                                   + [alibi], axis=0))
    mask = lambda keep, s: s if keep is None else jnp.where(keep, s, NEG)
    scores = [[[mask(keep, _nn(k_aug, q2s[h][:, g * tq:(g + 1) * tq]))
                for k_aug, keep in zip(k_augs[h:h + 3], keeps)]
               for g in range(GQA_GROUP)] for h in range(2)]
    for h in range(2):
        for g in range(GQA_GROUP):
            m = functools.reduce(jnp.maximum, [jnp.max(s, axis=0, keepdims=True) for s in scores[h][g]])
            o2 = sum(_nn(v, jnp.exp2(s - m).astype(BF16)) for v, s in zip(v_tiles[h:h + 3], scores[h][g]))
            o = _pick_head_rows(o2[:KV_WIDTH], k_head) / o2[KV_WIDTH:KV_WIDTH + 1]
            gate = _sigmoid(gates_ref[2 * GQA_GROUP + g:2 * GQA_GROUP + g + 1, h * tq:(h + 1) * tq])
            rows, cols = slice(g * HEAD_DIM, (g + 1) * HEAD_DIM), slice(h * tq, (h + 1) * tq)
            ow_ref[rows, cols] = other_ref[rows, cols] + o * gate


def _win_attn_call(qt, kk, vt, gates, other):
    B, _, S = qt.shape
    tq = min(Q_TILE, S)
    assert WINDOW == 2 * tq and S % (2 * tq) == 0
    k_spec = lambda back: pl.BlockSpec((None, 2 * tq, KV_WIDTH),
                                       lambda b, k, i: (b, jnp.maximum(i - back, 0), KK_COLBLK_WIN))
    v_spec = lambda back: pl.BlockSpec((None, KV_WIDTH, 2 * tq), lambda b, k, i: (b, 1, jnp.maximum(i - back, 0)))
    return pl.pallas_call(
        _win_attn_kernel,
        grid=(B, N_KV_HEADS, S // (2 * tq)),
        in_specs=[pl.BlockSpec((None, GROUP_WIDTH, 2 * tq), lambda b, k, i: (b, k, i)),
                  k_spec(1), k_spec(0), v_spec(1), v_spec(0),
                  pl.BlockSpec((None, None, GATE_ROWS, 2 * tq), lambda b, k, i: (b, k, 0, i)),
                  pl.BlockSpec((None, GROUP_WIDTH, 2 * tq), lambda b, k, i: (b, k, i))],
        out_specs=pl.BlockSpec((None, GROUP_WIDTH, 2 * tq), lambda b, k, i: (b, k, i)),
        out_shape=jax.ShapeDtypeStruct((B, ATTN_WIDTH, S), F32),
        compiler_params=_params("arbitrary", "arbitrary", "arbitrary"),
        name="win_attn",
    )(qt, kk, kk, vt, vt, gates, other)


def _group_rms(a, gain):
    c, tm = a.shape
    a3 = a.reshape(c // HEAD_DIM, HEAD_DIM, tm)
    ms = jnp.mean(a3 * a3, axis=1, keepdims=True)
    return (a3 * lax.rsqrt(ms + LN_EPS) * gain).reshape(c, tm)


def _out_proj_kernel(attn_ref, mlp_ref, x_ref, g1_ref, og_ref, wo_ref, lng_ref, lnb_ref, o_ref, *, alpha):
    n_attn = ATTN_WIDTH // HEAD_DIM
    attn = _group_rms(attn_ref[...].astype(F32), og_ref[:n_attn])
    mlp = _group_rms(mlp_ref[...].astype(F32), og_ref[n_attn:])
    y_t = jnp.concatenate([attn, mlp], axis=0).astype(BF16)
    y = _tn(y_t, wo_ref[...])
    z = alpha * x_ref[...] + (1.0 + g1_ref[...]) * y
    o_ref[...] = _layer_norm_rows(z, lng_ref[...], lnb_ref[...])


def _out_proj_call(attn, mlp, x, g1, out_g, w_o, ln_g, ln_b, *, alpha):
    B, S, D = x.shape
    tm = min(2 * TOK_TILE, S)
    cm = lambda width: pl.BlockSpec((None, width, tm), lambda b, i: (b, 0, i))
    row = pl.BlockSpec((1, D), lambda b, i: (0, 0))
    return pl.pallas_call(
        functools.partial(_out_proj_kernel, alpha=alpha),
        grid=(B, S // tm),
        in_specs=[cm(ATTN_WIDTH), cm(MLP_WIDTH),
                  pl.BlockSpec((None, tm, D), lambda b, i: (b, i, 0)),
                  pl.BlockSpec((None, 1, D), lambda b, i: (b, 0, 0)),
                  pl.BlockSpec(out_g.shape, lambda b, i: (0, 0, 0)),
                  pl.BlockSpec(w_o.shape, lambda b, i: (0, 0)),
                  row, row],
        out_specs=pl.BlockSpec((None, tm, D), lambda b, i: (b, i, 0)),
        out_shape=jax.ShapeDtypeStruct((B, S, D), F32),
        compiler_params=_params("arbitrary", "arbitrary"),
        name="out_proj",
    )(attn, mlp, x, g1, out_g, w_o, ln_g, ln_b)


def _ffn_kernel(x_ref, sc_ref, sh_ref, g2_ref, w1_ref, w3_ref, w2_ref, lng_ref, lnb_ref, o_ref, *, alpha, splits):
    x = x_ref[...]
    h = (x * (1.0 + sc_ref[...]) + sh_ref[...]).astype(BF16)
    f = None
    for lo, hi in splits:
        a = _nn(h, w1_ref[:, lo:hi])
        b = _nn(h, w3_ref[:, lo:hi])
        part = _nn((a * _sigmoid(a) * b).astype(BF16), w2_ref[lo:hi, :])
        f = part if f is None else f + part
    z = alpha * x + (1.0 + g2_ref[...]) * f
    o_ref[...] = _layer_norm_rows(z, lng_ref[...], lnb_ref[...])


def _ffn_call(x, sc, sh, g2, w1, w3, w2, ln_g, ln_b, *, alpha):
    B, S, D = x.shape
    d_ff = w1.shape[1]
    tm = min(TOK_TILE, S)
    half = (d_ff // 2 + 255) // 256 * 256
    splits = ((0, half), (half, d_ff))
    mod = pl.BlockSpec((None, 1, D), lambda b, i: (b, 0, 0))
    row = pl.BlockSpec((1, D), lambda b, i: (0, 0))
    resident = lambda shape: pl.BlockSpec(shape, lambda b, i: (0, 0), pipeline_mode=pl.Buffered(1))
    return pl.pallas_call(
        functools.partial(_ffn_kernel, alpha=alpha, splits=splits),
        grid=(B, S // tm),
        in_specs=[pl.BlockSpec((None, tm, D), lambda b, i: (b, i, 0)), mod, mod, mod,
                  resident(w1.shape), resident(w3.shape), resident(w2.shape), row, row],
        out_specs=pl.BlockSpec((None, tm, D), lambda b, i: (b, i, 0)),
        out_shape=jax.ShapeDtypeStruct((B, S, D), F32),
        compiler_params=_params("arbitrary", "arbitrary"),
        name="ffn",
    )(x, sc, sh, g2, w1, w3, w2, ln_g, ln_b)


def _in_proj_weights(w_in):
    sizes = (ATTN_WIDTH,) + (KV_WIDTH,) * 6 + (N_BRANCH * N_ATTN_HEADS, MLP_WIDTH, MLP_WIDTH)
    offs = [0]
    for s in sizes:
        offs.append(offs[-1] + s)
    col = lambda n: w_in[:, offs[n]:offs[n + 1]]
    q, kc, vc, ksl, vsl, kwn, vwn, gt, u, v = (col(n) for n in range(10))
    w_a = jnp.concatenate([ksl, kwn, kc, vc], axis=1).astype(BF16)
    D = w_in.shape[0]
    gt = gt.reshape(D, N_KV_HEADS, GQA_GROUP, N_BRANCH).transpose(1, 3, 2, 0)
    gt = gt.reshape(N_KV_HEADS, N_BRANCH * GQA_GROUP, D)
    gt = jnp.pad(gt, ((0, 0), (0, GATE_ROWS - N_BRANCH * GQA_GROUP), (0, 0)))
    gt = gt.reshape(N_KV_HEADS * GATE_ROWS, D)
    w_t = jnp.concatenate([u.T, v.T, q.T, vsl.T, vwn.T, gt], axis=0).astype(BF16)
    return w_a, w_t


def _compress_weights(cmp_w1):
    two, _, hid = cmp_w1.shape
    w = cmp_w1.reshape(two, 2, CMP_STRIDE, HEAD_DIM, hid)
    eye = jnp.eye(N_KV_HEADS, dtype=cmp_w1.dtype)
    big = jnp.einsum('thpdc,kj->tpkdjhc', w, eye)
    return big.reshape(two, CMP_STRIDE * N_KV_HEADS * HEAD_DIM, N_KV_HEADS * 2 * hid).astype(BF16)


def _hybrid_layer(x, mod, w_in, cmp_pos, cmp_w1, cmp_w2, vn_g, vn_b, w_s, b_s, out_g, w_o, ln1_g, ln1_b,
                  w1, w3, w2, ln2_g, ln2_b, *, alpha):
    B, S, D = x.shape
    assert S % Q_TILE == 0 and S % TOK_TILE == 0 and S // SLC_LEN <= MAX_SLC_BLOCKS
    sh1, sc1, g1, sh2, sc2, g2 = (mod[:, None, n * D:(n + 1) * D] for n in range(6))
    w_a, w_t = _in_proj_weights(w_in)
    kk, kvc, qt, vt, gates, mlp = _in_proj_call(
        x, sc1, sh1, w_a, w_t,
        vn_g.reshape(N_MLP_GROUPS, MLP_GROUP_DIM, 1), vn_b.reshape(N_MLP_GROUPS, MLP_GROUP_DIM, 1),
        jnp.swapaxes(w_s, 1, 2), b_s)
    ngrp = S // CMP_STRIDE
    tok = kvc.reshape(2, B, ngrp, CMP_STRIDE * KV_WIDTH)
    pos = jnp.broadcast_to(cmp_pos.reshape(2, 1, CMP_LEN * HEAD_DIM), (2, 8, CMP_LEN * HEAD_DIM))
    cmp_kv, cmp_kv_t = _compress_call(tok, _compress_weights(cmp_w1), cmp_w1.astype(BF16), pos, cmp_w2.astype(BF16))
    n_slc = S // SLC_LEN
    oc, bias = _cmp_attn_call(qt, cmp_kv, cmp_kv_t, gates, n_slc=n_slc, topk=min(SLC_TOPK, n_slc))
    ocw = _win_attn_call(qt, kk, vt, gates, oc)
    attn = _slc_attn_call(qt, bias, kk, vt, gates, ocw)
    row = lambda a: a.reshape(1, D)
    x = _out_proj_call(attn, mlp, x, g1, out_g.reshape(-1, HEAD_DIM, 1), w_o.astype(BF16),
                       row(ln1_g), row(ln1_b), alpha=alpha)
    return _ffn_call(x, sc2, sh2, g2, w1.astype(BF16), w3.astype(BF16), w2.astype(BF16),
                     row(ln2_g), row(ln2_b), alpha=alpha)


def kernel(x, c, w_ada, b_ada, w_in, cmp_pos, cmp_w1, cmp_w2, vn_g, vn_b, w_s, b_s, out_g, w_o, ln1_g, ln1_b,
           w1, w3, w2, ln2_g, ln2_b):
    depth = w_ada.shape[0]
    alpha = (2.0 * depth) ** 0.25
    mod = _ada_call(c, w_ada, b_ada)
    for l in range(depth):
        x = _hybrid_layer(x, mod[l], w_in[l], cmp_pos[l], cmp_w1[l], cmp_w2[l], vn_g[l], vn_b[l], w_s[l], b_s[l],
                          out_g[l], w_o[l], ln1_g[l], ln1_b[l], w1[l], w3[l], w2[l], ln2_g[l], ln2_b[l], alpha=alpha)
    return x
```
